```python
import math
import jax, jax.numpy as jnp
from jax import lax
import numpy as np

D_MODEL = 1024
BATCH = 1
SEQ = 16384
DEPTH = 2
DEC_BATCH = 32
DEC_SEQ = 16
PAST_LEN = 4096

CHUNK = 64
N_A = DEPTH // 2
N_B = DEPTH - N_A
N_DENSE = (DEPTH + 1) // 2
N_MOE = DEPTH // 2
POOL_WINDOWS = (2, 4, 8, 16)
N_POOL_GROUPS = 4
POOL_GROUP = D_MODEL // N_POOL_GROUPS
POOL_STATE = max(POOL_WINDOWS) - 1
HEAD_DIM = 64
N_HEADS = D_MODEL // (2 * HEAD_DIM)
V_DIM = 2 * HEAD_DIM
D_FF = ((8 * D_MODEL // 3 + 127) // 128) * 128
N_EXPERTS = 8
TOP_K = 2
N_BUCKETS = 32
MAX_DISTANCE = 128
Q_BLOCK = 128
EPS = 1e-6

kernel_name = "yoco_pool_diffattn_stream_step"


def rmsnorm(x, g):
    xf = x.astype(jnp.float32)
    y = xf * lax.rsqrt(jnp.mean(xf * xf, axis=-1, keepdims=True) + EPS)
    return (y * g.astype(jnp.float32)).astype(x.dtype)


def pool_mix(h, prev, start_pos, w_pool, scale):
    B, L, D = h.shape
    ext = jnp.concatenate([prev.astype(h.dtype), h], axis=1)
    c = jnp.cumsum(ext.astype(jnp.float32), axis=1)
    c = jnp.pad(c, ((0, 0), (1, 0), (0, 0)))
    t = jnp.arange(L)
    hi = c[:, POOL_STATE + 1:POOL_STATE + 1 + L]
    means = []
    for g, w in enumerate(POOL_WINDOWS):
        sl = slice(g * POOL_GROUP, (g + 1) * POOL_GROUP)
        lo = c[:, POOL_STATE + 1 - w:POOL_STATE + 1 - w + L, sl]
        cnt = jnp.minimum(w, start_pos + t + 1).astype(jnp.float32)
        means.append((hi[..., sl] - lo) / cnt[None, :, None])
    pooled = jnp.concatenate(means, axis=-1) - h.astype(jnp.float32)
    pooled = pooled.astype(h.dtype).reshape(B, L, N_POOL_GROUPS, POOL_GROUP)
    out = jnp.einsum('blgc,gcd->blgd', pooled, w_pool).reshape(B, L, D) * scale
    return out, ext[:, -POOL_STATE:]


def swiglu(h, wg, wu, wd):
    return (jax.nn.silu(h @ wg) * (h @ wu)) @ wd


def moe_swiglu(h, w_router, wg, wu, wd):
    B, L, D = h.shape
    hf = h.reshape(B * L, D)
    logits = (hf @ w_router).astype(jnp.float32)
    top_v, top_i = lax.top_k(logits, TOP_K)
    gates = jax.nn.softmax(top_v, axis=-1)
    y = jnp.zeros((B * L, D), jnp.float32)
    for e in range(N_EXPERTS):
        ge = jnp.sum(jnp.where(top_i == e, gates, 0.0), axis=-1)
        y = y + ge[:, None] * swiglu(hf, wg[e], wu[e], wd[e]).astype(jnp.float32)
    return y.astype(h.dtype).reshape(B, L, D)


def rel_bucket(rel):
    nb = N_BUCKETS // 2
    ret = jnp.where(rel > 0, nb, 0)
    n = jnp.abs(rel)
    max_exact = nb // 2
    large = max_exact + (jnp.log(jnp.maximum(n, 1).astype(jnp.float32) / max_exact)
                         / math.log(MAX_DISTANCE / max_exact) * (nb - max_exact)).astype(jnp.int32)
    large = jnp.minimum(large, nb - 1)
    return ret + jnp.where(n < max_exact, n, large)


def diff_attend(q, k, v, q_pos, k_pos, rel_bias, lam, lambda_init, g_subln):
    rel = k_pos[None, :] - q_pos[:, None]
    bias = jnp.transpose(rel_bias[rel_bucket(rel)], (2, 0, 1)).astype(jnp.float32)
    mask = (k_pos[None, :] // CHUNK) <= (q_pos[:, None] // CHUNK)
    s = jnp.einsum('bqhcd,bkhcd->bhcqk', q, k).astype(jnp.float32) + bias[None, :, None]
    s = jnp.where(mask[None, None, None], s, -jnp.inf)
    p = jax.nn.softmax(s, axis=-1)
    a = p[:, :, 0] - lam * p[:, :, 1]
    o = jnp.einsum('bhqk,bkhe->bqhe', a.astype(v.dtype), v)
    return rmsnorm(o, g_subln) * (1.0 - lambda_init)


def trunk(x, pool_prev, k_past, v_past, start_pos,
          g_pool_norm, w_pool, pool_scale,
          g_attn, w_q, g_qn, lambda_q1, lambda_k1, lambda_q2, lambda_k2, g_subln, w_o,
          g_kv, w_k, w_v, g_kn, rel_bias,
          g_ffn, w_gate_dense, w_up_dense, w_down_dense,
          w_router, w_gate_moe, w_up_moe, w_down_moe):
    B, L, D = x.shape
    new_pool = []
    k_new = v_new = k_all = v_all = None
    for layer in range(DEPTH):
        if layer < N_A:
            h = rmsnorm(x, g_pool_norm[layer])
            mix, st = pool_mix(h, pool_prev[layer], start_pos, w_pool[layer], pool_scale[layer])
            x = x + mix
            new_pool.append(st)
        else:
            if k_new is None:
                hkv = rmsnorm(x, g_kv)
                k_new = rmsnorm((hkv @ w_k).reshape(B, L, N_HEADS, 2, HEAD_DIM), g_kn)
                v_new = (hkv @ w_v).reshape(B, L, N_HEADS, V_DIM)
                if k_past is None:
                    k_all, v_all = k_new, v_new
                else:
                    k_all = jnp.concatenate([k_past.astype(k_new.dtype), k_new], axis=1)
                    v_all = jnp.concatenate([v_past.astype(v_new.dtype), v_new], axis=1)
            j = layer - N_A
            lambda_init = 0.8 - 0.6 * math.exp(-0.3 * layer)
            lam = (jnp.exp(jnp.sum(lambda_q1[j].astype(jnp.float32) * lambda_k1[j].astype(jnp.float32)))
                   - jnp.exp(jnp.sum(lambda_q2[j].astype(jnp.float32) * lambda_k2[j].astype(jnp.float32)))
                   + lambda_init)
            hq = rmsnorm(x, g_attn[j])
            q = rmsnorm((hq @ w_q[j]).reshape(B, L, N_HEADS, 2, HEAD_DIM), g_qn[j]) * (HEAD_DIM ** -0.5)
            k_pos = jnp.arange(k_all.shape[1])
            if k_past is None:
                nb = L // Q_BLOCK
                qb = jnp.moveaxis(q.reshape(B, nb, Q_BLOCK, N_HEADS, 2, HEAD_DIM), 1, 0)
                qpb = jnp.arange(L).reshape(nb, Q_BLOCK)
                ob = lax.map(lambda a: diff_attend(a[0], k_all, v_all, a[1], k_pos, rel_bias,
                                                   lam, lambda_init, g_subln[j]), (qb, qpb))
                o = jnp.moveaxis(ob, 0, 1).reshape(B, L, D)
            else:
                q_pos = start_pos + jnp.arange(L)
                o = diff_attend(q, k_all, v_all, q_pos, k_pos, rel_bias,
                                lam, lambda_init, g_subln[j]).reshape(B, L, D)
            x = x + o @ w_o[j]
        h = rmsnorm(x, g_ffn[layer])
        if layer % 2 == 0:
            i = layer // 2
            x = x + swiglu(h, w_gate_dense[i], w_up_dense[i], w_down_dense[i])
        else:
            i = layer // 2
            x = x + moe_swiglu(h, w_router[i], w_gate_moe[i], w_up_moe[i], w_down_moe[i])
    return x, k_new, v_new, jnp.stack(new_pool, axis=0)


def setup_inputs(seed: int = 0) -> dict:
    key = jax.random.key(seed)
    ks = jax.random.split(key, 40)
    f32 = jnp.float32
    nrm = lambda k, shape, s: jax.random.normal(k, shape, f32) * s
    gain = lambda k, shape: 1.0 + 0.02 * jax.random.normal(k, shape, f32)
    return {
        "x_prompt": nrm(ks[0], (BATCH, SEQ, D_MODEL), 1.0),
        "x_sample": nrm(ks[1], (DEC_BATCH, DEC_SEQ, D_MODEL), 1.0),
        "cache_k": nrm(ks[2], (DEC_BATCH, PAST_LEN, N_HEADS, 2, HEAD_DIM), 1.0),
        "cache_v": nrm(ks[3], (DEC_BATCH, PAST_LEN, N_HEADS, V_DIM), 1.0),
        "state_pool": nrm(ks[4], (N_A, DEC_BATCH, POOL_STATE, D_MODEL), 1.0),
        "g_pool_norm": gain(ks[5], (N_A, D_MODEL)),
        "w_pool": nrm(ks[6], (N_A, N_POOL_GROUPS, POOL_GROUP, POOL_GROUP), POOL_GROUP ** -0.5),
        "pool_scale": gain(ks[7], (N_A, D_MODEL)),
        "g_attn": gain(ks[8], (N_B, D_MODEL)),
        "w_q": nrm(ks[9], (N_B, D_MODEL, D_MODEL), D_MODEL ** -0.5),
        "g_qn": gain(ks[10], (N_B, HEAD_DIM)),
        "lambda_q1": nrm(ks[11], (N_B, HEAD_DIM), 0.1),
        "lambda_k1": nrm(ks[12], (N_B, HEAD_DIM), 0.1),
        "lambda_q2": nrm(ks[13], (N_B, HEAD_DIM), 0.1),
        "lambda_k2": nrm(ks[14], (N_B, HEAD_DIM), 0.1),
        "g_subln": gain(ks[15], (N_B, V_DIM)),
        "w_o": nrm(ks[16], (N_B, D_MODEL, D_MODEL), D_MODEL ** -0.5),
        "g_kv": gain(ks[17], (D_MODEL,)),
        "w_k": nrm(ks[18], (D_MODEL, D_MODEL), D_MODEL ** -0.5),
        "w_v": nrm(ks[19], (D_MODEL, D_MODEL), D_MODEL ** -0.5),
        "g_kn": gain(ks[20], (HEAD_DIM,)),
        "rel_bias": nrm(ks[21], (N_BUCKETS, N_HEADS), 0.5),
        "g_ffn": gain(ks[22], (DEPTH, D_MODEL)),
        "w_gate_dense": nrm(ks[23], (N_DENSE, D_MODEL, D_FF), D_MODEL ** -0.5),
        "w_up_dense": nrm(ks[24], (N_DENSE, D_MODEL, D_FF), D_MODEL ** -0.5),
        "w_down_dense": nrm(ks[25], (N_DENSE, D_FF, D_MODEL), D_FF ** -0.5),
        "w_router": nrm(ks[26], (N_MOE, D_MODEL, N_EXPERTS), D_MODEL ** -0.5),
        "w_gate_moe": nrm(ks[27], (N_MOE, N_EXPERTS, D_MODEL, D_FF), D_MODEL ** -0.5),
        "w_up_moe": nrm(ks[28], (N_MOE, N_EXPERTS, D_MODEL, D_FF), D_MODEL ** -0.5),
        "w_down_moe": nrm(ks[29], (N_MOE, N_EXPERTS, D_FF, D_MODEL), D_FF ** -0.5),
    }


def reference(x_prompt, x_sample, cache_k, cache_v, state_pool,
              g_pool_norm, w_pool, pool_scale,
              g_attn, w_q, g_qn, lambda_q1, lambda_k1, lambda_q2, lambda_k2, g_subln, w_o,
              g_kv, w_k, w_v, g_kn, rel_bias,
              g_ffn, w_gate_dense, w_up_dense, w_down_dense,
              w_router, w_gate_moe, w_up_moe, w_down_moe):
    weights = (g_pool_norm, w_pool, pool_scale,
               g_attn, w_q, g_qn, lambda_q1, lambda_k1, lambda_q2, lambda_k2, g_subln, w_o,
               g_kv, w_k, w_v, g_kn, rel_bias,
               g_ffn, w_gate_dense, w_up_dense, w_down_dense,
               w_router, w_gate_moe, w_up_moe, w_down_moe)
    pool_zero = jnp.zeros((N_A, x_prompt.shape[0], POOL_STATE, D_MODEL), x_prompt.dtype)
    y_prompt, k_prompt, v_prompt, pool_prompt = trunk(x_prompt, pool_zero, None, None, 0, *weights)
    y_sample, k_sample, v_sample, pool_sample = trunk(x_sample, state_pool, cache_k, cache_v,
                                                      PAST_LEN, *weights)
    return (y_prompt, y_sample, k_prompt, v_prompt, pool_prompt, k_sample, v_sample, pool_sample)
```

```python
import functools
import math

import numpy as np
import jax
import jax.numpy as jnp
from jax import lax
from jax.experimental import pallas as pl
from jax.experimental.pallas import tpu as pltpu

EPS = 1e-6
CHUNK = 64
POOL_WINDOWS = (2, 4, 8, 16)
POOL_STATE = max(POOL_WINDOWS) - 1
HEAD_DIM = 64
V_DIM = 2 * HEAD_DIM
N_EXPERTS = 8
TOP_K = 2
MAX_EXACT = 8
BUCKET_UPPER = (1, 2, 3, 4, 5, 6, 7, 8, 12, 16, 23, 32, 46, 64, 91)
FAR_DISTANCE = 128
NEG = -1e30

ROW_TILE = 512
LANE = 128
VMEM_LIMIT = 56 * 1024 * 1024

F32 = jnp.float32
BF16 = jnp.bfloat16


def _dot(a, b):
    return jnp.dot(a, b, preferred_element_type=F32)


def _dot_nt(a, b):
    return lax.dot_general(a, b, (((1,), (1,)), ((), ())), preferred_element_type=F32)


def _rms_unit(x):
    return x * lax.rsqrt(jnp.mean(x * x, axis=-1, keepdims=True) + EPS)


def _rel_bias_tile(rel, tab):
    n = jnp.abs(rel)
    neg = tab(15)
    pos = tab(31)
    for b in range(14, -1, -1):
        lt = n < BUCKET_UPPER[b]
        neg = jnp.where(lt, tab(b), neg)
        pos = jnp.where(lt, tab(16 + b), pos)
    return jnp.where(rel > 0, pos, neg)


def _params(*sem):
    return pltpu.CompilerParams(dimension_semantics=sem, vmem_limit_bytes=VMEM_LIMIT)


def _pool_kernel(xp_ref, xs_ref, st_ref, g_ref, w_ref, sc_ref,
                 x1_ref, pp_ref, ps_ref, ext_ref, ext3_ref, *, past_len):
    i = pl.program_id(0)
    nt = pl.num_programs(0) - 1
    T, D = xp_ref.shape
    gw = D // len(POOL_WINDOWS)

    @pl.when(i < nt)
    def _prompt():
        x = xp_ref[...]
        h = _rms_unit(x) * g_ref[...]

        @pl.when(i == 0)
        def _():
            ext_ref[0:16, :] = jnp.zeros((16, D), F32)

        ext_ref[16:16 + T, :] = h
        row = i * T + lax.broadcasted_iota(jnp.int32, (T, 1), 0)
        parts = []
        for gi, w in enumerate(POOL_WINDOWS):
            c0 = gi * gw
            s = ext_ref[16:16 + T, c0:c0 + gw]
            for j in range(1, w):
                s = s + ext_ref[16 - j:16 - j + T, c0:c0 + gw]
            cnt = jnp.minimum(w, row + 1).astype(F32)
            pooled = s / cnt - h[:, c0:c0 + gw]
            parts.append(_dot(pooled.astype(BF16), w_ref[gi]))
        mix = jnp.concatenate(parts, axis=-1) * sc_ref[...]
        x1_ref[...] = x + mix
        tail = ext_ref[T:T + 16, :]
        ext_ref[0:16, :] = tail

        @pl.when(i == nt - 1)
        def _():
            pp_ref[...] = tail[1:16, :]

    @pl.when(i == nt)
    def _sample():
        B = st_ref.shape[0]
        L = T // B
        x = xs_ref[...]
        h = _rms_unit(x) * g_ref[...]
        ext3_ref[:, 1:16, :] = st_ref[...]
        ext3_ref[:, 16:16 + L, :] = h.reshape(B, L, D)
        t = lax.broadcasted_iota(jnp.int32, (1, L, 1), 1)
        parts = []
        for gi, w in enumerate(POOL_WINDOWS):
            c0 = gi * gw
            s = ext3_ref[:, 16:16 + L, c0:c0 + gw]
            for j in range(1, w):
                s = s + ext3_ref[:, 16 - j:16 - j + L, c0:c0 + gw]
            cnt = jnp.minimum(w, past_len + t + 1).astype(F32)
            pooled = (s / cnt).reshape(T, gw) - h[:, c0:c0 + gw]
            parts.append(_dot(pooled.astype(BF16), w_ref[gi]))
        mix = jnp.concatenate(parts, axis=-1) * sc_ref[...]
        x1_ref[...] = x + mix
        ps_ref[...] = ext3_ref[:, 16 + L - POOL_STATE:16 + L, :]


def _pool_layer(xp, xs, state, g, w, sc, past_len):
    Lp, D = xp.shape
    T = ROW_TILE
    nt = Lp // T
    B = state.shape[0]
    L = xs.shape[0] // B
    return pl.pallas_call(
        functools.partial(_pool_kernel, past_len=past_len),
        grid=(nt + 1,),
        in_specs=[
            pl.BlockSpec((T, D), lambda i: (jnp.minimum(i, nt - 1), 0)),
            pl.BlockSpec((T, D), lambda i: (0, 0)),
            pl.BlockSpec((B, POOL_STATE, D), lambda i: (0, 0, 0)),
            pl.BlockSpec((1, D), lambda i: (0, 0)),
            pl.BlockSpec(w.shape, lambda i: (0, 0, 0)),
            pl.BlockSpec((1, D), lambda i: (0, 0)),
        ],
        out_specs=[
            pl.BlockSpec((T, D), lambda i: (i, 0)),
            pl.BlockSpec((POOL_STATE, D), lambda i: (0, 0)),
            pl.BlockSpec((B, POOL_STATE, D), lambda i: (0, 0, 0)),
        ],
        out_shape=[
            jax.ShapeDtypeStruct((Lp + T, D), F32),
            jax.ShapeDtypeStruct((POOL_STATE, D), F32),
            jax.ShapeDtypeStruct((B, POOL_STATE, D), F32),
        ],
        scratch_shapes=[
            pltpu.VMEM((16 + T, D), F32),
            pltpu.VMEM((B, 16 + L, D), F32),
        ],
        compiler_params=_params("arbitrary"),
        name="pool_mixer",
    )(xp, xs, state, g, w, sc)


def _swiglu_tile(hb, wg, wu, wd):
    gt = _dot(hb, wg)
    ut = _dot(hb, wu)
    a = gt * jax.nn.sigmoid(gt) * ut
    return _dot(a.astype(BF16), wd)


def _ffn_kernel(x_ref, g_ref, wg_ref, wu_ref, wd_ref, out_ref, hb_ref, acc_ref):
    f = pl.program_id(1)

    @pl.when(f == 0)
    def _():
        x = x_ref[...]
        hb_ref[...] = (_rms_unit(x) * g_ref[...]).astype(BF16)
        acc_ref[...] = x

    acc_ref[...] += _swiglu_tile(hb_ref[...], wg_ref[...], wu_ref[...], wd_ref[...])

    @pl.when(f == pl.num_programs(1) - 1)
    def _():
        out_ref[...] = acc_ref[...]


def _ffn_splits(d_ff):
    nf = 2 if (d_ff // 2) % LANE == 0 else 1
    return nf, d_ff // nf


def _dense_ffn(x, g, wg, wu, wd):
    N, D = x.shape
    T = ROW_TILE
    nf, tf = _ffn_splits(wg.shape[1])
    return pl.pallas_call(
        _ffn_kernel,
        grid=(N // T, nf),
        in_specs=[
            pl.BlockSpec((T, D), lambda i, f: (i, 0)),
            pl.BlockSpec((1, D), lambda i, f: (0, 0)),
            pl.BlockSpec((D, tf), lambda i, f: (0, f)),
            pl.BlockSpec((D, tf), lambda i, f: (0, f)),
            pl.BlockSpec((tf, D), lambda i, f: (f, 0)),
        ],
        out_specs=pl.BlockSpec((T, D), lambda i, f: (i, 0)),
        out_shape=jax.ShapeDtypeStruct((N, D), F32),
        scratch_shapes=[pltpu.VMEM((T, D), BF16), pltpu.VMEM((T, D), F32)],
        compiler_params=_params("arbitrary", "arbitrary"),
        name="dense_swiglu",
    )(x, g, wg, wu, wd)


def _qkv_kernel(x_ref, gkv_ref, gq_ref, wk_ref, wv_ref, wq_ref, gkn_ref, gqn_ref,
                gsum_ref, gbc_ref,
                kp_ref, vp_ref, ks_ref, vs_ref, qh_ref, kh_ref, vh_ref, qs_ref):
    i = pl.program_id(0)
    nt = pl.num_programs(0) - 1
    n_heads = qh_ref.shape[0]

    xn = _rms_unit(x_ref[...])
    hkv = (xn * gkv_ref[...]).astype(BF16)
    hq = (xn * gq_ref[...]).astype(BF16)

    def head_norm(y, g):
        ssq = _dot((y * y).astype(BF16), gsum_ref[...])
        rs = lax.rsqrt(ssq * (1.0 / HEAD_DIM) + EPS)
        rs_hi = rs.astype(BF16)
        rs_lo = (rs - rs_hi.astype(F32)).astype(BF16)
        rsb = _dot(rs_hi, gbc_ref[...]) + _dot(rs_lo, gbc_ref[...])
        return y * rsb * g

    k = head_norm(_dot(hkv, wk_ref[...]), gkn_ref[...])
    v = _dot(hkv, wv_ref[...])
    q = head_norm(_dot(hq, wq_ref[...]), gqn_ref[...])

    @pl.when(i < nt)
    def _():
        kp_ref[...] = k
        vp_ref[...] = v
        for h in range(n_heads):
            sl = slice(h * V_DIM, (h + 1) * V_DIM)
            qh_ref[h] = q[:, sl].astype(BF16)
            kh_ref[h] = k[:, sl].astype(BF16)
            vh_ref[h] = v[:, sl].astype(BF16)

    @pl.when(i == nt)
    def _():
        ks_ref[...] = k
        vs_ref[...] = v
        qs_ref[...] = q.astype(BF16)


def _qkv_proj(x, gkv, gq, wk, wv, wq, gkn_t, gqn_t, gsum, gbc, n_sample):
    N, D = x.shape
    T = ROW_TILE
    nt = N // T - 1
    Lp = N - n_sample
    n_heads = D // V_DIM
    const2 = lambda i: (0, 0)
    prow = lambda i: (jnp.minimum(i, nt - 1), 0)
    phead = lambda i: (0, jnp.minimum(i, nt - 1), 0)
    return pl.pallas_call(
        _qkv_kernel,
        grid=(nt + 1,),
        in_specs=[
            pl.BlockSpec((T, D), lambda i: (i, 0)),
            pl.BlockSpec((1, D), const2),
            pl.BlockSpec((1, D), const2),
            pl.BlockSpec((D, D), const2),
            pl.BlockSpec((D, D), const2),
            pl.BlockSpec((D, D), const2),
            pl.BlockSpec((1, D), const2),
            pl.BlockSpec((1, D), const2),
            pl.BlockSpec(gsum.shape, const2),
            pl.BlockSpec(gbc.shape, const2),
        ],
        out_specs=[
            pl.BlockSpec((T, D), prow),
            pl.BlockSpec((T, D), prow),
            pl.BlockSpec((T, D), const2),
            pl.BlockSpec((T, D), const2),
            pl.BlockSpec((n_heads, T, V_DIM), phead),
            pl.BlockSpec((n_heads, T, V_DIM), phead),
            pl.BlockSpec((n_heads, T, V_DIM), phead),
            pl.BlockSpec((T, D), const2),
        ],
        out_shape=[
            jax.ShapeDtypeStruct((Lp, D), F32),
            jax.ShapeDtypeStruct((Lp, D), F32),
            jax.ShapeDtypeStruct((n_sample, D), F32),
            jax.ShapeDtypeStruct((n_sample, D), F32),
            jax.ShapeDtypeStruct((n_heads, Lp, V_DIM), BF16),
            jax.ShapeDtypeStruct((n_heads, Lp, V_DIM), BF16),
            jax.ShapeDtypeStruct((n_heads, Lp, V_DIM), BF16),
            jax.ShapeDtypeStruct((n_sample, D), BF16),
        ],
        compiler_params=_params("arbitrary"),
        name="qkv_proj",
    )(x, gkv, gq, wk, wv, wq, gkn_t, gqn_t, gsum, gbc)


def _diff_out(o0, o1, lam, g, out_scale):
    o = o0 - lam * o1
    return _rms_unit(o) * g * out_scale


def _attn_prompt_kernel(bias_ref, lam_ref, q_ref, k_ref, v_ref, g_ref, o_ref,
                        bd_ref, bs_ref, m_ref, l_ref, acc_ref, *, out_scale):
    h = pl.program_id(0)
    i = pl.program_id(1)
    T = q_ref.shape[1]

    @pl.when(i == 0)
    def _():
        row = lax.broadcasted_iota(jnp.int32, (T, T), 0)
        col = lax.broadcasted_iota(jnp.int32, (T, T), 1)
        tab = lambda b: bias_ref[b, h]
        visible = (col // CHUNK) <= (row // CHUNK)
        bd_ref[...] = jnp.where(visible, _rel_bias_tile(col - row, tab), NEG)
        bs_ref[...] = _rel_bias_tile(col - row - T, tab)

    m_ref[...] = jnp.full(m_ref.shape, NEG, F32)
    l_ref[...] = jnp.zeros(l_ref.shape, F32)
    acc_ref[...] = jnp.zeros(acc_ref.shape, F32)
    q = q_ref[0]
    qc = (q[:, :HEAD_DIM], q[:, HEAD_DIM:])

    def update(j, bias):
        start = pl.multiple_of(j * T, T)
        kt = k_ref[0, pl.ds(start, T), :]
        vt = v_ref[0, pl.ds(start, T), :]
        for c in range(2):
            s = _dot_nt(qc[c], kt[:, c * HEAD_DIM:(c + 1) * HEAD_DIM]) + bias
            m_old = m_ref[c]
            m_new = jnp.maximum(m_old, jnp.max(s, axis=-1, keepdims=True))
            alpha = jnp.exp(m_old - m_new)
            p = jnp.exp(s - m_new)
            l_ref[c] = alpha * l_ref[c] + jnp.sum(p, axis=-1, keepdims=True)
            acc_ref[c] = alpha * acc_ref[c] + _dot(p.astype(BF16), vt)
            m_ref[c] = m_new

    far_bias = bias_ref[15, h]

    def far_body(j, carry):
        update(j, far_bias)
        return carry

    lax.fori_loop(0, jnp.maximum(i - 1, 0), far_body, 0)

    @pl.when(i >= 1)
    def _():
        update(i - 1, bs_ref[...])

    update(i, bd_ref[...])

    o0 = acc_ref[0] / l_ref[0]
    o1 = acc_ref[1] / l_ref[1]
    o_ref[...] = _diff_out(o0, o1, lam_ref[0], g_ref[...], out_scale).astype(o_ref.dtype)


def _attn_prompt(rel_bias, lam, qh, kh, vh, g_subln, out_scale):
    n_heads, Lp, _ = qh.shape
    T = ROW_TILE
    assert T % CHUNK == 0 and T >= FAR_DISTANCE
    return pl.pallas_call(
        functools.partial(_attn_prompt_kernel, out_scale=out_scale),
        grid=(n_heads, Lp // T),
        in_specs=[
            pl.BlockSpec(memory_space=pltpu.SMEM),
            pl.BlockSpec(memory_space=pltpu.SMEM),
            pl.BlockSpec((1, T, V_DIM), lambda h, i: (h, i, 0)),
            pl.BlockSpec((1, Lp, V_DIM), lambda h, i: (h, 0, 0)),
            pl.BlockSpec((1, Lp, V_DIM), lambda h, i: (h, 0, 0)),
            pl.BlockSpec((1, V_DIM), lambda h, i: (0, 0)),
        ],
        out_specs=pl.BlockSpec((T, V_DIM), lambda h, i: (i, h)),
        out_shape=jax.ShapeDtypeStruct((Lp, n_heads * V_DIM), BF16),
        scratch_shapes=[
            pltpu.VMEM((T, T), F32),
            pltpu.VMEM((T, T), F32),
            pltpu.VMEM((2, T, 1), F32),
            pltpu.VMEM((2, T, 1), F32),
            pltpu.VMEM((2, T, V_DIM), F32),
        ],
        compiler_params=_params("arbitrary", "arbitrary"),
        name="attn_prompt",
    )(rel_bias, lam, qh, kh, vh, g_subln)


def _attn_sample_kernel(lam_ref, qbd_ref, ck_ref, cv_ref, kn_ref, vn_ref, tab_ref, g_ref,
                        o_ref, nearb_ref, newb_ref, m_ref, l_ref, acc_ref, *, out_scale):
    b = pl.program_id(0)
    kb = pl.program_id(1)
    nkb = pl.num_programs(1)
    R, Tk = nearb_ref.shape
    L = kn_ref.shape[0]
    n_heads = o_ref.shape[1] // V_DIM

    @pl.when((b == 0) & (kb == 0))
    def _():
        tab = lambda bkt: tab_ref[:, bkt:bkt + 1]
        t_near = lax.broadcasted_iota(jnp.int32, (R, Tk), 0) % L
        col = lax.broadcasted_iota(jnp.int32, (R, Tk), 1)
        nearb_ref[...] = _rel_bias_tile(col - Tk - t_near, tab)
        t_new = lax.broadcasted_iota(jnp.int32, (R, L), 0) % L
        col_new = lax.broadcasted_iota(jnp.int32, (R, L), 1)
        newb_ref[...] = _rel_bias_tile(col_new - t_new, tab)

    @pl.when(kb == 0)
    def _():
        m_ref[...] = jnp.full(m_ref.shape, NEG, F32)
        l_ref[...] = jnp.zeros(l_ref.shape, F32)
        acc_ref[...] = jnp.zeros(acc_ref.shape, F32)

    def update(s, vb):
        m_old = m_ref[...]
        m_new = jnp.maximum(m_old, jnp.max(s, axis=-1, keepdims=True))
        alpha = jnp.exp(m_old - m_new)
        p = jnp.exp(s - m_new)
        l_ref[...] = alpha * l_ref[...] + jnp.sum(p, axis=-1, keepdims=True)
        acc_ref[...] = alpha * acc_ref[...] + _dot(p.astype(BF16), vb)
        m_ref[...] = m_new

    qbd = qbd_ref[0]
    s = _dot_nt(qbd, ck_ref[0].astype(BF16))
    vb = cv_ref[0].astype(BF16)

    @pl.when(kb < nkb - 1)
    def _():
        update(s + tab_ref[:, 15:16], vb)

    @pl.when(kb == nkb - 1)
    def _():
        update(s + nearb_ref[...], vb)
        s_new = _dot_nt(qbd, kn_ref[...].astype(BF16)) + newb_ref[...]
        update(s_new, vn_ref[...].astype(BF16))
        lam = lam_ref[0]
        for h in range(n_heads):
            r0 = h * 2 * L
            cs = slice(h * V_DIM, (h + 1) * V_DIM)
            o0 = acc_ref[r0:r0 + L, cs] / l_ref[r0:r0 + L, :]
            o1 = acc_ref[r0 + L:r0 + 2 * L, cs] / l_ref[r0 + L:r0 + 2 * L, :]
            o_ref[:, cs] = _diff_out(o0, o1, lam, g_ref[...], out_scale).astype(o_ref.dtype)


def _attn_sample(lam, qbd, cache_k, cache_v, k_new, v_new, tab, g_subln, out_scale):
    B, past, D = cache_k.shape
    R = qbd.shape[1]
    L = k_new.shape[0] // B
    Tk = min(1024, past)
    assert past % Tk == 0 and Tk >= FAR_DISTANCE + L
    return pl.pallas_call(
        functools.partial(_attn_sample_kernel, out_scale=out_scale),
        grid=(B, past // Tk),
        in_specs=[
            pl.BlockSpec(memory_space=pltpu.SMEM),
            pl.BlockSpec((1, R, D), lambda b, k: (b, 0, 0)),
            pl.BlockSpec((1, Tk, D), lambda b, k: (b, k, 0)),
            pl.BlockSpec((1, Tk, D), lambda b, k: (b, k, 0)),
            pl.BlockSpec((L, D), lambda b, k: (b, 0)),
            pl.BlockSpec((L, D), lambda b, k: (b, 0)),
            pl.BlockSpec(tab.shape, lambda b, k: (0, 0)),
            pl.BlockSpec((1, V_DIM), lambda b, k: (0, 0)),
        ],
        out_specs=pl.BlockSpec((L, D), lambda b, k: (b, 0)),
        out_shape=jax.ShapeDtypeStruct((B * L, D), BF16),
        scratch_shapes=[
            pltpu.VMEM((R, Tk), F32),
            pltpu.VMEM((R, L), F32),
            pltpu.VMEM((R, 1), F32),
            pltpu.VMEM((R, 1), F32),
            pltpu.VMEM((R, D), F32),
        ],
        compiler_params=_params("arbitrary", "arbitrary"),
        name="attn_sample",
    )(lam, qbd, cache_k, cache_v, k_new, v_new, tab, g_subln)


def _oproj_router_kernel(op_ref, os_ref, x_ref, wo_ref, g_ref, wrh_ref, wrl_ref,
                         x3_ref, h_ref, route_ref):
    i = pl.program_id(0)
    nt = pl.num_programs(0) - 1
    o = jnp.where(i == nt, os_ref[...], op_ref[...])
    x3 = x_ref[...] + _dot(o, wo_ref[...])
    x3_ref[...] = x3
    h = _rms_unit(x3) * g_ref[...]
    h_ref[...] = h
    h_hi = h.astype(BF16)
    h_lo = (h - h_hi.astype(F32)).astype(BF16)
    logits = _dot(h_hi, wrh_ref[...]) + (_dot(h_lo, wrh_ref[...]) + _dot(h_hi, wrl_ref[...]))
    lane = lax.broadcasted_iota(jnp.int32, logits.shape, 1)
    lg = jnp.where(lane < N_EXPERTS, logits, -jnp.inf)
    m1 = jnp.max(lg, axis=-1, keepdims=True)
    i1 = jnp.min(jnp.where(lg == m1, lane, LANE), axis=-1, keepdims=True)
    lg2 = jnp.where(lane == i1, -jnp.inf, lg)
    m2 = jnp.max(lg2, axis=-1, keepdims=True)
    i2 = jnp.min(jnp.where(lg2 == m2, lane, LANE), axis=-1, keepdims=True)
    e2 = jnp.exp(m2 - m1)
    den = 1.0 + e2
    g1 = 1.0 / den
    g2 = e2 / den
    route_ref[...] = jnp.where(lane == 0, i1.astype(F32),
                               jnp.where(lane == 1, i2.astype(F32),
                                         jnp.where(lane == 2, g1,
                                                   jnp.where(lane == 3, g2, 0.0))))


def _oproj_router(o_p, o_s, x, wo, g, wr_hi, wr_lo):
    N, D = x.shape
    T = ROW_TILE
    nt = N // T - 1
    const2 = lambda i: (0, 0)
    row = lambda i: (i, 0)
    return pl.pallas_call(
        _oproj_router_kernel,
        grid=(nt + 1,),
        in_specs=[
            pl.BlockSpec((T, D), lambda i: (jnp.minimum(i, nt - 1), 0)),
            pl.BlockSpec((T, D), const2),
            pl.BlockSpec((T, D), row),
            pl.BlockSpec((D, D), const2),
            pl.BlockSpec((1, D), const2),
            pl.BlockSpec((D, LANE), const2),
            pl.BlockSpec((D, LANE), const2),
        ],
        out_specs=[
            pl.BlockSpec((T, D), row),
            pl.BlockSpec((T, D), row),
            pl.BlockSpec((T, LANE), row),
        ],
        out_shape=[
            jax.ShapeDtypeStruct((N, D), F32),
            jax.ShapeDtypeStruct((N, D), F32),
            jax.ShapeDtypeStruct((N, LANE), F32),
        ],
        compiler_params=_params("arbitrary"),
        name="oproj_router",
    )(o_p, o_s, x, wo, g, wr_hi, wr_lo)


def _gather_rows(idx_ref, n, src_hbm, dst_ref, sem, stride=1, offset=0):
    def row_copy(r, src_row):
        return pltpu.make_async_copy(src_hbm.at[pl.ds(src_row, 1), :],
                                     dst_ref.at[pl.ds(r, 1), :], sem)

    def issue(r, carry):
        row_copy(r, idx_ref[0, 0, stride * r + offset]).start()
        return carry

    def drain(r, carry):
        row_copy(r, 0).wait()
        return carry

    lax.fori_loop(0, n, issue, 0)
    lax.fori_loop(0, n, drain, 0)


def _dispatch_kernel(idx_ref, src_hbm, out_ref, sem):
    _gather_rows(idx_ref, out_ref.shape[0], src_hbm, out_ref, sem)


def _dispatch(src_idx, h):
    P = src_idx.shape[0]
    D = h.shape[1]
    T = ROW_TILE
    return pl.pallas_call(
        _dispatch_kernel,
        grid=(P // T,),
        in_specs=[
            pl.BlockSpec((1, 1, T), lambda i: (i, 0, 0), memory_space=pltpu.SMEM),
            pl.BlockSpec(memory_space=pl.ANY),
        ],
        out_specs=pl.BlockSpec((T, D), lambda i: (i, 0)),
        out_shape=jax.ShapeDtypeStruct((P, D), F32),
        scratch_shapes=[pltpu.SemaphoreType.DMA(())],
        compiler_params=_params("arbitrary"),
        name="moe_dispatch",
    )(src_idx.reshape(P // T, 1, T), h)


def _experts_kernel(te_ref, na_ref, xs_ref, gate_ref, wg_ref, wu_ref, wd_ref, out_ref, acc_ref):
    i = pl.program_id(0)
    f = pl.program_id(1)
    last = pl.num_programs(1) - 1
    active = i < na_ref[0]

    @pl.when(active)
    def _():
        @pl.when(f == 0)
        def _():
            acc_ref[...] = jnp.zeros(acc_ref.shape, F32)

        acc_ref[...] += _swiglu_tile(xs_ref[...].astype(BF16), wg_ref[0], wu_ref[0], wd_ref[0])

        @pl.when(f == last)
        def _():
            out_ref[...] = acc_ref[...] * gate_ref[...]

    @pl.when(jnp.logical_not(active) & (f == last))
    def _():
        out_ref[...] = jnp.zeros(out_ref.shape, F32)


def _experts(tile_expert, n_active, xs, gate_sorted, wg, wu, wd):
    P, D = xs.shape
    T = ROW_TILE
    nf, tf = _ffn_splits(wg.shape[2])
    fidx = lambda i, f, na: jnp.where(i < na[0], f, nf - 1)
    return pl.pallas_call(
        _experts_kernel,
        grid_spec=pltpu.PrefetchScalarGridSpec(
            num_scalar_prefetch=2,
            grid=(P // T, nf),
            in_specs=[
                pl.BlockSpec((T, D), lambda i, f, te, na: (i, 0)),
                pl.BlockSpec((T, 1), lambda i, f, te, na: (i, 0)),
                pl.BlockSpec((1, D, tf), lambda i, f, te, na: (te[i], 0, fidx(i, f, na))),
                pl.BlockSpec((1, D, tf), lambda i, f, te, na: (te[i], 0, fidx(i, f, na))),
                pl.BlockSpec((1, tf, D), lambda i, f, te, na: (te[i], fidx(i, f, na), 0)),
            ],
            out_specs=pl.BlockSpec((T, D), lambda i, f, te, na: (i, 0)),
            scratch_shapes=[pltpu.VMEM((T, D), F32)],
        ),
        out_shape=jax.ShapeDtypeStruct((P, D), F32),
        compiler_params=_params("arbitrary", "arbitrary"),
        name="moe_experts",
    )(tile_expert, n_active, xs, gate_sorted, wg, wu, wd)


def _combine_kernel(pos_ref, x_ref, ys_hbm, yp_ref, ysm_ref, a_ref, b_ref, sem_a, sem_b):
    i = pl.program_id(0)
    nt = pl.num_programs(0) - 1
    T = x_ref.shape[0]
    _gather_rows(pos_ref, T, ys_hbm, a_ref, sem_a, stride=TOP_K, offset=0)
    _gather_rows(pos_ref, T, ys_hbm, b_ref, sem_b, stride=TOP_K, offset=1)
    y = x_ref[...] + (a_ref[...] + b_ref[...])

    @pl.when(i < nt)
    def _():
        yp_ref[...] = y

    @pl.when(i == nt)
    def _():
        ysm_ref[...] = y


def _combine(dest, x3, ys, n_sample):
    N, D = x3.shape
    T = ROW_TILE
    nt = N // T - 1
    return pl.pallas_call(
        _combine_kernel,
        grid=(nt + 1,),
        in_specs=[
            pl.BlockSpec((1, 1, TOP_K * T), lambda i: (i, 0, 0), memory_space=pltpu.SMEM),
            pl.BlockSpec((T, D), lambda i: (i, 0)),
            pl.BlockSpec(memory_space=pl.ANY),
        ],
        out_specs=[
            pl.BlockSpec((T, D), lambda i: (jnp.minimum(i, nt - 1), 0)),
            pl.BlockSpec((T, D), lambda i: (0, 0)),
        ],
        out_shape=[
            jax.ShapeDtypeStruct((N - n_sample, D), F32),
            jax.ShapeDtypeStruct((n_sample, D), F32),
        ],
        scratch_shapes=[
            pltpu.VMEM((T, D), F32),
            pltpu.VMEM((T, D), F32),
            pltpu.SemaphoreType.DMA(()),
            pltpu.SemaphoreType.DMA(()),
        ],
        compiler_params=_params("arbitrary"),
        name="moe_combine",
    )(dest.reshape(nt + 1, 1, TOP_K * T), x3, ys)


def _routing_tables(route, tile):
    n = route.shape[0]
    experts = route[:, :TOP_K].astype(jnp.int32).reshape(-1)
    gates = route[:, TOP_K:2 * TOP_K].reshape(-1)
    n_assign = n * TOP_K
    n_tiles = n_assign // tile + N_EXPERTS
    onehot = (experts[:, None] == jnp.arange(N_EXPERTS)[None, :]).astype(jnp.int32)
    csum = jnp.cumsum(onehot, axis=0)
    rank = jnp.sum(onehot * csum, axis=1) - 1
    counts = csum[-1]
    tiles_per = (counts + tile - 1) // tile
    tile_end = jnp.cumsum(tiles_per)
    group_off = (tile_end - tiles_per) * tile
    dest = group_off[experts] + rank
    src_idx = jnp.zeros((n_tiles * tile,), jnp.int32).at[dest].set(jnp.arange(n_assign, dtype=jnp.int32) // TOP_K)
    gate_sorted = jnp.zeros((n_tiles * tile,), F32).at[dest].set(gates)
    n_active = tile_end[-1:].astype(jnp.int32)
    tile_ids = jnp.minimum(jnp.arange(n_tiles, dtype=jnp.int32), n_active[0] - 1)
    tile_expert = jnp.sum((tile_ids[:, None] >= tile_end[None, :]).astype(jnp.int32), axis=1)
    return dest.astype(jnp.int32), src_idx, gate_sorted[:, None], tile_expert, n_active


def _group_matrices(d_model):
    n_groups = d_model // HEAD_DIM
    gsum = np.zeros((d_model, LANE), np.float32)
    gsum[np.arange(d_model), np.arange(d_model) // HEAD_DIM] = 1.0
    assert n_groups <= LANE
    return jnp.asarray(gsum, BF16), jnp.asarray(gsum.T.copy(), BF16)


def kernel(x_prompt, x_sample, cache_k, cache_v, state_pool, g_pool_norm, w_pool, pool_scale, g_attn, w_q, g_qn, lambda_q1, lambda_k1, lambda_q2, lambda_k2, g_subln, w_o, g_kv, w_k, w_v, g_kn, rel_bias, g_ffn, w_gate_dense, w_up_dense, w_down_dense, w_router, w_gate_moe, w_up_moe, w_down_moe):
    Bp, Lp, D = x_prompt.shape
    Bs, Ls, _ = x_sample.shape
    past = cache_k.shape[1]
    n_heads = D // V_DIM
    n_sample = Bs * Ls
    assert Bp == 1 and n_sample == ROW_TILE and Lp % ROW_TILE == 0
    assert g_pool_norm.shape[0] == 1 and g_attn.shape[0] == 1
    bf = lambda a: a.astype(BF16)
    row = lambda a: a.reshape(1, -1)

    x1, pool_p, pool_s = _pool_layer(
        x_prompt.reshape(Lp, D), x_sample.reshape(n_sample, D), state_pool[0],
        row(g_pool_norm[0]), bf(w_pool[0]), row(pool_scale[0]), past)
    x2 = _dense_ffn(x1, row(g_ffn[0]), bf(w_gate_dense[0]), bf(w_up_dense[0]), bf(w_down_dense[0]))

    layer = 1
    lambda_init = 0.8 - 0.6 * math.exp(-0.3 * layer)
    lam = (jnp.exp(jnp.sum(lambda_q1[0] * lambda_k1[0])) - jnp.exp(jnp.sum(lambda_q2[0] * lambda_k2[0]))
           + lambda_init).reshape(1)
    gsum, gbc = _group_matrices(D)
    n_groups = D // HEAD_DIM
    k_p, v_p, k_s, v_s, qh, kh, vh, q_s = _qkv_proj(
        x2, row(g_kv), row(g_attn[0]), bf(w_k), bf(w_v), bf(w_q[0]),
        row(jnp.tile(g_kn, n_groups)), row(jnp.tile(g_qn[0], n_groups) * (HEAD_DIM ** -0.5)),
        gsum, gbc, n_sample)
    out_scale = 1.0 - lambda_init
    g_sub = row(g_subln[0])
    o_p = _attn_prompt(rel_bias, lam, qh, kh, vh, g_sub, out_scale)

    q4 = q_s.reshape(Bs, Ls, n_groups, HEAD_DIM).transpose(0, 2, 1, 3)
    eye = jnp.eye(n_groups, dtype=BF16)
    qbd = (q4[:, :, :, None, :] * eye[None, :, None, :, None]).reshape(Bs, n_groups * Ls, D)
    tab = jnp.repeat(rel_bias.T, 2 * Ls, axis=0)
    o_s = _attn_sample(lam, qbd, cache_k.reshape(Bs, past, D), cache_v.reshape(Bs, past, D),
                       k_s, v_s, tab, g_sub, out_scale)

    wr = jnp.pad(w_router[0], ((0, 0), (0, LANE - N_EXPERTS)))
    wr_hi = bf(wr)
    wr_lo = bf(wr - wr_hi.astype(F32))
    x3, h_moe, route = _oproj_router(o_p, o_s, x2, bf(w_o[0]), row(g_ffn[1]), wr_hi, wr_lo)
    dest, src_idx, gate_sorted, tile_expert, n_active = _routing_tables(route, ROW_TILE)
    xs = _dispatch(src_idx, h_moe)
    ys = _experts(tile_expert, n_active, xs, gate_sorted,
                  bf(w_gate_moe[0]), bf(w_up_moe[0]), bf(w_down_moe[0]))
    y_p, y_s = _combine(dest, x3, ys, n_sample)

    return (y_p.reshape(Bp, Lp, D), y_s.reshape(Bs, Ls, D),
            k_p.reshape(Bp, Lp, n_heads, 2, HEAD_DIM), v_p.reshape(Bp, Lp, n_heads, V_DIM),
            pool_p.reshape(1, Bp, POOL_STATE, D),
            k_s.reshape(Bs, Ls, n_heads, 2, HEAD_DIM), v_s.reshape(Bs, Ls, n_heads, V_DIM),
            pool_s.reshape(1, Bs, POOL_STATE, D))
```

```python
import functools
import math

import numpy as np
import jax
import jax.numpy as jnp
from jax import lax
from jax.experimental import pallas as pl
from jax.experimental.pallas import tpu as pltpu

EPS = 1e-6
CHUNK = 64
POOL_WINDOWS = (2, 4, 8, 16)
POOL_STATE = max(POOL_WINDOWS) - 1
HEAD_DIM = 64
V_DIM = 2 * HEAD_DIM
N_EXPERTS = 8
TOP_K = 2
MAX_EXACT = 8
BUCKET_UPPER = (1, 2, 3, 4, 5, 6, 7, 8, 12, 16, 23, 32, 46, 64, 91)
FAR_DISTANCE = 128
NEG = -1e30
LOG2E = math.log2(math.e)
UNSHIFTED_SCORE_LIMIT = 80.0

ROW_TILE = 512
LANE = 128
VMEM_LIMIT = 56 * 1024 * 1024

F32 = jnp.float32
BF16 = jnp.bfloat16


def _dot(a, b):
    return jnp.dot(a, b, preferred_element_type=F32)


def _dot_nt(a, b):
    return lax.dot_general(a, b, (((1,), (1,)), ((), ())), preferred_element_type=F32)


def _rms_unit(x):
    return x * lax.rsqrt(jnp.mean(x * x, axis=-1, keepdims=True) + EPS)


def _rel_bias_tile(rel, tab):
    n = jnp.abs(rel)
    neg = tab(15)
    pos = tab(31)
    for b in range(14, -1, -1):
        lt = n < BUCKET_UPPER[b]
        neg = jnp.where(lt, tab(b), neg)
        pos = jnp.where(lt, tab(16 + b), pos)
    return jnp.where(rel > 0, pos, neg)


def _params(*sem):
    return pltpu.CompilerParams(dimension_semantics=sem, vmem_limit_bytes=VMEM_LIMIT)


def _pool_kernel(xp_ref, xs_ref, st_ref, g_ref, w_ref, sc_ref,
                 x1_ref, pp_ref, ps_ref, ext_ref, ext3_ref, *, past_len):
    i = pl.program_id(0)
    nt = pl.num_programs(0) - 1
    T, D = xp_ref.shape
    gw = D // len(POOL_WINDOWS)

    @pl.when(i < nt)
    def _prompt():
        x = xp_ref[...]
        h = _rms_unit(x) * g_ref[...]

        @pl.when(i == 0)
        def _():
            ext_ref[0:16, :] = jnp.zeros((16, D), F32)

        ext_ref[16:16 + T, :] = h
        row = i * T + lax.broadcasted_iota(jnp.int32, (T, 1), 0)
        parts = []
        for gi, w in enumerate(POOL_WINDOWS):
            c0 = gi * gw
            s = ext_ref[16:16 + T, c0:c0 + gw]
            for j in range(1, w):
                s = s + ext_ref[16 - j:16 - j + T, c0:c0 + gw]
            cnt = jnp.minimum(w, row + 1).astype(F32)
            pooled = s / cnt - h[:, c0:c0 + gw]
            parts.append(_dot(pooled.astype(BF16), w_ref[gi]))
        mix = jnp.concatenate(parts, axis=-1) * sc_ref[...]
        x1_ref[...] = x + mix
        tail = ext_ref[T:T + 16, :]
        ext_ref[0:16, :] = tail

        @pl.when(i == nt - 1)
        def _():
            pp_ref[...] = tail[1:16, :]

    @pl.when(i == nt)
    def _sample():
        B = st_ref.shape[0]
        L = T // B
        x = xs_ref[...]
        h = _rms_unit(x) * g_ref[...]
        ext3_ref[:, 1:16, :] = st_ref[...]
        ext3_ref[:, 16:16 + L, :] = h.reshape(B, L, D)
        t = lax.broadcasted_iota(jnp.int32, (1, L, 1), 1)
        parts = []
        for gi, w in enumerate(POOL_WINDOWS):
            c0 = gi * gw
            s = ext3_ref[:, 16:16 + L, c0:c0 + gw]
            for j in range(1, w):
                s = s + ext3_ref[:, 16 - j:16 - j + L, c0:c0 + gw]
            cnt = jnp.minimum(w, past_len + t + 1).astype(F32)
            pooled = (s / cnt).reshape(T, gw) - h[:, c0:c0 + gw]
            parts.append(_dot(pooled.astype(BF16), w_ref[gi]))
        mix = jnp.concatenate(parts, axis=-1) * sc_ref[...]
        x1_ref[...] = x + mix
        ps_ref[...] = ext3_ref[:, 16 + L - POOL_STATE:16 + L, :]


def _pool_layer(xp, xs, state, g, w, sc, past_len):
    Lp, D = xp.shape
    T = ROW_TILE
    nt = Lp // T
    B = state.shape[0]
    L = xs.shape[0] // B
    return pl.pallas_call(
        functools.partial(_pool_kernel, past_len=past_len),
        grid=(nt + 1,),
        in_specs=[
            pl.BlockSpec((T, D), lambda i: (jnp.minimum(i, nt - 1), 0)),
            pl.BlockSpec((T, D), lambda i: (0, 0)),
            pl.BlockSpec((B, POOL_STATE, D), lambda i: (0, 0, 0)),
            pl.BlockSpec((1, D), lambda i: (0, 0)),
            pl.BlockSpec(w.shape, lambda i: (0, 0, 0)),
            pl.BlockSpec((1, D), lambda i: (0, 0)),
        ],
        out_specs=[
            pl.BlockSpec((T, D), lambda i: (i, 0)),
            pl.BlockSpec((POOL_STATE, D), lambda i: (0, 0)),
            pl.BlockSpec((B, POOL_STATE, D), lambda i: (0, 0, 0)),
        ],
        out_shape=[
            jax.ShapeDtypeStruct((Lp + T, D), F32),
            jax.ShapeDtypeStruct((POOL_STATE, D), F32),
            jax.ShapeDtypeStruct((B, POOL_STATE, D), F32),
        ],
        scratch_shapes=[
            pltpu.VMEM((16 + T, D), F32),
            pltpu.VMEM((B, 16 + L, D), F32),
        ],
        compiler_params=_params("arbitrary"),
        name="pool_mixer",
    )(xp, xs, state, g, w, sc)


def _swiglu_tile(hb, wg, wu, wd):
    gt = _dot(hb, wg)
    ut = _dot(hb, wu)
    a = gt * jax.nn.sigmoid(gt) * ut
    return _dot(a.astype(BF16), wd)


def _ffn_kernel(x_ref, g_ref, wg_ref, wu_ref, wd_ref, out_ref, hb_ref, acc_ref):
    f = pl.program_id(1)

    @pl.when(f == 0)
    def _():
        x = x_ref[...]
        hb_ref[...] = (_rms_unit(x) * g_ref[...]).astype(BF16)
        acc_ref[...] = x

    acc_ref[...] += _swiglu_tile(hb_ref[...], wg_ref[...], wu_ref[...], wd_ref[...])

    @pl.when(f == pl.num_programs(1) - 1)
    def _():
        out_ref[...] = acc_ref[...]


def _ffn_splits(d_ff):
    nf = 2 if (d_ff // 2) % LANE == 0 else 1
    return nf, d_ff // nf


def _dense_ffn(x, g, wg, wu, wd):
    N, D = x.shape
    T = ROW_TILE
    nf, tf = _ffn_splits(wg.shape[1])
    return pl.pallas_call(
        _ffn_kernel,
        grid=(N // T, nf),
        in_specs=[
            pl.BlockSpec((T, D), lambda i, f: (i, 0)),
            pl.BlockSpec((1, D), lambda i, f: (0, 0)),
            pl.BlockSpec((D, tf), lambda i, f: (0, f)),
            pl.BlockSpec((D, tf), lambda i, f: (0, f)),
            pl.BlockSpec((tf, D), lambda i, f: (f, 0)),
        ],
        out_specs=pl.BlockSpec((T, D), lambda i, f: (i, 0)),
        out_shape=jax.ShapeDtypeStruct((N, D), F32),
        scratch_shapes=[pltpu.VMEM((T, D), BF16), pltpu.VMEM((T, D), F32)],
        compiler_params=_params("arbitrary", "arbitrary"),
        name="dense_swiglu",
    )(x, g, wg, wu, wd)


def _qkv_kernel(x_ref, gkv_ref, gq_ref, wk_ref, wv_ref, wq_ref, gkn_ref, gqn_ref,
                gsum_ref, gbc_ref,
                kp_ref, vp_ref, ks_ref, vs_ref, qh_ref, kh_ref, vh_ref, qs_ref):
    i = pl.program_id(0)
    nt = pl.num_programs(0) - 1
    n_heads = qh_ref.shape[0]

    xn = _rms_unit(x_ref[...])
    hkv = (xn * gkv_ref[...]).astype(BF16)
    hq = (xn * gq_ref[...]).astype(BF16)

    def head_norm(y, g):
        ssq = _dot((y * y).astype(BF16), gsum_ref[...])
        rs = lax.rsqrt(ssq * (1.0 / HEAD_DIM) + EPS)
        rs_hi = rs.astype(BF16)
        rs_lo = (rs - rs_hi.astype(F32)).astype(BF16)
        rsb = _dot(rs_hi, gbc_ref[...]) + _dot(rs_lo, gbc_ref[...])
        return y * rsb * g

    k = head_norm(_dot(hkv, wk_ref[...]), gkn_ref[...])
    v = _dot(hkv, wv_ref[...])
    q = head_norm(_dot(hq, wq_ref[...]), gqn_ref[...])

    @pl.when(i < nt)
    def _():
        kp_ref[...] = k
        vp_ref[...] = v
        for h in range(n_heads):
            sl = slice(h * V_DIM, (h + 1) * V_DIM)
            qh_ref[h] = q[:, sl].astype(BF16)
            kh_ref[h] = k[:, sl].T.astype(BF16)
            vh_ref[h] = v[:, sl].astype(BF16)

    @pl.when(i == nt)
    def _():
        ks_ref[...] = k
        vs_ref[...] = v
        qs_ref[...] = q.astype(BF16)


def _qkv_proj(x, gkv, gq, wk, wv, wq, gkn_t, gqn_t, gsum, gbc, n_sample):
    N, D = x.shape
    T = ROW_TILE
    nt = N // T - 1
    Lp = N - n_sample
    n_heads = D // V_DIM
    const2 = lambda i: (0, 0)
    prow = lambda i: (jnp.minimum(i, nt - 1), 0)
    phead = lambda i: (0, jnp.minimum(i, nt - 1), 0)
    return pl.pallas_call(
        _qkv_kernel,
        grid=(nt + 1,),
        in_specs=[
            pl.BlockSpec((T, D), lambda i: (i, 0)),
            pl.BlockSpec((1, D), const2),
            pl.BlockSpec((1, D), const2),
            pl.BlockSpec((D, D), const2),
            pl.BlockSpec((D, D), const2),
            pl.BlockSpec((D, D), const2),
            pl.BlockSpec((1, D), const2),
            pl.BlockSpec((1, D), const2),
            pl.BlockSpec(gsum.shape, const2),
            pl.BlockSpec(gbc.shape, const2),
        ],
        out_specs=[
            pl.BlockSpec((T, D), prow),
            pl.BlockSpec((T, D), prow),
            pl.BlockSpec((T, D), const2),
            pl.BlockSpec((T, D), const2),
            pl.BlockSpec((n_heads, T, V_DIM), phead),
            pl.BlockSpec((n_heads, V_DIM, T), lambda i: (0, 0, jnp.minimum(i, nt - 1))),
            pl.BlockSpec((n_heads, T, V_DIM), phead),
            pl.BlockSpec((T, D), const2),
        ],
        out_shape=[
            jax.ShapeDtypeStruct((Lp, D), F32),
            jax.ShapeDtypeStruct((Lp, D), F32),
            jax.ShapeDtypeStruct((n_sample, D), F32),
            jax.ShapeDtypeStruct((n_sample, D), F32),
            jax.ShapeDtypeStruct((n_heads, Lp, V_DIM), BF16),
            jax.ShapeDtypeStruct((n_heads, V_DIM, Lp), BF16),
            jax.ShapeDtypeStruct((n_heads, Lp, V_DIM), BF16),
            jax.ShapeDtypeStruct((n_sample, D), BF16),
        ],
        compiler_params=_params("arbitrary"),
        name="qkv_proj",
    )(x, gkv, gq, wk, wv, wq, gkn_t, gqn_t, gsum, gbc)


def _diff_out(o0, o1, lam, g, out_scale):
    o = o0 - lam * o1
    return _rms_unit(o) * g * out_scale


def _lane_partial_sum(p):
    out = p[:, 0:LANE]
    for k in range(1, p.shape[1] // LANE):
        out = out + p[:, k * LANE:(k + 1) * LANE]
    return out


def _attn_prompt_kernel(bias_ref, lam_ref, q_ref, kt_ref, v_ref, g_ref, o_ref,
                        bd_ref, bs_ref, l_ref, acc_ref, *m_scratch, out_scale):
    h = pl.program_id(0)
    i = pl.program_id(1)
    T = q_ref.shape[1]
    online = bool(m_scratch)

    @pl.when(i == 0)
    def _():
        row = lax.broadcasted_iota(jnp.int32, (T, T), 0)
        col = lax.broadcasted_iota(jnp.int32, (T, T), 1)
        tab = lambda b: bias_ref[b, h] * LOG2E
        visible = (col // CHUNK) <= (row // CHUNK)
        bd_ref[...] = jnp.where(visible, _rel_bias_tile(col - row, tab), NEG)
        bs_ref[...] = _rel_bias_tile(col - row - T, tab)

    if online:
        m_ref, = m_scratch
        m_ref[...] = jnp.full(m_ref.shape, NEG, F32)
    l_ref[...] = jnp.zeros(l_ref.shape, F32)
    acc_ref[...] = jnp.zeros(acc_ref.shape, F32)
    q = q_ref[0]
    qc = (q[:, :HEAD_DIM], q[:, HEAD_DIM:])

    def update(j, bias):
        start = pl.multiple_of(j * T, T)
        vt = v_ref[0, pl.ds(start, T), :]
        for c in range(2):
            kt = kt_ref[0, c * HEAD_DIM:(c + 1) * HEAD_DIM, pl.ds(start, T)]
            s = _dot(qc[c], kt) + bias
            if online:
                m_old = m_ref[c]
                m_new = jnp.maximum(m_old, jnp.max(s, axis=-1, keepdims=True))
                alpha = jnp.exp2(m_old - m_new)
                p = jnp.exp2(s - m_new)
                l_ref[c] = alpha * l_ref[c] + _lane_partial_sum(p)
                acc_ref[c] = alpha * acc_ref[c] + _dot(p.astype(BF16), vt)
                m_ref[c] = m_new
            else:
                p = jnp.exp2(s)
                l_ref[c] += _lane_partial_sum(p)
                acc_ref[c] += _dot(p.astype(BF16), vt)

    far_bias = bias_ref[15, h] * LOG2E

    def far_body(j, carry):
        update(j, far_bias)
        return carry

    lax.fori_loop(0, jnp.maximum(i - 1, 0), far_body, 0)

    @pl.when(i >= 1)
    def _():
        update(i - 1, bs_ref[...])

    update(i, bd_ref[...])

    o0 = acc_ref[0] / jnp.sum(l_ref[0], axis=-1, keepdims=True)
    o1 = acc_ref[1] / jnp.sum(l_ref[1], axis=-1, keepdims=True)
    o_ref[...] = _diff_out(o0, o1, lam_ref[0], g_ref[...], out_scale).astype(o_ref.dtype)


def _attn_prompt(rel_bias, lam, qh, kth, vh, g_subln, out_scale, online):
    n_heads, Lp, _ = qh.shape
    T = ROW_TILE
    assert T % CHUNK == 0 and T >= FAR_DISTANCE
    m_scratch = [pltpu.VMEM((2, T, 1), F32)] if online else []
    return pl.pallas_call(
        functools.partial(_attn_prompt_kernel, out_scale=out_scale),
        grid=(n_heads, Lp // T),
        in_specs=[
            pl.BlockSpec(memory_space=pltpu.SMEM),
            pl.BlockSpec(memory_space=pltpu.SMEM),
            pl.BlockSpec((1, T, V_DIM), lambda h, i: (h, i, 0)),
            pl.BlockSpec((1, V_DIM, Lp), lambda h, i: (h, 0, 0)),
            pl.BlockSpec((1, Lp, V_DIM), lambda h, i: (h, 0, 0)),
            pl.BlockSpec((1, V_DIM), lambda h, i: (0, 0)),
        ],
        out_specs=pl.BlockSpec((T, V_DIM), lambda h, i: (i, h)),
        out_shape=jax.ShapeDtypeStruct((Lp, n_heads * V_DIM), BF16),
        scratch_shapes=[
            pltpu.VMEM((T, T), F32),
            pltpu.VMEM((T, T), F32),
            pltpu.VMEM((2, T, LANE), F32),
            pltpu.VMEM((2, T, V_DIM), F32),
        ] + m_scratch,
        compiler_params=_params("arbitrary", "arbitrary"),
        name="attn_prompt_online" if online else "attn_prompt",
    )(rel_bias, lam, qh, kth, vh, g_subln)


def _attn_sample_kernel(lam_ref, qbd_ref, ck_ref, cv_ref, kn_ref, vn_ref, tab_ref, g_ref,
                        o_ref, nearb_ref, newb_ref, m_ref, l_ref, acc_ref, *, out_scale):
    b = pl.program_id(0)
    kb = pl.program_id(1)
    nkb = pl.num_programs(1)
    R, Tk = nearb_ref.shape
    L = kn_ref.shape[0]
    n_heads = o_ref.shape[1] // V_DIM

    @pl.when((b == 0) & (kb == 0))
    def _():
        tab = lambda bkt: tab_ref[:, bkt:bkt + 1]
        t_near = lax.broadcasted_iota(jnp.int32, (R, Tk), 0) % L
        col = lax.broadcasted_iota(jnp.int32, (R, Tk), 1)
        nearb_ref[...] = _rel_bias_tile(col - Tk - t_near, tab)
        t_new = lax.broadcasted_iota(jnp.int32, (R, L), 0) % L
        col_new = lax.broadcasted_iota(jnp.int32, (R, L), 1)
        newb_ref[...] = _rel_bias_tile(col_new - t_new, tab)

    @pl.when(kb == 0)
    def _():
        m_ref[...] = jnp.full(m_ref.shape, NEG, F32)
        l_ref[...] = jnp.zeros(l_ref.shape, F32)
        acc_ref[...] = jnp.zeros(acc_ref.shape, F32)

    def update(s, vb):
        m_old = m_ref[...]
        m_new = jnp.maximum(m_old, jnp.max(s, axis=-1, keepdims=True))
        alpha = jnp.exp2(m_old - m_new)
        p = jnp.exp2(s - m_new)
        l_ref[...] = alpha * l_ref[...] + jnp.sum(p, axis=-1, keepdims=True)
        acc_ref[...] = alpha * acc_ref[...] + _dot(p.astype(BF16), vb)
        m_ref[...] = m_new

    qbd = qbd_ref[0]
    s = _dot_nt(qbd, ck_ref[0].astype(BF16))
    vb = cv_ref[0].astype(BF16)

    @pl.when(kb < nkb - 1)
    def _():
        update(s + tab_ref[:, 15:16], vb)

    @pl.when(kb == nkb - 1)
    def _():
        update(s + nearb_ref[...], vb)
        s_new = _dot_nt(qbd, kn_ref[...].astype(BF16)) + newb_ref[...]
        update(s_new, vn_ref[...].astype(BF16))
        lam = lam_ref[0]
        for h in range(n_heads):
            r0 = h * 2 * L
            cs = slice(h * V_DIM, (h + 1) * V_DIM)
            o0 = acc_ref[r0:r0 + L, cs] / l_ref[r0:r0 + L, :]
            o1 = acc_ref[r0 + L:r0 + 2 * L, cs] / l_ref[r0 + L:r0 + 2 * L, :]
            o_ref[:, cs] = _diff_out(o0, o1, lam, g_ref[...], out_scale).astype(o_ref.dtype)


def _attn_sample(lam, qbd, cache_k, cache_v, k_new, v_new, tab, g_subln, out_scale):
    B, past, D = cache_k.shape
    R = qbd.shape[1]
    L = k_new.shape[0] // B
    Tk = min(1024, past)
    assert past % Tk == 0 and Tk >= FAR_DISTANCE + L
    return pl.pallas_call(
        functools.partial(_attn_sample_kernel, out_scale=out_scale),
        grid=(B, past // Tk),
        in_specs=[
            pl.BlockSpec(memory_space=pltpu.SMEM),
            pl.BlockSpec((1, R, D), lambda b, k: (b, 0, 0)),
            pl.BlockSpec((1, Tk, D), lambda b, k: (b, k, 0)),
            pl.BlockSpec((1, Tk, D), lambda b, k: (b, k, 0)),
            pl.BlockSpec((L, D), lambda b, k: (b, 0)),
            pl.BlockSpec((L, D), lambda b, k: (b, 0)),
            pl.BlockSpec(tab.shape, lambda b, k: (0, 0)),
            pl.BlockSpec((1, V_DIM), lambda b, k: (0, 0)),
        ],
        out_specs=pl.BlockSpec((L, D), lambda b, k: (b, 0)),
        out_shape=jax.ShapeDtypeStruct((B * L, D), BF16),
        scratch_shapes=[
            pltpu.VMEM((R, Tk), F32),
            pltpu.VMEM((R, L), F32),
            pltpu.VMEM((R, 1), F32),
            pltpu.VMEM((R, 1), F32),
            pltpu.VMEM((R, D), F32),
        ],
        compiler_params=_params("arbitrary", "arbitrary"),
        name="attn_sample",
    )(lam, qbd, cache_k, cache_v, k_new, v_new, tab, g_subln)


def _oproj_router_kernel(op_ref, os_ref, x_ref, wo_ref, g_ref, wrh_ref, wrl_ref,
                         x3_ref, h_ref, route_ref):
    i = pl.program_id(0)
    nt = pl.num_programs(0) - 1
    o = jnp.where(i == nt, os_ref[...], op_ref[...])
    x3 = x_ref[...] + _dot(o, wo_ref[...])
    x3_ref[...] = x3
    h = _rms_unit(x3) * g_ref[...]
    h_ref[...] = h
    h_hi = h.astype(BF16)
    h_lo = (h - h_hi.astype(F32)).astype(BF16)
    logits = _dot(h_hi, wrh_ref[...]) + (_dot(h_lo, wrh_ref[...]) + _dot(h_hi, wrl_ref[...]))
    lane = lax.broadcasted_iota(jnp.int32, logits.shape, 1)
    lg = jnp.where(lane < N_EXPERTS, logits, -jnp.inf)
    m1 = jnp.max(lg, axis=-1, keepdims=True)
    i1 = jnp.min(jnp.where(lg == m1, lane, LANE), axis=-1, keepdims=True)
    lg2 = jnp.where(lane == i1, -jnp.inf, lg)
    m2 = jnp.max(lg2, axis=-1, keepdims=True)
    i2 = jnp.min(jnp.where(lg2 == m2, lane, LANE), axis=-1, keepdims=True)
    e2 = jnp.exp(m2 - m1)
    den = 1.0 + e2
    g1 = 1.0 / den
    g2 = e2 / den
    route_ref[...] = jnp.where(lane == 0, i1.astype(F32),
                               jnp.where(lane == 1, i2.astype(F32),
                                         jnp.where(lane == 2, g1,
                                                   jnp.where(lane == 3, g2, 0.0))))


def _oproj_router(o_p, o_s, x, wo, g, wr_hi, wr_lo):
    N, D = x.shape
    T = ROW_TILE
    nt = N // T - 1
    const2 = lambda i: (0, 0)
    row = lambda i: (i, 0)
    return pl.pallas_call(
        _oproj_router_kernel,
        grid=(nt + 1,),
        in_specs=[
            pl.BlockSpec((T, D), lambda i: (jnp.minimum(i, nt - 1), 0)),
            pl.BlockSpec((T, D), const2),
            pl.BlockSpec((T, D), row),
            pl.BlockSpec((D, D), const2),
            pl.BlockSpec((1, D), const2),
            pl.BlockSpec((D, LANE), const2),
            pl.BlockSpec((D, LANE), const2),
        ],
        out_specs=[
            pl.BlockSpec((T, D), row),
            pl.BlockSpec((T, D), row),
            pl.BlockSpec((T, LANE), row),
        ],
        out_shape=[
            jax.ShapeDtypeStruct((N, D), F32),
            jax.ShapeDtypeStruct((N, D), F32),
            jax.ShapeDtypeStruct((N, LANE), F32),
        ],
        compiler_params=_params("arbitrary"),
        name="oproj_router",
    )(o_p, o_s, x, wo, g, wr_hi, wr_lo)


def _gather_rows(idx_ref, n, src_hbm, dst_ref, sem, stride=1, offset=0):
    def row_copy(r, src_row):
        return pltpu.make_async_copy(src_hbm.at[pl.ds(src_row, 1), :],
                                     dst_ref.at[pl.ds(r, 1), :], sem)

    def issue(r, carry):
        row_copy(r, idx_ref[0, 0, stride * r + offset]).start()
        return carry

    def drain(r, carry):
        row_copy(r, 0).wait()
        return carry

    lax.fori_loop(0, n, issue, 0)
    lax.fori_loop(0, n, drain, 0)


def _dispatch_kernel(idx_ref, src_hbm, out_ref, sem):
    _gather_rows(idx_ref, out_ref.shape[0], src_hbm, out_ref, sem)


def _dispatch(src_idx, h):
    P = src_idx.shape[0]
    D = h.shape[1]
    T = ROW_TILE
    return pl.pallas_call(
        _dispatch_kernel,
        grid=(P // T,),
        in_specs=[
            pl.BlockSpec((1, 1, T), lambda i: (i, 0, 0), memory_space=pltpu.SMEM),
            pl.BlockSpec(memory_space=pl.ANY),
        ],
        out_specs=pl.BlockSpec((T, D), lambda i: (i, 0)),
        out_shape=jax.ShapeDtypeStruct((P, D), F32),
        scratch_shapes=[pltpu.SemaphoreType.DMA(())],
        compiler_params=_params("arbitrary"),
        name="moe_dispatch",
    )(src_idx.reshape(P // T, 1, T), h)


def _experts_kernel(te_ref, na_ref, xs_ref, gate_ref, wg_ref, wu_ref, wd_ref, out_ref, acc_ref):
    i = pl.program_id(0)
    f = pl.program_id(1)
    last = pl.num_programs(1) - 1
    active = i < na_ref[0]

    @pl.when(active)
    def _():
        @pl.when(f == 0)
        def _():
            acc_ref[...] = jnp.zeros(acc_ref.shape, F32)

        acc_ref[...] += _swiglu_tile(xs_ref[...].astype(BF16), wg_ref[0], wu_ref[0], wd_ref[0])

        @pl.when(f == last)
        def _():
            out_ref[...] = acc_ref[...] * gate_ref[...]

    @pl.when(jnp.logical_not(active) & (f == last))
    def _():
        out_ref[...] = jnp.zeros(out_ref.shape, F32)


def _experts(tile_expert, n_active, xs, gate_sorted, wg, wu, wd):
    P, D = xs.shape
    T = ROW_TILE
    nf, tf = _ffn_splits(wg.shape[2])
    fidx = lambda i, f, na: jnp.where(i < na[0], f, nf - 1)
    return pl.pallas_call(
        _experts_kernel,
        grid_spec=pltpu.PrefetchScalarGridSpec(
            num_scalar_prefetch=2,
            grid=(P // T, nf),
            in_specs=[
                pl.BlockSpec((T, D), lambda i, f, te, na: (i, 0)),
                pl.BlockSpec((T, 1), lambda i, f, te, na: (i, 0)),
                pl.BlockSpec((1, D, tf), lambda i, f, te, na: (te[i], 0, fidx(i, f, na))),
                pl.BlockSpec((1, D, tf), lambda i, f, te, na: (te[i], 0, fidx(i, f, na))),
                pl.BlockSpec((1, tf, D), lambda i, f, te, na: (te[i], fidx(i, f, na), 0)),
            ],
            out_specs=pl.BlockSpec((T, D), lambda i, f, te, na: (i, 0)),
            scratch_shapes=[pltpu.VMEM((T, D), F32)],
        ),
        out_shape=jax.ShapeDtypeStruct((P, D), F32),
        compiler_params=_params("arbitrary", "arbitrary"),
        name="moe_experts",
    )(tile_expert, n_active, xs, gate_sorted, wg, wu, wd)


def _combine_kernel(pos_ref, x_ref, ys_hbm, yp_ref, ysm_ref, a_ref, b_ref, sem_a, sem_b):
    i = pl.program_id(0)
    nt = pl.num_programs(0) - 1
    T = x_ref.shape[0]
    _gather_rows(pos_ref, T, ys_hbm, a_ref, sem_a, stride=TOP_K, offset=0)
    _gather_rows(pos_ref, T, ys_hbm, b_ref, sem_b, stride=TOP_K, offset=1)
    y = x_ref[...] + (a_ref[...] + b_ref[...])

    @pl.when(i < nt)
    def _():
        yp_ref[...] = y

    @pl.when(i == nt)
    def _():
        ysm_ref[...] = y


def _combine(dest, x3, ys, n_sample):
    N, D = x3.shape
    T = ROW_TILE
    nt = N // T - 1
    return pl.pallas_call(
        _combine_kernel,
        grid=(nt + 1,),
        in_specs=[
            pl.BlockSpec((1, 1, TOP_K * T), lambda i: (i, 0, 0), memory_space=pltpu.SMEM),
            pl.BlockSpec((T, D), lambda i: (i, 0)),
            pl.BlockSpec(memory_space=pl.ANY),
        ],
        out_specs=[
            pl.BlockSpec((T, D), lambda i: (jnp.minimum(i, nt - 1), 0)),
            pl.BlockSpec((T, D), lambda i: (0, 0)),
        ],
        out_shape=[
            jax.ShapeDtypeStruct((N - n_sample, D), F32),
            jax.ShapeDtypeStruct((n_sample, D), F32),
        ],
        scratch_shapes=[
            pltpu.VMEM((T, D), F32),
            pltpu.VMEM((T, D), F32),
            pltpu.SemaphoreType.DMA(()),
            pltpu.SemaphoreType.DMA(()),
        ],
        compiler_params=_params("arbitrary"),
        name="moe_combine",
    )(dest.reshape(nt + 1, 1, TOP_K * T), x3, ys)


def _routing_tables(route, tile):
    n = route.shape[0]
    experts = route[:, :TOP_K].astype(jnp.int32).reshape(-1)
    gates = route[:, TOP_K:2 * TOP_K].reshape(-1)
    n_assign = n * TOP_K
    n_tiles = n_assign // tile + N_EXPERTS
    onehot = (experts[:, None] == jnp.arange(N_EXPERTS)[None, :]).astype(jnp.int32)
    csum = jnp.cumsum(onehot, axis=0)
    rank = jnp.sum(onehot * csum, axis=1) - 1
    counts = csum[-1]
    tiles_per = (counts + tile - 1) // tile
    tile_end = jnp.cumsum(tiles_per)
    group_off = (tile_end - tiles_per) * tile
    dest = group_off[experts] + rank
    src_idx = jnp.zeros((n_tiles * tile,), jnp.int32).at[dest].set(jnp.arange(n_assign, dtype=jnp.int32) // TOP_K)
    gate_sorted = jnp.zeros((n_tiles * tile,), F32).at[dest].set(gates)
    n_active = tile_end[-1:].astype(jnp.int32)
    tile_ids = jnp.minimum(jnp.arange(n_tiles, dtype=jnp.int32), n_active[0] - 1)
    tile_expert = jnp.sum((tile_ids[:, None] >= tile_end[None, :]).astype(jnp.int32), axis=1)
    return dest.astype(jnp.int32), src_idx, gate_sorted[:, None], tile_expert, n_active


def _group_matrices(d_model):
    n_groups = d_model // HEAD_DIM
    gsum = np.zeros((d_model, LANE), np.float32)
    gsum[np.arange(d_model), np.arange(d_model) // HEAD_DIM] = 1.0
    assert n_groups <= LANE
    return jnp.asarray(gsum, BF16), jnp.asarray(gsum.T.copy(), BF16)


def kernel(x_prompt, x_sample, cache_k, cache_v, state_pool, g_pool_norm, w_pool, pool_scale, g_attn, w_q, g_qn, lambda_q1, lambda_k1, lambda_q2, lambda_k2, g_subln, w_o, g_kv, w_k, w_v, g_kn, rel_bias, g_ffn, w_gate_dense, w_up_dense, w_down_dense, w_router, w_gate_moe, w_up_moe, w_down_moe):
    Bp, Lp, D = x_prompt.shape
    Bs, Ls, _ = x_sample.shape
    past = cache_k.shape[1]
    n_heads = D // V_DIM
    n_sample = Bs * Ls
    assert Bp == 1 and n_sample == ROW_TILE and Lp % ROW_TILE == 0
    assert g_pool_norm.shape[0] == 1 and g_attn.shape[0] == 1
    bf = lambda a: a.astype(BF16)
    row = lambda a: a.reshape(1, -1)

    x1, pool_p, pool_s = _pool_layer(
        x_prompt.reshape(Lp, D), x_sample.reshape(n_sample, D), state_pool[0],
        row(g_pool_norm[0]), bf(w_pool[0]), row(pool_scale[0]), past)
    x2 = _dense_ffn(x1, row(g_ffn[0]), bf(w_gate_dense[0]), bf(w_up_dense[0]), bf(w_down_dense[0]))

    layer = 1
    lambda_init = 0.8 - 0.6 * math.exp(-0.3 * layer)
    lam = (jnp.exp(jnp.sum(lambda_q1[0] * lambda_k1[0])) - jnp.exp(jnp.sum(lambda_q2[0] * lambda_k2[0]))
           + lambda_init).reshape(1)
    gsum, gbc = _group_matrices(D)
    n_groups = D // HEAD_DIM
    k_p, v_p, k_s, v_s, qh, kh, vh, q_s = _qkv_proj(
        x2, row(g_kv), row(g_attn[0]), bf(w_k), bf(w_v), bf(w_q[0]),
        row(jnp.tile(g_kn, n_groups)), row(jnp.tile(g_qn[0], n_groups) * (HEAD_DIM ** -0.5 * LOG2E)),
        gsum, gbc, n_sample)
    out_scale = 1.0 - lambda_init
    g_sub = row(g_subln[0])
    score_bound = LOG2E * (math.sqrt(HEAD_DIM) * jnp.max(jnp.abs(g_qn[0])) * jnp.max(jnp.abs(g_kn))
                           + jnp.max(jnp.abs(rel_bias)))
    o_p = lax.cond(
        score_bound <= UNSHIFTED_SCORE_LIMIT,
        functools.partial(_attn_prompt, out_scale=out_scale, online=False),
        functools.partial(_attn_prompt, out_scale=out_scale, online=True),
        rel_bias, lam, qh, kh, vh, g_sub)

    q4 = q_s.reshape(Bs, Ls, n_groups, HEAD_DIM).transpose(0, 2, 1, 3)
    eye = jnp.eye(n_groups, dtype=BF16)
    qbd = (q4[:, :, :, None, :] * eye[None, :, None, :, None]).reshape(Bs, n_groups * Ls, D)
    tab = jnp.repeat(rel_bias.T, 2 * Ls, axis=0) * LOG2E
    o_s = _attn_sample(lam, qbd, cache_k.reshape(Bs, past, D), cache_v.reshape(Bs, past, D),
                       k_s, v_s, tab, g_sub, out_scale)

    wr = jnp.pad(w_router[0], ((0, 0), (0, LANE - N_EXPERTS)))
    wr_hi = bf(wr)
    wr_lo = bf(wr - wr_hi.astype(F32))
    x3, h_moe, route = _oproj_router(o_p, o_s, x2, bf(w_o[0]), row(g_ffn[1]), wr_hi, wr_lo)
    dest, src_idx, gate_sorted, tile_expert, n_active = _routing_tables(route, ROW_TILE)
    xs = _dispatch(src_idx, h_moe)
    ys = _experts(tile_expert, n_active, xs, gate_sorted,
                  bf(w_gate_moe[0]), bf(w_up_moe[0]), bf(w_down_moe[0]))
    y_p, y_s = _combine(dest, x3, ys, n_sample)

    return (y_p.reshape(Bp, Lp, D), y_s.reshape(Bs, Ls, D),
            k_p.reshape(Bp, Lp, n_heads, 2, HEAD_DIM), v_p.reshape(Bp, Lp, n_heads, V_DIM),
            pool_p.reshape(1, Bp, POOL_STATE, D),
            k_s.reshape(Bs, Ls, n_heads, 2, HEAD_DIM), v_s.reshape(Bs, Ls, n_heads, V_DIM),
            pool_s.reshape(1, Bs, POOL_STATE, D))
```

```python
import functools
import math

import numpy as np
import jax
import jax.numpy as jnp
from jax import lax
from jax.experimental import pallas as pl
from jax.experimental.pallas import tpu as pltpu

EPS = 1e-6
CHUNK = 64
POOL_WINDOWS = (2, 4, 8, 16)
POOL_STATE = max(POOL_WINDOWS) - 1
HEAD_DIM = 64
V_DIM = 2 * HEAD_DIM
N_EXPERTS = 8
TOP_K = 2
MAX_EXACT = 8
BUCKET_UPPER = (1, 2, 3, 4, 5, 6, 7, 8, 12, 16, 23, 32, 46, 64, 91)
FAR_DISTANCE = 128
NEG = -1e30
LOG2E = math.log2(math.e)
UNSHIFTED_SCORE_LIMIT = 80.0

ROW_TILE = 512
FAR_TILES_PER_TRIP = 4
LANE = 128
VMEM_LIMIT = 56 * 1024 * 1024

F32 = jnp.float32
BF16 = jnp.bfloat16


def _dot(a, b):
    return jnp.dot(a, b, preferred_element_type=F32)


def _dot_nt(a, b):
    return lax.dot_general(a, b, (((1,), (1,)), ((), ())), preferred_element_type=F32)


def _rms_unit(x):
    return x * lax.rsqrt(jnp.mean(x * x, axis=-1, keepdims=True) + EPS)


def _rel_bias_tile(rel, tab):
    n = jnp.abs(rel)
    neg = tab(15)
    pos = tab(31)
    for b in range(14, -1, -1):
        lt = n < BUCKET_UPPER[b]
        neg = jnp.where(lt, tab(b), neg)
        pos = jnp.where(lt, tab(16 + b), pos)
    return jnp.where(rel > 0, pos, neg)


def _params(*sem):
    return pltpu.CompilerParams(dimension_semantics=sem, vmem_limit_bytes=VMEM_LIMIT)


def _pool_kernel(xp_ref, xs_ref, st_ref, g_ref, w_ref, sc_ref,
                 x1_ref, pp_ref, ps_ref, ext_ref, ext3_ref, *, past_len):
    i = pl.program_id(0)
    nt = pl.num_programs(0) - 1
    T, D = xp_ref.shape
    gw = D // len(POOL_WINDOWS)

    @pl.when(i < nt)
    def _prompt():
        x = xp_ref[...]
        h = _rms_unit(x) * g_ref[...]

        @pl.when(i == 0)
        def _():
            ext_ref[0:16, :] = jnp.zeros((16, D), F32)

        ext_ref[16:16 + T, :] = h
        row = i * T + lax.broadcasted_iota(jnp.int32, (T, 1), 0)
        parts = []
        for gi, w in enumerate(POOL_WINDOWS):
            c0 = gi * gw
            s = ext_ref[16:16 + T, c0:c0 + gw]
            for j in range(1, w):
                s = s + ext_ref[16 - j:16 - j + T, c0:c0 + gw]
            cnt = jnp.minimum(w, row + 1).astype(F32)
            pooled = s / cnt - h[:, c0:c0 + gw]
            parts.append(_dot(pooled.astype(BF16), w_ref[gi]))
        mix = jnp.concatenate(parts, axis=-1) * sc_ref[...]
        x1_ref[...] = x + mix
        tail = ext_ref[T:T + 16, :]
        ext_ref[0:16, :] = tail

        @pl.when(i == nt - 1)
        def _():
            pp_ref[...] = tail[1:16, :]

    @pl.when(i == nt)
    def _sample():
        B = st_ref.shape[0]
        L = T // B
        x = xs_ref[...]
        h = _rms_unit(x) * g_ref[...]
        ext3_ref[:, 1:16, :] = st_ref[...]
        ext3_ref[:, 16:16 + L, :] = h.reshape(B, L, D)
        t = lax.broadcasted_iota(jnp.int32, (1, L, 1), 1)
        parts = []
        for gi, w in enumerate(POOL_WINDOWS):
            c0 = gi * gw
            s = ext3_ref[:, 16:16 + L, c0:c0 + gw]
            for j in range(1, w):
                s = s + ext3_ref[:, 16 - j:16 - j + L, c0:c0 + gw]
            cnt = jnp.minimum(w, past_len + t + 1).astype(F32)
            pooled = (s / cnt).reshape(T, gw) - h[:, c0:c0 + gw]
            parts.append(_dot(pooled.astype(BF16), w_ref[gi]))
        mix = jnp.concatenate(parts, axis=-1) * sc_ref[...]
        x1_ref[...] = x + mix
        ps_ref[...] = ext3_ref[:, 16 + L - POOL_STATE:16 + L, :]


def _pool_layer(xp, xs, state, g, w, sc, past_len):
    Lp, D = xp.shape
    T = ROW_TILE
    nt = Lp // T
    B = state.shape[0]
    L = xs.shape[0] // B
    return pl.pallas_call(
        functools.partial(_pool_kernel, past_len=past_len),
        grid=(nt + 1,),
        in_specs=[
            pl.BlockSpec((T, D), lambda i: (jnp.minimum(i, nt - 1), 0)),
            pl.BlockSpec((T, D), lambda i: (0, 0)),
            pl.BlockSpec((B, POOL_STATE, D), lambda i: (0, 0, 0)),
            pl.BlockSpec((1, D), lambda i: (0, 0)),
            pl.BlockSpec(w.shape, lambda i: (0, 0, 0)),
            pl.BlockSpec((1, D), lambda i: (0, 0)),
        ],
        out_specs=[
            pl.BlockSpec((T, D), lambda i: (i, 0)),
            pl.BlockSpec((POOL_STATE, D), lambda i: (0, 0)),
            pl.BlockSpec((B, POOL_STATE, D), lambda i: (0, 0, 0)),
        ],
        out_shape=[
            jax.ShapeDtypeStruct((Lp + T, D), F32),
            jax.ShapeDtypeStruct((POOL_STATE, D), F32),
            jax.ShapeDtypeStruct((B, POOL_STATE, D), F32),
        ],
        scratch_shapes=[
            pltpu.VMEM((16 + T, D), F32),
            pltpu.VMEM((B, 16 + L, D), F32),
        ],
        compiler_params=_params("arbitrary"),
        name="pool_mixer",
    )(xp, xs, state, g, w, sc)


def _swiglu_tile(hb, wg, wu, wd):
    gt = _dot(hb, wg)
    ut = _dot(hb, wu)
    a = gt * jax.nn.sigmoid(gt) * ut
    return _dot(a.astype(BF16), wd)


def _ffn_kernel(x_ref, g_ref, wg_ref, wu_ref, wd_ref, out_ref, hb_ref, acc_ref):
    f = pl.program_id(1)

    @pl.when(f == 0)
    def _():
        x = x_ref[...]
        hb_ref[...] = (_rms_unit(x) * g_ref[...]).astype(BF16)
        acc_ref[...] = x

    acc_ref[...] += _swiglu_tile(hb_ref[...], wg_ref[...], wu_ref[...], wd_ref[...])

    @pl.when(f == pl.num_programs(1) - 1)
    def _():
        out_ref[...] = acc_ref[...]


def _ffn_splits(d_ff):
    nf = 2 if (d_ff // 2) % LANE == 0 else 1
    return nf, d_ff // nf


def _dense_ffn(x, g, wg, wu, wd):
    N, D = x.shape
    T = ROW_TILE
    nf, tf = _ffn_splits(wg.shape[1])
    return pl.pallas_call(
        _ffn_kernel,
        grid=(N // T, nf),
        in_specs=[
            pl.BlockSpec((T, D), lambda i, f: (i, 0)),
            pl.BlockSpec((1, D), lambda i, f: (0, 0)),
            pl.BlockSpec((D, tf), lambda i, f: (0, f)),
            pl.BlockSpec((D, tf), lambda i, f: (0, f)),
            pl.BlockSpec((tf, D), lambda i, f: (f, 0)),
        ],
        out_specs=pl.BlockSpec((T, D), lambda i, f: (i, 0)),
        out_shape=jax.ShapeDtypeStruct((N, D), F32),
        scratch_shapes=[pltpu.VMEM((T, D), BF16), pltpu.VMEM((T, D), F32)],
        compiler_params=_params("arbitrary", "arbitrary"),
        name="dense_swiglu",
    )(x, g, wg, wu, wd)


def _qkv_kernel(x_ref, gkv_ref, gq_ref, wk_ref, wv_ref, wq_ref, gkn_ref, gqn_ref,
                gsum_ref, gbc_ref,
                kp_ref, vp_ref, ks_ref, vs_ref, qh_ref, kh_ref, vh_ref, qs_ref):
    i = pl.program_id(0)
    nt = pl.num_programs(0) - 1
    n_heads = qh_ref.shape[0]

    xn = _rms_unit(x_ref[...])
    hkv = (xn * gkv_ref[...]).astype(BF16)
    hq = (xn * gq_ref[...]).astype(BF16)

    def head_norm(y, g):
        ssq = _dot((y * y).astype(BF16), gsum_ref[...])
        rs = lax.rsqrt(ssq * (1.0 / HEAD_DIM) + EPS)
        rs_hi = rs.astype(BF16)
        rs_lo = (rs - rs_hi.astype(F32)).astype(BF16)
        rsb = _dot(rs_hi, gbc_ref[...]) + _dot(rs_lo, gbc_ref[...])
        return y * rsb * g

    k = head_norm(_dot(hkv, wk_ref[...]), gkn_ref[...])
    v = _dot(hkv, wv_ref[...])
    q = head_norm(_dot(hq, wq_ref[...]), gqn_ref[...])

    @pl.when(i < nt)
    def _():
        kp_ref[...] = k
        vp_ref[...] = v
        for h in range(n_heads):
            sl = slice(h * V_DIM, (h + 1) * V_DIM)
            qh_ref[h] = q[:, sl].astype(BF16)
            kh_ref[h] = k[:, sl].T.astype(BF16)
            vh_ref[h] = v[:, sl].astype(BF16)

    @pl.when(i == nt)
    def _():
        ks_ref[...] = k
        vs_ref[...] = v
        qs_ref[...] = q.astype(BF16)


def _qkv_proj(x, gkv, gq, wk, wv, wq, gkn_t, gqn_t, gsum, gbc, n_sample):
    N, D = x.shape
    T = ROW_TILE
    nt = N // T - 1
    Lp = N - n_sample
    n_heads = D // V_DIM
    const2 = lambda i: (0, 0)
    prow = lambda i: (jnp.minimum(i, nt - 1), 0)
    phead = lambda i: (0, jnp.minimum(i, nt - 1), 0)
    return pl.pallas_call(
        _qkv_kernel,
        grid=(nt + 1,),
        in_specs=[
            pl.BlockSpec((T, D), lambda i: (i, 0)),
            pl.BlockSpec((1, D), const2),
            pl.BlockSpec((1, D), const2),
            pl.BlockSpec((D, D), const2),
            pl.BlockSpec((D, D), const2),
            pl.BlockSpec((D, D), const2),
            pl.BlockSpec((1, D), const2),
            pl.BlockSpec((1, D), const2),
            pl.BlockSpec(gsum.shape, const2),
            pl.BlockSpec(gbc.shape, const2),
        ],
        out_specs=[
            pl.BlockSpec((T, D), prow),
            pl.BlockSpec((T, D), prow),
            pl.BlockSpec((T, D), const2),
            pl.BlockSpec((T, D), const2),
            pl.BlockSpec((n_heads, T, V_DIM), phead),
            pl.BlockSpec((n_heads, V_DIM, T), lambda i: (0, 0, jnp.minimum(i, nt - 1))),
            pl.BlockSpec((n_heads, T, V_DIM), phead),
            pl.BlockSpec((T, D), const2),
        ],
        out_shape=[
            jax.ShapeDtypeStruct((Lp, D), F32),
            jax.ShapeDtypeStruct((Lp, D), F32),
            jax.ShapeDtypeStruct((n_sample, D), F32),
            jax.ShapeDtypeStruct((n_sample, D), F32),
            jax.ShapeDtypeStruct((n_heads, Lp, V_DIM), BF16),
            jax.ShapeDtypeStruct((n_heads, V_DIM, Lp), BF16),
            jax.ShapeDtypeStruct((n_heads, Lp, V_DIM), BF16),
            jax.ShapeDtypeStruct((n_sample, D), BF16),
        ],
        compiler_params=_params("arbitrary"),
        name="qkv_proj",
    )(x, gkv, gq, wk, wv, wq, gkn_t, gqn_t, gsum, gbc)


def _diff_out(o0, o1, lam, g, out_scale):
    o = o0 - lam * o1
    return _rms_unit(o) * g * out_scale


def _lane_partial_sum(p):
    out = p[:, 0:LANE]
    for k in range(1, p.shape[1] // LANE):
        out = out + p[:, k * LANE:(k + 1) * LANE]
    return out


def _attn_prompt_kernel(bias_ref, lam_ref, q_ref, kt_ref, v_ref, g_ref, o_ref,
                        bd_ref, bs_ref, l_ref, acc_ref, *m_scratch, out_scale):
    h = pl.program_id(0)
    i = pl.program_id(1)
    T = q_ref.shape[1]
    online = bool(m_scratch)

    @pl.when(i == 0)
    def _():
        row = lax.broadcasted_iota(jnp.int32, (T, T), 0)
        col = lax.broadcasted_iota(jnp.int32, (T, T), 1)
        tab = lambda b: bias_ref[b, h] * LOG2E
        visible = (col // CHUNK) <= (row // CHUNK)
        bd_ref[...] = jnp.where(visible, _rel_bias_tile(col - row, tab), NEG)
        bs_ref[...] = _rel_bias_tile(col - row - T, tab)

    if online:
        m_ref, = m_scratch
        m_ref[...] = jnp.full(m_ref.shape, NEG, F32)
    l_ref[...] = jnp.zeros(l_ref.shape, F32)
    acc_ref[...] = jnp.zeros(acc_ref.shape, F32)
    q = q_ref[0]
    qc = (q[:, :HEAD_DIM], q[:, HEAD_DIM:])

    def update(j, bias, width=T):
        start = pl.multiple_of(j * T, T)
        vt = v_ref[0, pl.ds(start, width), :]
        for c in range(2):
            kt = kt_ref[0, c * HEAD_DIM:(c + 1) * HEAD_DIM, pl.ds(start, width)]
            s = _dot(qc[c], kt) + bias
            if online:
                m_old = m_ref[c]
                m_new = jnp.maximum(m_old, jnp.max(s, axis=-1, keepdims=True))
                alpha = jnp.exp2(m_old - m_new)
                p = jnp.exp2(s - m_new)
                l_ref[c] = alpha * l_ref[c] + _lane_partial_sum(p)
                acc_ref[c] = alpha * acc_ref[c] + _dot(p.astype(BF16), vt)
                m_ref[c] = m_new
            else:
                p = jnp.exp2(s)
                l_ref[c] += _lane_partial_sum(p)
                acc_ref[c] += _dot(p.astype(BF16), vt)

    far_bias = bias_ref[15, h] * LOG2E

    n_far = jnp.maximum(i - 1, 0)

    def far_body(jj, carry):
        update(FAR_TILES_PER_TRIP * jj, far_bias, width=FAR_TILES_PER_TRIP * T)
        return carry

    def far_rest(j, carry):
        update(j, far_bias)
        return carry

    n_trips = n_far // FAR_TILES_PER_TRIP
    lax.fori_loop(0, n_trips, far_body, 0)
    lax.fori_loop(n_trips * FAR_TILES_PER_TRIP, n_far, far_rest, 0)

    @pl.when(i >= 1)
    def _():
        update(i - 1, bs_ref[...])

    update(i, bd_ref[...])

    o0 = acc_ref[0] / jnp.sum(l_ref[0], axis=-1, keepdims=True)
    o1 = acc_ref[1] / jnp.sum(l_ref[1], axis=-1, keepdims=True)
    o_ref[...] = _diff_out(o0, o1, lam_ref[0], g_ref[...], out_scale).astype(o_ref.dtype)


def _attn_prompt(rel_bias, lam, qh, kth, vh, g_subln, out_scale, online):
    n_heads, Lp, _ = qh.shape
    T = ROW_TILE
    assert T % CHUNK == 0 and T >= FAR_DISTANCE
    m_scratch = [pltpu.VMEM((2, T, 1), F32)] if online else []
    return pl.pallas_call(
        functools.partial(_attn_prompt_kernel, out_scale=out_scale),
        grid=(n_heads, Lp // T),
        in_specs=[
            pl.BlockSpec(memory_space=pltpu.SMEM),
            pl.BlockSpec(memory_space=pltpu.SMEM),
            pl.BlockSpec((1, T, V_DIM), lambda h, i: (h, i, 0)),
            pl.BlockSpec((1, V_DIM, Lp), lambda h, i: (h, 0, 0)),
            pl.BlockSpec((1, Lp, V_DIM), lambda h, i: (h, 0, 0)),
            pl.BlockSpec((1, V_DIM), lambda h, i: (0, 0)),
        ],
        out_specs=pl.BlockSpec((T, V_DIM), lambda h, i: (i, h)),
        out_shape=jax.ShapeDtypeStruct((Lp, n_heads * V_DIM), BF16),
        scratch_shapes=[
            pltpu.VMEM((T, T), F32),
            pltpu.VMEM((T, T), F32),
            pltpu.VMEM((2, T, LANE), F32),
            pltpu.VMEM((2, T, V_DIM), F32),
        ] + m_scratch,
        compiler_params=_params("arbitrary", "arbitrary"),
        name="attn_prompt_online" if online else "attn_prompt",
    )(rel_bias, lam, qh, kth, vh, g_subln)


def _attn_sample_kernel(lam_ref, qbd_ref, ck_ref, cv_ref, kn_ref, vn_ref, tab_ref, g_ref,
                        o_ref, nearb_ref, newb_ref, m_ref, l_ref, acc_ref, *, out_scale):
    b = pl.program_id(0)
    kb = pl.program_id(1)
    nkb = pl.num_programs(1)
    R, Tk = nearb_ref.shape
    L = kn_ref.shape[0]
    n_heads = o_ref.shape[1] // V_DIM

    @pl.when((b == 0) & (kb == 0))
    def _():
        tab = lambda bkt: tab_ref[:, bkt:bkt + 1]
        t_near = lax.broadcasted_iota(jnp.int32, (R, Tk), 0) % L
        col = lax.broadcasted_iota(jnp.int32, (R, Tk), 1)
        nearb_ref[...] = _rel_bias_tile(col - Tk - t_near, tab)
        t_new = lax.broadcasted_iota(jnp.int32, (R, L), 0) % L
        col_new = lax.broadcasted_iota(jnp.int32, (R, L), 1)
        newb_ref[...] = _rel_bias_tile(col_new - t_new, tab)

    @pl.when(kb == 0)
    def _():
        m_ref[...] = jnp.full(m_ref.shape, NEG, F32)
        l_ref[...] = jnp.zeros(l_ref.shape, F32)
        acc_ref[...] = jnp.zeros(acc_ref.shape, F32)

    rows_per_head = 2 * L

    def update(s, value_of_head):
        m_old = m_ref[...]
        m_new = jnp.maximum(m_old, jnp.max(s, axis=-1, keepdims=True))
        alpha = jnp.exp2(m_old - m_new)
        p = jnp.exp2(s - m_new)
        l_ref[...] = alpha * l_ref[...] + jnp.sum(p, axis=-1, keepdims=True)
        pb = p.astype(BF16)
        for h in range(n_heads):
            rs = slice(h * rows_per_head, (h + 1) * rows_per_head)
            acc_ref[rs, :] = alpha[rs, :] * acc_ref[rs, :] + _dot(pb[rs, :], value_of_head(h))
        m_ref[...] = m_new

    qbd = qbd_ref[0]
    s = _dot_nt(qbd, ck_ref[0])
    cache_value = lambda h: cv_ref[0, pl.ds(h, Tk, stride=n_heads), :].astype(BF16)

    @pl.when(kb < nkb - 1)
    def _():
        update(s + tab_ref[:, 15:16], cache_value)

    @pl.when(kb == nkb - 1)
    def _():
        update(s + nearb_ref[...], cache_value)
        s_new = _dot_nt(qbd, kn_ref[...].astype(BF16)) + newb_ref[...]
        update(s_new, lambda h: vn_ref[:, h * V_DIM:(h + 1) * V_DIM].astype(BF16))
        lam = lam_ref[0]
        o = acc_ref[...] / l_ref[...]
        for h in range(n_heads):
            r0 = h * rows_per_head
            o_ref[:, h * V_DIM:(h + 1) * V_DIM] = _diff_out(
                o[r0:r0 + L, :], o[r0 + L:r0 + 2 * L, :], lam, g_ref[...], out_scale).astype(o_ref.dtype)


def _attn_sample(lam, qbd, cache_k, cache_v, k_new, v_new, tab, g_subln, out_scale):
    B, past, D = cache_k.shape
    n_heads = cache_v.shape[2]
    R = qbd.shape[1]
    L = k_new.shape[0] // B
    Tk = min(1024, past)
    assert past % Tk == 0 and Tk >= FAR_DISTANCE + L
    return pl.pallas_call(
        functools.partial(_attn_sample_kernel, out_scale=out_scale),
        grid=(B, past // Tk),
        in_specs=[
            pl.BlockSpec(memory_space=pltpu.SMEM),
            pl.BlockSpec((1, R, D), lambda b, k: (b, 0, 0)),
            pl.BlockSpec((1, Tk, D), lambda b, k: (b, k, 0)),
            pl.BlockSpec((1, Tk * n_heads, V_DIM), lambda b, k: (b, k, 0)),
            pl.BlockSpec((L, D), lambda b, k: (b, 0)),
            pl.BlockSpec((L, D), lambda b, k: (b, 0)),
            pl.BlockSpec(tab.shape, lambda b, k: (0, 0)),
            pl.BlockSpec((1, V_DIM), lambda b, k: (0, 0)),
        ],
        out_specs=pl.BlockSpec((L, D), lambda b, k: (b, 0)),
        out_shape=jax.ShapeDtypeStruct((B * L, D), BF16),
        scratch_shapes=[
            pltpu.VMEM((R, Tk), F32),
            pltpu.VMEM((R, L), F32),
            pltpu.VMEM((R, 1), F32),
            pltpu.VMEM((R, 1), F32),
            pltpu.VMEM((R, V_DIM), F32),
        ],
        compiler_params=_params("arbitrary", "arbitrary"),
        name="attn_sample",
    )(lam, qbd, cache_k, cache_v.reshape(B, past * n_heads, V_DIM), k_new, v_new, tab, g_subln)


def _oproj_router_kernel(op_ref, os_ref, x_ref, wo_ref, g_ref, wrh_ref, wrl_ref,
                         x3_ref, h_ref, route_ref, route_t_ref):
    i = pl.program_id(0)
    nt = pl.num_programs(0) - 1
    o = jnp.where(i == nt, os_ref[...], op_ref[...])
    x3 = x_ref[...] + _dot(o, wo_ref[...])
    x3_ref[...] = x3
    h = _rms_unit(x3) * g_ref[...]
    h_hi = h.astype(BF16)
    h_ref[...] = h_hi
    h_lo = (h - h_hi.astype(F32)).astype(BF16)
    logits = _dot(h_hi, wrh_ref[...]) + (_dot(h_lo, wrh_ref[...]) + _dot(h_hi, wrl_ref[...]))
    lane = lax.broadcasted_iota(jnp.int32, logits.shape, 1)
    lg = jnp.where(lane < N_EXPERTS, logits, -jnp.inf)
    m1 = jnp.max(lg, axis=-1, keepdims=True)
    i1 = jnp.min(jnp.where(lg == m1, lane, LANE), axis=-1, keepdims=True)
    lg2 = jnp.where(lane == i1, -jnp.inf, lg)
    m2 = jnp.max(lg2, axis=-1, keepdims=True)
    i2 = jnp.min(jnp.where(lg2 == m2, lane, LANE), axis=-1, keepdims=True)
    e2 = jnp.exp(m2 - m1)
    den = 1.0 + e2
    g1 = 1.0 / den
    g2 = e2 / den
    route = jnp.where(lane == 0, i1.astype(F32),
                      jnp.where(lane == 1, i2.astype(F32),
                                jnp.where(lane == 2, g1,
                                          jnp.where(lane == 3, g2, 0.0))))
    route_ref[...] = route
    route_t_ref[...] = route.T


def _oproj_router(o_p, o_s, x, wo, g, wr_hi, wr_lo):
    N, D = x.shape
    T = ROW_TILE
    nt = N // T - 1
    const2 = lambda i: (0, 0)
    row = lambda i: (i, 0)
    return pl.pallas_call(
        _oproj_router_kernel,
        grid=(nt + 1,),
        in_specs=[
            pl.BlockSpec((T, D), lambda i: (jnp.minimum(i, nt - 1), 0)),
            pl.BlockSpec((T, D), const2),
            pl.BlockSpec((T, D), row),
            pl.BlockSpec((D, D), const2),
            pl.BlockSpec((1, D), const2),
            pl.BlockSpec((D, LANE), const2),
            pl.BlockSpec((D, LANE), const2),
        ],
        out_specs=[
            pl.BlockSpec((T, D), row),
            pl.BlockSpec((T, D), row),
            pl.BlockSpec((T, LANE), row),
            pl.BlockSpec((LANE, T), lambda i: (0, i)),
        ],
        out_shape=[
            jax.ShapeDtypeStruct((N, D), F32),
            jax.ShapeDtypeStruct((N, D), BF16),
            jax.ShapeDtypeStruct((N, LANE), F32),
            jax.ShapeDtypeStruct((LANE, N), F32),
        ],
        compiler_params=_params("arbitrary"),
        name="oproj_router",
    )(o_p, o_s, x, wo, g, wr_hi, wr_lo)


MOE_CHUNK = 128
MOE_ROW_ALIGN = 16


def _for_each_chunk(cnt_ref, step, n_chunks, fn):
    for e in range(N_EXPERTS):
        for k in range(n_chunks):
            @pl.when(cnt_ref[step * N_EXPERTS + e] > k * MOE_CHUNK)
            def _(e=e, k=k):
                fn(e, k)


def _dispatch_kernel(off_ref, cnt_ref, hb_ref, rt_ref, utri_ref, xs_in_ref, xs_ref, stage_ref, sem):
    del xs_in_ref
    t = pl.program_id(0)
    T = hb_ref.shape[0]
    S = MOE_CHUNK
    nk = T // S

    def chunk_copy(step, e, k):
        start = pl.multiple_of(off_ref[step * N_EXPERTS + e] + k * S, MOE_ROW_ALIGN)
        slot = e * nk + k
        return pltpu.make_async_copy(stage_ref.at[slot], xs_ref.at[pl.ds(start, S), :], sem.at[slot])

    @pl.when(t > 0)
    def _():
        _for_each_chunk(cnt_ref, t - 1, nk, lambda e, k: chunk_copy(t - 1, e, k).wait())

    rt = rt_ref[...]
    expert_id = lax.broadcasted_iota(jnp.int32, (N_EXPERTS, T), 0).astype(F32)
    member = (rt[0:1, :] == expert_id) | (rt[1:2, :] == expert_id)
    rank = _dot(member.astype(BF16), utri_ref[...])
    rank = jnp.where(member, rank, 0.0)
    hb = hb_ref[...]
    row = lax.broadcasted_iota(jnp.int32, (S, 1), 0)

    def emit(e, k):
        sel = rank[e:e + 1, :] == (row + (k * S + 1)).astype(F32)
        stage_ref[e * nk + k] = _dot(sel.astype(BF16), hb).astype(BF16)
        chunk_copy(t, e, k).start()

    _for_each_chunk(cnt_ref, t, nk, emit)

    @pl.when(t == pl.num_programs(0) - 1)
    def _():
        _for_each_chunk(cnt_ref, t, nk, lambda e, k: chunk_copy(t, e, k).wait())


def _dispatch(off, cnt, hb, route_t, utri, n_rows):
    N, D = hb.shape
    T = ROW_TILE
    nk = T // MOE_CHUNK
    n_slots = N_EXPERTS * nk
    xs_init = jnp.zeros((n_rows, D), BF16)
    return pl.pallas_call(
        _dispatch_kernel,
        grid_spec=pltpu.PrefetchScalarGridSpec(
            num_scalar_prefetch=2,
            grid=(N // T,),
            in_specs=[
                pl.BlockSpec((T, D), lambda t, off, cnt: (t, 0)),
                pl.BlockSpec((LANE, T), lambda t, off, cnt: (0, t)),
                pl.BlockSpec((T, T), lambda t, off, cnt: (0, 0)),
                pl.BlockSpec(memory_space=pl.ANY),
            ],
            out_specs=pl.BlockSpec(memory_space=pl.ANY),
            scratch_shapes=[
                pltpu.VMEM((n_slots, MOE_CHUNK, D), BF16),
                pltpu.SemaphoreType.DMA((n_slots,)),
            ],
        ),
        out_shape=jax.ShapeDtypeStruct((n_rows, D), BF16),
        input_output_aliases={5: 0},
        compiler_params=_params("arbitrary"),
        name="moe_dispatch",
    )(off, cnt, hb, route_t, utri, xs_init)


def _experts_kernel(te_ref, na_ref, xs_ref, wg_ref, wu_ref, wd_ref, out_ref, acc_ref):
    i = pl.program_id(0)
    f = pl.program_id(1)
    last = pl.num_programs(1) - 1
    active = i < na_ref[0]

    @pl.when(active)
    def _():
        @pl.when(f == 0)
        def _():
            acc_ref[...] = jnp.zeros(acc_ref.shape, F32)

        acc_ref[...] += _swiglu_tile(xs_ref[...], wg_ref[0], wu_ref[0], wd_ref[0])

        @pl.when(f == last)
        def _():
            out_ref[...] = acc_ref[...].astype(out_ref.dtype)

    @pl.when(jnp.logical_not(active) & (f == last))
    def _():
        out_ref[...] = jnp.zeros(out_ref.shape, out_ref.dtype)


def _experts(tile_expert, n_active, xs, wg, wu, wd):
    P, D = xs.shape
    T = ROW_TILE
    nf, tf = _ffn_splits(wg.shape[2])
    fidx = lambda i, f, na: jnp.where(i < na[0], f, nf - 1)
    return pl.pallas_call(
        _experts_kernel,
        grid_spec=pltpu.PrefetchScalarGridSpec(
            num_scalar_prefetch=2,
            grid=(P // T, nf),
            in_specs=[
                pl.BlockSpec((T, D), lambda i, f, te, na: (i, 0)),
                pl.BlockSpec((1, D, tf), lambda i, f, te, na: (te[i], 0, fidx(i, f, na))),
                pl.BlockSpec((1, D, tf), lambda i, f, te, na: (te[i], 0, fidx(i, f, na))),
                pl.BlockSpec((1, tf, D), lambda i, f, te, na: (te[i], fidx(i, f, na), 0)),
            ],
            out_specs=pl.BlockSpec((T, D), lambda i, f, te, na: (i, 0)),
            scratch_shapes=[pltpu.VMEM((T, D), F32)],
        ),
        out_shape=jax.ShapeDtypeStruct((P, D), BF16),
        compiler_params=_params("arbitrary", "arbitrary"),
        name="moe_experts",
    )(tile_expert, n_active, xs, wg, wu, wd)


def _combine_kernel(off_ref, cnt_ref, x_ref, route_ref, ltri_ref, ys_hbm, yp_ref, ysm_ref,
                    buf_ref, acc_ref, sem):
    t = pl.program_id(0)
    nt = pl.num_programs(0) - 1
    T = x_ref.shape[0]
    S = MOE_CHUNK
    nk = T // S

    def chunk_copy(e, k):
        start = pl.multiple_of(off_ref[t * N_EXPERTS + e] + k * S, MOE_ROW_ALIGN)
        slot = e * nk + k
        return pltpu.make_async_copy(ys_hbm.at[pl.ds(start, S), :], buf_ref.at[slot], sem.at[slot])

    _for_each_chunk(cnt_ref, t, nk, lambda e, k: chunk_copy(e, k).start())

    route = route_ref[...]
    e1, e2, g1, g2 = (route[:, j:j + 1] for j in range(2 * TOP_K))
    lane = lax.broadcasted_iota(jnp.int32, route.shape, 1).astype(F32)
    member = (lane == e1) | (lane == e2)
    rank = _dot(ltri_ref[...], member.astype(BF16))
    rank = jnp.where(member, rank, 0.0)
    col = lax.broadcasted_iota(jnp.int32, (1, S), 1)
    acc_ref[...] = x_ref[...]

    def absorb(e, k):
        chunk_copy(e, k).wait()
        sel = rank[:, e:e + 1] == (col + (k * S + 1)).astype(F32)
        gate = jnp.where(e1 == e, g1, 0.0) + jnp.where(e2 == e, g2, 0.0)
        acc_ref[...] += gate * _dot(sel.astype(BF16), buf_ref[e * nk + k])

    _for_each_chunk(cnt_ref, t, nk, absorb)

    @pl.when(t < nt)
    def _():
        yp_ref[...] = acc_ref[...]

    @pl.when(t == nt)
    def _():
        ysm_ref[...] = acc_ref[...]


def _combine(off, cnt, x3, route, ltri, ys, n_sample):
    N, D = x3.shape
    T = ROW_TILE
    nt = N // T - 1
    n_slots = N_EXPERTS * (T // MOE_CHUNK)
    return pl.pallas_call(
        _combine_kernel,
        grid_spec=pltpu.PrefetchScalarGridSpec(
            num_scalar_prefetch=2,
            grid=(nt + 1,),
            in_specs=[
                pl.BlockSpec((T, D), lambda t, off, cnt: (t, 0)),
                pl.BlockSpec((T, LANE), lambda t, off, cnt: (t, 0)),
                pl.BlockSpec((T, T), lambda t, off, cnt: (0, 0)),
                pl.BlockSpec(memory_space=pl.ANY),
            ],
            out_specs=[
                pl.BlockSpec((T, D), lambda t, off, cnt: (jnp.minimum(t, nt - 1), 0)),
                pl.BlockSpec((T, D), lambda t, off, cnt: (0, 0)),
            ],
            scratch_shapes=[
                pltpu.VMEM((n_slots, MOE_CHUNK, D), BF16),
                pltpu.VMEM((T, D), F32),
                pltpu.SemaphoreType.DMA((n_slots,)),
            ],
        ),
        out_shape=[
            jax.ShapeDtypeStruct((N - n_sample, D), F32),
            jax.ShapeDtypeStruct((n_sample, D), F32),
        ],
        compiler_params=_params("arbitrary"),
        name="moe_combine",
    )(off, cnt, x3, route, ltri, ys)


def _routing_tables(route, tile):
    n = route.shape[0]
    n_tok_tiles = n // tile
    experts = route[:, :TOP_K].astype(jnp.int32)
    onehot = (experts[:, :, None] == jnp.arange(N_EXPERTS)[None, None, :]).astype(jnp.int32).sum(axis=1)
    cnt = onehot.reshape(n_tok_tiles, tile, N_EXPERTS).sum(axis=1)
    span = (cnt + MOE_ROW_ALIGN - 1) // MOE_ROW_ALIGN * MOE_ROW_ALIGN
    totals = span.sum(axis=0)
    tiles_per = (totals + MOE_CHUNK + tile - 1) // tile
    tile_end = jnp.cumsum(tiles_per)
    group_off = (tile_end - tiles_per) * tile
    off = group_off[None, :] + jnp.cumsum(span, axis=0) - span
    max_rows = n * TOP_K + N_EXPERTS * (MOE_CHUNK + n_tok_tiles * (MOE_ROW_ALIGN - 1))
    n_row_tiles = max_rows // tile + N_EXPERTS
    n_active = tile_end[-1:].astype(jnp.int32)
    tile_ids = jnp.minimum(jnp.arange(n_row_tiles, dtype=jnp.int32), n_active[0] - 1)
    tile_expert = jnp.sum((tile_ids[:, None] >= tile_end[None, :]).astype(jnp.int32), axis=1)
    return (off.reshape(-1).astype(jnp.int32), cnt.reshape(-1).astype(jnp.int32),
            tile_expert, n_active, n_row_tiles * tile)


def _group_matrices(d_model):
    n_groups = d_model // HEAD_DIM
    gsum = np.zeros((d_model, LANE), np.float32)
    gsum[np.arange(d_model), np.arange(d_model) // HEAD_DIM] = 1.0
    assert n_groups <= LANE
    return jnp.asarray(gsum, BF16), jnp.asarray(gsum.T.copy(), BF16)


def kernel(x_prompt, x_sample, cache_k, cache_v, state_pool, g_pool_norm, w_pool, pool_scale, g_attn, w_q, g_qn, lambda_q1, lambda_k1, lambda_q2, lambda_k2, g_subln, w_o, g_kv, w_k, w_v, g_kn, rel_bias, g_ffn, w_gate_dense, w_up_dense, w_down_dense, w_router, w_gate_moe, w_up_moe, w_down_moe):
    Bp, Lp, D = x_prompt.shape
    Bs, Ls, _ = x_sample.shape
    past = cache_k.shape[1]
    n_heads = D // V_DIM
    n_sample = Bs * Ls
    assert Bp == 1 and n_sample == ROW_TILE and Lp % ROW_TILE == 0
    assert g_pool_norm.shape[0] == 1 and g_attn.shape[0] == 1
    bf = lambda a: a.astype(BF16)
    row = lambda a: a.reshape(1, -1)

    x1, pool_p, pool_s = _pool_layer(
        x_prompt.reshape(Lp, D), x_sample.reshape(n_sample, D), state_pool[0],
        row(g_pool_norm[0]), bf(w_pool[0]), row(pool_scale[0]), past)
    x2 = _dense_ffn(x1, row(g_ffn[0]), bf(w_gate_dense[0]), bf(w_up_dense[0]), bf(w_down_dense[0]))

    layer = 1
    lambda_init = 0.8 - 0.6 * math.exp(-0.3 * layer)
    lam = (jnp.exp(jnp.sum(lambda_q1[0] * lambda_k1[0])) - jnp.exp(jnp.sum(lambda_q2[0] * lambda_k2[0]))
           + lambda_init).reshape(1)
    gsum, gbc = _group_matrices(D)
    n_groups = D // HEAD_DIM
    k_p, v_p, k_s, v_s, qh, kh, vh, q_s = _qkv_proj(
        x2, row(g_kv), row(g_attn[0]), bf(w_k), bf(w_v), bf(w_q[0]),
        row(jnp.tile(g_kn, n_groups)), row(jnp.tile(g_qn[0], n_groups) * (HEAD_DIM ** -0.5 * LOG2E)),
        gsum, gbc, n_sample)
    out_scale = 1.0 - lambda_init
    g_sub = row(g_subln[0])
    score_bound = LOG2E * (math.sqrt(HEAD_DIM) * jnp.max(jnp.abs(g_qn[0])) * jnp.max(jnp.abs(g_kn))
                           + jnp.max(jnp.abs(rel_bias)))
    o_p = lax.cond(
        score_bound <= UNSHIFTED_SCORE_LIMIT,
        functools.partial(_attn_prompt, out_scale=out_scale, online=False),
        functools.partial(_attn_prompt, out_scale=out_scale, online=True),
        rel_bias, lam, qh, kh, vh, g_sub)

    q4 = q_s.reshape(Bs, Ls, n_groups, HEAD_DIM).transpose(0, 2, 1, 3)
    eye = jnp.eye(n_groups, dtype=BF16)
    qbd = (q4[:, :, :, None, :] * eye[None, :, None, :, None]).reshape(Bs, n_groups * Ls, D)
    tab = jnp.repeat(rel_bias.T, 2 * Ls, axis=0) * LOG2E
    o_s = _attn_sample(lam, qbd, bf(cache_k.reshape(Bs, past, D)), cache_v,
                       k_s, v_s, tab, g_sub, out_scale)

    wr = jnp.pad(w_router[0], ((0, 0), (0, LANE - N_EXPERTS)))
    wr_hi = bf(wr)
    wr_lo = bf(wr - wr_hi.astype(F32))
    x3, h_moe, route, route_t = _oproj_router(o_p, o_s, x2, bf(w_o[0]), row(g_ffn[1]), wr_hi, wr_lo)
    off, cnt, tile_expert, n_active, n_rows = _routing_tables(route, ROW_TILE)
    ltri = jnp.asarray(np.tril(np.ones((ROW_TILE, ROW_TILE), np.float32)), BF16)
    xs = _dispatch(off, cnt, h_moe, route_t, ltri.T, n_rows)
    ys = _experts(tile_expert, n_active, xs, bf(w_gate_moe[0]), bf(w_up_moe[0]), bf(w_down_moe[0]))
    y_p, y_s = _combine(off, cnt, x3, route, ltri, ys, n_sample)

    return (y_p.reshape(Bp, Lp, D), y_s.reshape(Bs, Ls, D),
            k_p.reshape(Bp, Lp, n_heads, 2, HEAD_DIM), v_p.reshape(Bp, Lp, n_heads, V_DIM),
            pool_p.reshape(1, Bp, POOL_STATE, D),
            k_s.reshape(Bs, Ls, n_heads, 2, HEAD_DIM), v_s.reshape(Bs, Ls, n_heads, V_DIM),
            pool_s.reshape(1, Bs, POOL_STATE, D))
```

```python
import functools
import math

import numpy as np
import jax
import jax.numpy as jnp
from jax import lax
from jax.experimental import pallas as pl
from jax.experimental.pallas import tpu as pltpu

EPS = 1e-6
CHUNK = 64
POOL_WINDOWS = (2, 4, 8, 16)
POOL_STATE = max(POOL_WINDOWS) - 1
HEAD_DIM = 64
V_DIM = 2 * HEAD_DIM
N_EXPERTS = 8
TOP_K = 2
MAX_EXACT = 8
BUCKET_UPPER = (1, 2, 3, 4, 5, 6, 7, 8, 12, 16, 23, 32, 46, 64, 91)
FAR_DISTANCE = 128
NEG = -1e30
LOG2E = math.log2(math.e)
UNSHIFTED_SCORE_LIMIT = 80.0

ROW_TILE = 512
FAR_TILES_PER_TRIP = 4
LANE = 128
VMEM_LIMIT = 56 * 1024 * 1024

F32 = jnp.float32
BF16 = jnp.bfloat16


def _dot(a, b):
    return jnp.dot(a, b, preferred_element_type=F32)


def _dot_nt(a, b):
    return lax.dot_general(a, b, (((1,), (1,)), ((), ())), preferred_element_type=F32)


def _rms_unit(x):
    return x * lax.rsqrt(jnp.mean(x * x, axis=-1, keepdims=True) + EPS)


def _rel_bias_tile(rel, tab):
    n = jnp.abs(rel)
    neg = tab(15)
    pos = tab(31)
    for b in range(14, -1, -1):
        lt = n < BUCKET_UPPER[b]
        neg = jnp.where(lt, tab(b), neg)
        pos = jnp.where(lt, tab(16 + b), pos)
    return jnp.where(rel > 0, pos, neg)


def _params(*sem):
    return pltpu.CompilerParams(dimension_semantics=sem, vmem_limit_bytes=VMEM_LIMIT)


def _pool_kernel(xp_ref, xs_ref, st_ref, g_ref, w_ref, sc_ref,
                 x1_ref, pp_ref, ps_ref, ext_ref, ext3_ref, *, past_len):
    i = pl.program_id(0)
    nt = pl.num_programs(0) - 1
    T, D = xp_ref.shape
    gw = D // len(POOL_WINDOWS)

    @pl.when(i < nt)
    def _prompt():
        x = xp_ref[...]
        h = _rms_unit(x) * g_ref[...]

        @pl.when(i == 0)
        def _():
            ext_ref[0:16, :] = jnp.zeros((16, D), F32)

        ext_ref[16:16 + T, :] = h
        row = i * T + lax.broadcasted_iota(jnp.int32, (T, 1), 0)
        parts = []
        for gi, w in enumerate(POOL_WINDOWS):
            c0 = gi * gw
            s = ext_ref[16:16 + T, c0:c0 + gw]
            for j in range(1, w):
                s = s + ext_ref[16 - j:16 - j + T, c0:c0 + gw]
            cnt = jnp.minimum(w, row + 1).astype(F32)
            pooled = s / cnt - h[:, c0:c0 + gw]
            parts.append(_dot(pooled.astype(BF16), w_ref[gi]))
        mix = jnp.concatenate(parts, axis=-1) * sc_ref[...]
        x1_ref[...] = x + mix
        tail = ext_ref[T:T + 16, :]
        ext_ref[0:16, :] = tail

        @pl.when(i == nt - 1)
        def _():
            pp_ref[...] = tail[1:16, :]

    @pl.when(i == nt)
    def _sample():
        B = st_ref.shape[0]
        L = T // B
        x = xs_ref[...]
        h = _rms_unit(x) * g_ref[...]
        ext3_ref[:, 1:16, :] = st_ref[...]
        ext3_ref[:, 16:16 + L, :] = h.reshape(B, L, D)
        t = lax.broadcasted_iota(jnp.int32, (1, L, 1), 1)
        parts = []
        for gi, w in enumerate(POOL_WINDOWS):
            c0 = gi * gw
            s = ext3_ref[:, 16:16 + L, c0:c0 + gw]
            for j in range(1, w):
                s = s + ext3_ref[:, 16 - j:16 - j + L, c0:c0 + gw]
            cnt = jnp.minimum(w, past_len + t + 1).astype(F32)
            pooled = (s / cnt).reshape(T, gw) - h[:, c0:c0 + gw]
            parts.append(_dot(pooled.astype(BF16), w_ref[gi]))
        mix = jnp.concatenate(parts, axis=-1) * sc_ref[...]
        x1_ref[...] = x + mix
        ps_ref[...] = ext3_ref[:, 16 + L - POOL_STATE:16 + L, :]


def _pool_layer(xp, xs, state, g, w, sc, past_len):
    Lp, D = xp.shape
    T = ROW_TILE
    nt = Lp // T
    B = state.shape[0]
    L = xs.shape[0] // B
    return pl.pallas_call(
        functools.partial(_pool_kernel, past_len=past_len),
        grid=(nt + 1,),
        in_specs=[
            pl.BlockSpec((T, D), lambda i: (jnp.minimum(i, nt - 1), 0)),
            pl.BlockSpec((T, D), lambda i: (0, 0)),
            pl.BlockSpec((B, POOL_STATE, D), lambda i: (0, 0, 0)),
            pl.BlockSpec((1, D), lambda i: (0, 0)),
            pl.BlockSpec(w.shape, lambda i: (0, 0, 0)),
            pl.BlockSpec((1, D), lambda i: (0, 0)),
        ],
        out_specs=[
            pl.BlockSpec((T, D), lambda i: (i, 0)),
            pl.BlockSpec((POOL_STATE, D), lambda i: (0, 0)),
            pl.BlockSpec((B, POOL_STATE, D), lambda i: (0, 0, 0)),
        ],
        out_shape=[
            jax.ShapeDtypeStruct((Lp + T, D), F32),
            jax.ShapeDtypeStruct((POOL_STATE, D), F32),
            jax.ShapeDtypeStruct((B, POOL_STATE, D), F32),
        ],
        scratch_shapes=[
            pltpu.VMEM((16 + T, D), F32),
            pltpu.VMEM((B, 16 + L, D), F32),
        ],
        compiler_params=_params("arbitrary"),
        name="pool_mixer",
    )(xp, xs, state, g, w, sc)


def _swiglu_tile(hb, wg, wu, wd):
    gt = _dot(hb, wg)
    ut = _dot(hb, wu)
    a = gt * jax.nn.sigmoid(gt) * ut
    return _dot(a.astype(BF16), wd)


def _ffn_kernel(x_ref, g_ref, wg_ref, wu_ref, wd_ref, out_ref, hb_ref, acc_ref):
    f = pl.program_id(1)

    @pl.when(f == 0)
    def _():
        x = x_ref[...]
        hb_ref[...] = (_rms_unit(x) * g_ref[...]).astype(BF16)
        acc_ref[...] = x

    acc_ref[...] += _swiglu_tile(hb_ref[...], wg_ref[...], wu_ref[...], wd_ref[...])

    @pl.when(f == pl.num_programs(1) - 1)
    def _():
        out_ref[...] = acc_ref[...]


def _ffn_splits(d_ff):
    nf = 2 if (d_ff // 2) % LANE == 0 else 1
    return nf, d_ff // nf


def _dense_ffn(x, g, wg, wu, wd):
    N, D = x.shape
    T = ROW_TILE
    nf, tf = _ffn_splits(wg.shape[1])
    return pl.pallas_call(
        _ffn_kernel,
        grid=(N // T, nf),
        in_specs=[
            pl.BlockSpec((T, D), lambda i, f: (i, 0)),
            pl.BlockSpec((1, D), lambda i, f: (0, 0)),
            pl.BlockSpec((D, tf), lambda i, f: (0, f)),
            pl.BlockSpec((D, tf), lambda i, f: (0, f)),
            pl.BlockSpec((tf, D), lambda i, f: (f, 0)),
        ],
        out_specs=pl.BlockSpec((T, D), lambda i, f: (i, 0)),
        out_shape=jax.ShapeDtypeStruct((N, D), F32),
        scratch_shapes=[pltpu.VMEM((T, D), BF16), pltpu.VMEM((T, D), F32)],
        compiler_params=_params("arbitrary", "arbitrary"),
        name="dense_swiglu",
    )(x, g, wg, wu, wd)


def _qkv_kernel(x_ref, gkv_ref, gq_ref, wk_ref, wv_ref, wq_ref, gkn_ref, gqn_ref,
                gsum_ref, gbc_ref,
                kp_ref, vp_ref, ks_ref, vs_ref, qh_ref, kh_ref, vh_ref, qs_ref):
    i = pl.program_id(0)
    nt = pl.num_programs(0) - 1
    n_heads = qh_ref.shape[0]

    xn = _rms_unit(x_ref[...])
    hkv = (xn * gkv_ref[...]).astype(BF16)
    hq = (xn * gq_ref[...]).astype(BF16)

    def head_norm(y, g):
        ssq = _dot((y * y).astype(BF16), gsum_ref[...])
        rs = lax.rsqrt(ssq * (1.0 / HEAD_DIM) + EPS)
        rs_hi = rs.astype(BF16)
        rs_lo = (rs - rs_hi.astype(F32)).astype(BF16)
        rsb = _dot(rs_hi, gbc_ref[...]) + _dot(rs_lo, gbc_ref[...])
        return y * rsb * g

    k = head_norm(_dot(hkv, wk_ref[...]), gkn_ref[...])
    v = _dot(hkv, wv_ref[...])
    q = head_norm(_dot(hq, wq_ref[...]), gqn_ref[...])

    @pl.when(i < nt)
    def _():
        kp_ref[...] = k
        vp_ref[...] = v
        for h in range(n_heads):
            sl = slice(h * V_DIM, (h + 1) * V_DIM)
            qh_ref[h] = q[:, sl].astype(BF16)
            kh_ref[h] = k[:, sl].T.astype(BF16)
            vh_ref[h] = v[:, sl].astype(BF16)

    @pl.when(i == nt)
    def _():
        ks_ref[...] = k
        vs_ref[...] = v
        qs_ref[...] = q.astype(BF16)


def _qkv_proj(x, gkv, gq, wk, wv, wq, gkn_t, gqn_t, gsum, gbc, n_sample):
    N, D = x.shape
    T = ROW_TILE
    nt = N // T - 1
    Lp = N - n_sample
    n_heads = D // V_DIM
    const2 = lambda i: (0, 0)
    prow = lambda i: (jnp.minimum(i, nt - 1), 0)
    phead = lambda i: (0, jnp.minimum(i, nt - 1), 0)
    return pl.pallas_call(
        _qkv_kernel,
        grid=(nt + 1,),
        in_specs=[
            pl.BlockSpec((T, D), lambda i: (i, 0)),
            pl.BlockSpec((1, D), const2),
            pl.BlockSpec((1, D), const2),
            pl.BlockSpec((D, D), const2),
            pl.BlockSpec((D, D), const2),
            pl.BlockSpec((D, D), const2),
            pl.BlockSpec((1, D), const2),
            pl.BlockSpec((1, D), const2),
            pl.BlockSpec(gsum.shape, const2),
            pl.BlockSpec(gbc.shape, const2),
        ],
        out_specs=[
            pl.BlockSpec((T, D), prow),
            pl.BlockSpec((T, D), prow),
            pl.BlockSpec((T, D), const2),
            pl.BlockSpec((T, D), const2),
            pl.BlockSpec((n_heads, T, V_DIM), phead),
            pl.BlockSpec((n_heads, V_DIM, T), lambda i: (0, 0, jnp.minimum(i, nt - 1))),
            pl.BlockSpec((n_heads, T, V_DIM), phead),
            pl.BlockSpec((T, D), const2),
        ],
        out_shape=[
            jax.ShapeDtypeStruct((Lp, D), F32),
            jax.ShapeDtypeStruct((Lp, D), F32),
            jax.ShapeDtypeStruct((n_sample, D), F32),
            jax.ShapeDtypeStruct((n_sample, D), F32),
            jax.ShapeDtypeStruct((n_heads, Lp, V_DIM), BF16),
            jax.ShapeDtypeStruct((n_heads, V_DIM, Lp), BF16),
            jax.ShapeDtypeStruct((n_heads, Lp, V_DIM), BF16),
            jax.ShapeDtypeStruct((n_sample, D), BF16),
        ],
        compiler_params=_params("arbitrary"),
        name="qkv_proj",
    )(x, gkv, gq, wk, wv, wq, gkn_t, gqn_t, gsum, gbc)


def _diff_out(o0, o1, lam, g, out_scale):
    o = o0 - lam * o1
    return _rms_unit(o) * g * out_scale


def _lane_partial_sum(p):
    out = p[:, 0:LANE]
    for k in range(1, p.shape[1] // LANE):
        out = out + p[:, k * LANE:(k + 1) * LANE]
    return out


def _attn_prompt_kernel(bias_ref, lam_ref, q_ref, kt_ref, v_ref, g_ref, o_ref,
                        bd_ref, bs_ref, l_ref, acc_ref, *m_scratch, out_scale):
    h = pl.program_id(0)
    i = pl.program_id(1)
    T = q_ref.shape[1]
    online = bool(m_scratch)

    @pl.when(i == 0)
    def _():
        row = lax.broadcasted_iota(jnp.int32, (T, T), 0)
        col = lax.broadcasted_iota(jnp.int32, (T, T), 1)
        tab = lambda b: bias_ref[b, h] * LOG2E
        visible = (col // CHUNK) <= (row // CHUNK)
        bd_ref[...] = jnp.where(visible, _rel_bias_tile(col - row, tab), NEG)
        bs_ref[...] = _rel_bias_tile(col - row - T, tab)

    if online:
        m_ref, = m_scratch
        m_ref[...] = jnp.full(m_ref.shape, NEG, F32)
    l_ref[...] = jnp.zeros(l_ref.shape, F32)
    acc_ref[...] = jnp.zeros(acc_ref.shape, F32)
    q = q_ref[0]
    qc = (q[:, :HEAD_DIM], q[:, HEAD_DIM:])

    def update(j, bias, width=T):
        start = pl.multiple_of(j * T, T)
        vt = v_ref[0, pl.ds(start, width), :]
        for c in range(2):
            kt = kt_ref[0, c * HEAD_DIM:(c + 1) * HEAD_DIM, pl.ds(start, width)]
            s = _dot(qc[c], kt) + bias
            if online:
                m_old = m_ref[c]
                m_new = jnp.maximum(m_old, jnp.max(s, axis=-1, keepdims=True))
                alpha = jnp.exp2(m_old - m_new)
                p = jnp.exp2(s - m_new)
                l_ref[c] = alpha * l_ref[c] + _lane_partial_sum(p)
                acc_ref[c] = alpha * acc_ref[c] + _dot(p.astype(BF16), vt)
                m_ref[c] = m_new
            else:
                p = jnp.exp2(s)
                l_ref[c] += _lane_partial_sum(p)
                acc_ref[c] += _dot(p.astype(BF16), vt)

    far_bias = bias_ref[15, h] * LOG2E

    n_far = jnp.maximum(i - 1, 0)

    def far_body(jj, carry):
        update(FAR_TILES_PER_TRIP * jj, far_bias, width=FAR_TILES_PER_TRIP * T)
        return carry

    def far_rest(j, carry):
        update(j, far_bias)
        return carry

    n_trips = n_far // FAR_TILES_PER_TRIP
    lax.fori_loop(0, n_trips, far_body, 0)
    lax.fori_loop(n_trips * FAR_TILES_PER_TRIP, n_far, far_rest, 0)

    @pl.when(i >= 1)
    def _():
        update(i - 1, bs_ref[...])

    update(i, bd_ref[...])

    o0 = acc_ref[0] / jnp.sum(l_ref[0], axis=-1, keepdims=True)
    o1 = acc_ref[1] / jnp.sum(l_ref[1], axis=-1, keepdims=True)
    o_ref[...] = _diff_out(o0, o1, lam_ref[0], g_ref[...], out_scale).astype(o_ref.dtype)


def _attn_prompt(rel_bias, lam, qh, kth, vh, g_subln, out_scale, online):
    n_heads, Lp, _ = qh.shape
    T = ROW_TILE
    assert T % CHUNK == 0 and T >= FAR_DISTANCE
    m_scratch = [pltpu.VMEM((2, T, 1), F32)] if online else []
    return pl.pallas_call(
        functools.partial(_attn_prompt_kernel, out_scale=out_scale),
        grid=(n_heads, Lp // T),
        in_specs=[
            pl.BlockSpec(memory_space=pltpu.SMEM),
            pl.BlockSpec(memory_space=pltpu.SMEM),
            pl.BlockSpec((1, T, V_DIM), lambda h, i: (h, i, 0)),
            pl.BlockSpec((1, V_DIM, Lp), lambda h, i: (h, 0, 0)),
            pl.BlockSpec((1, Lp, V_DIM), lambda h, i: (h, 0, 0)),
            pl.BlockSpec((1, V_DIM), lambda h, i: (0, 0)),
        ],
        out_specs=pl.BlockSpec((T, V_DIM), lambda h, i: (i, h)),
        out_shape=jax.ShapeDtypeStruct((Lp, n_heads * V_DIM), BF16),
        scratch_shapes=[
            pltpu.VMEM((T, T), F32),
            pltpu.VMEM((T, T), F32),
            pltpu.VMEM((2, T, LANE), F32),
            pltpu.VMEM((2, T, V_DIM), F32),
        ] + m_scratch,
        compiler_params=_params("arbitrary", "arbitrary"),
        name="attn_prompt_online" if online else "attn_prompt",
    )(rel_bias, lam, qh, kth, vh, g_subln)


def _attn_sample_kernel(lam_ref, qbd_ref, ck_hbm, cv_ref, kn_ref, vn_ref, tab_ref, g_ref,
                        o_ref, nearb_ref, newb_ref, m_ref, l_ref, acc_ref, kbuf_ref, ksem, *, out_scale):
    b = pl.program_id(0)
    kb = pl.program_id(1)
    nb = pl.num_programs(0)
    nkb = pl.num_programs(1)
    R, Tk = nearb_ref.shape
    L = kn_ref.shape[0]
    n_heads = o_ref.shape[1] // V_DIM
    n_comp = 2 * n_heads

    def k_copy(step, slot, hc):
        bb, kk = step // nkb, step % nkb
        return pltpu.make_async_copy(
            ck_hbm.at[bb, pl.ds(kk * Tk, Tk), hc // 2, hc % 2, :], kbuf_ref.at[slot, hc], ksem.at[slot, hc])

    step = b * nkb + kb
    slot = step % 2

    @pl.when(step == 0)
    def _():
        for hc in range(n_comp):
            k_copy(step, slot, hc).start()

    @pl.when(step + 1 < nb * nkb)
    def _():
        for hc in range(n_comp):
            k_copy(step + 1, 1 - slot, hc).start()

    @pl.when(step == 0)
    def _():
        tab = lambda bkt: tab_ref[:, bkt:bkt + 1]
        t_near = lax.broadcasted_iota(jnp.int32, (R, Tk), 0) % L
        col = lax.broadcasted_iota(jnp.int32, (R, Tk), 1)
        nearb_ref[...] = _rel_bias_tile(col - Tk - t_near, tab)
        t_new = lax.broadcasted_iota(jnp.int32, (R, L), 0) % L
        col_new = lax.broadcasted_iota(jnp.int32, (R, L), 1)
        newb_ref[...] = _rel_bias_tile(col_new - t_new, tab)

    @pl.when(kb == 0)
    def _():
        m_ref[...] = jnp.full(m_ref.shape, NEG, F32)
        l_ref[...] = jnp.zeros(l_ref.shape, F32)
        acc_ref[...] = jnp.zeros(acc_ref.shape, F32)

    rows_per_head = 2 * L

    def update(s, value_of_head):
        m_old = m_ref[...]
        m_new = jnp.maximum(m_old, jnp.max(s, axis=-1, keepdims=True))
        alpha = jnp.exp2(m_old - m_new)
        p = jnp.exp2(s - m_new)
        l_ref[...] = alpha * l_ref[...] + jnp.sum(p, axis=-1, keepdims=True)
        pb = p.astype(BF16)
        for h in range(n_heads):
            rs = slice(h * rows_per_head, (h + 1) * rows_per_head)
            acc_ref[rs, :] = alpha[rs, :] * acc_ref[rs, :] + _dot(pb[rs, :], value_of_head(h))
        m_ref[...] = m_new

    for hc in range(n_comp):
        k_copy(step, slot, hc).wait()
    qbd = qbd_ref[0]
    keys = jnp.concatenate([kbuf_ref[slot, hc].astype(BF16) for hc in range(n_comp)], axis=-1)
    s = _dot_nt(qbd, keys)
    cache_value = lambda h: cv_ref[0, pl.ds(h, Tk, stride=n_heads), :].astype(BF16)

    @pl.when(kb < nkb - 1)
    def _():
        update(s + tab_ref[:, 15:16], cache_value)

    @pl.when(kb == nkb - 1)
    def _():
        update(s + nearb_ref[...], cache_value)
        s_new = _dot_nt(qbd, kn_ref[...].astype(BF16)) + newb_ref[...]
        update(s_new, lambda h: vn_ref[:, h * V_DIM:(h + 1) * V_DIM].astype(BF16))
        lam = lam_ref[0]
        o = acc_ref[...] / l_ref[...]
        for h in range(n_heads):
            r0 = h * rows_per_head
            o_ref[:, h * V_DIM:(h + 1) * V_DIM] = _diff_out(
                o[r0:r0 + L, :], o[r0 + L:r0 + 2 * L, :], lam, g_ref[...], out_scale).astype(o_ref.dtype)


def _attn_sample(lam, qbd, cache_k, cache_v, k_new, v_new, tab, g_subln, out_scale):
    B, past, n_heads = cache_v.shape[:3]
    R, D = qbd.shape[1:]
    L = k_new.shape[0] // B
    Tk = min(1024, past)
    assert past % Tk == 0 and Tk >= FAR_DISTANCE + L
    return pl.pallas_call(
        functools.partial(_attn_sample_kernel, out_scale=out_scale),
        grid=(B, past // Tk),
        in_specs=[
            pl.BlockSpec(memory_space=pltpu.SMEM),
            pl.BlockSpec((1, R, D), lambda b, k: (b, 0, 0)),
            pl.BlockSpec(memory_space=pl.ANY),
            pl.BlockSpec((1, Tk * n_heads, V_DIM), lambda b, k: (b, k, 0)),
            pl.BlockSpec((L, D), lambda b, k: (b, 0)),
            pl.BlockSpec((L, D), lambda b, k: (b, 0)),
            pl.BlockSpec(tab.shape, lambda b, k: (0, 0)),
            pl.BlockSpec((1, V_DIM), lambda b, k: (0, 0)),
        ],
        out_specs=pl.BlockSpec((L, D), lambda b, k: (b, 0)),
        out_shape=jax.ShapeDtypeStruct((B * L, D), BF16),
        scratch_shapes=[
            pltpu.VMEM((R, Tk), F32),
            pltpu.VMEM((R, L), F32),
            pltpu.VMEM((R, 1), F32),
            pltpu.VMEM((R, 1), F32),
            pltpu.VMEM((R, V_DIM), F32),
            pltpu.VMEM((2, 2 * n_heads, Tk, HEAD_DIM), F32),
            pltpu.SemaphoreType.DMA((2, 2 * n_heads)),
        ],
        compiler_params=_params("arbitrary", "arbitrary"),
        name="attn_sample",
    )(lam, qbd, cache_k, cache_v.reshape(B, past * n_heads, V_DIM), k_new, v_new, tab, g_subln)


def _oproj_router_kernel(op_ref, os_ref, x_ref, wo_ref, g_ref, wrh_ref, wrl_ref,
                         x3_ref, h_ref, route_ref, route_t_ref):
    i = pl.program_id(0)
    nt = pl.num_programs(0) - 1
    o = jnp.where(i == nt, os_ref[...], op_ref[...])
    x3 = x_ref[...] + _dot(o, wo_ref[...])
    x3_ref[...] = x3
    h = _rms_unit(x3) * g_ref[...]
    h_hi = h.astype(BF16)
    h_ref[...] = h_hi
    h_lo = (h - h_hi.astype(F32)).astype(BF16)
    logits = _dot(h_hi, wrh_ref[...]) + (_dot(h_lo, wrh_ref[...]) + _dot(h_hi, wrl_ref[...]))
    lane = lax.broadcasted_iota(jnp.int32, logits.shape, 1)
    lg = jnp.where(lane < N_EXPERTS, logits, -jnp.inf)
    m1 = jnp.max(lg, axis=-1, keepdims=True)
    i1 = jnp.min(jnp.where(lg == m1, lane, LANE), axis=-1, keepdims=True)
    lg2 = jnp.where(lane == i1, -jnp.inf, lg)
    m2 = jnp.max(lg2, axis=-1, keepdims=True)
    i2 = jnp.min(jnp.where(lg2 == m2, lane, LANE), axis=-1, keepdims=True)
    e2 = jnp.exp(m2 - m1)
    den = 1.0 + e2
    g1 = 1.0 / den
    g2 = e2 / den
    route = jnp.where(lane == 0, i1.astype(F32),
                      jnp.where(lane == 1, i2.astype(F32),
                                jnp.where(lane == 2, g1,
                                          jnp.where(lane == 3, g2, 0.0))))
    route_ref[...] = route
    route_t_ref[...] = route.T


def _oproj_router(o_p, o_s, x, wo, g, wr_hi, wr_lo):
    N, D = x.shape
    T = ROW_TILE
    nt = N // T - 1
    const2 = lambda i: (0, 0)
    row = lambda i: (i, 0)
    return pl.pallas_call(
        _oproj_router_kernel,
        grid=(nt + 1,),
        in_specs=[
            pl.BlockSpec((T, D), lambda i: (jnp.minimum(i, nt - 1), 0)),
            pl.BlockSpec((T, D), const2),
            pl.BlockSpec((T, D), row),
            pl.BlockSpec((D, D), const2),
            pl.BlockSpec((1, D), const2),
            pl.BlockSpec((D, LANE), const2),
            pl.BlockSpec((D, LANE), const2),
        ],
        out_specs=[
            pl.BlockSpec((T, D), row),
            pl.BlockSpec((T, D), row),
            pl.BlockSpec((T, LANE), row),
            pl.BlockSpec((LANE, T), lambda i: (0, i)),
        ],
        out_shape=[
            jax.ShapeDtypeStruct((N, D), F32),
            jax.ShapeDtypeStruct((N, D), BF16),
            jax.ShapeDtypeStruct((N, LANE), F32),
            jax.ShapeDtypeStruct((LANE, N), F32),
        ],
        compiler_params=_params("arbitrary"),
        name="oproj_router",
    )(o_p, o_s, x, wo, g, wr_hi, wr_lo)


MOE_CHUNK = 128
MOE_ROW_ALIGN = 16


def _for_each_chunk(cnt_ref, step, n_chunks, fn):
    for e in range(N_EXPERTS):
        for k in range(n_chunks):
            @pl.when(cnt_ref[step * N_EXPERTS + e] > k * MOE_CHUNK)
            def _(e=e, k=k):
                fn(e, k)


def _dispatch_kernel(off_ref, cnt_ref, hb_ref, rt_ref, utri_ref, xs_in_ref, xs_ref, stage_ref, sem):
    del xs_in_ref
    t = pl.program_id(0)
    T = hb_ref.shape[0]
    S = MOE_CHUNK
    nk = T // S

    def chunk_copy(step, e, k):
        start = pl.multiple_of(off_ref[step * N_EXPERTS + e] + k * S, MOE_ROW_ALIGN)
        slot = e * nk + k
        return pltpu.make_async_copy(stage_ref.at[slot], xs_ref.at[pl.ds(start, S), :], sem.at[slot])

    @pl.when(t > 0)
    def _():
        _for_each_chunk(cnt_ref, t - 1, nk, lambda e, k: chunk_copy(t - 1, e, k).wait())

    rt = rt_ref[...]
    expert_id = lax.broadcasted_iota(jnp.int32, (N_EXPERTS, T), 0).astype(F32)
    member = (rt[0:1, :] == expert_id) | (rt[1:2, :] == expert_id)
    rank = _dot(member.astype(BF16), utri_ref[...])
    rank = jnp.where(member, rank, 0.0)
    hb = hb_ref[...]
    row = lax.broadcasted_iota(jnp.int32, (S, 1), 0)

    def emit(e, k):
        sel = rank[e:e + 1, :] == (row + (k * S + 1)).astype(F32)
        stage_ref[e * nk + k] = _dot(sel.astype(BF16), hb).astype(BF16)
        chunk_copy(t, e, k).start()

    _for_each_chunk(cnt_ref, t, nk, emit)

    @pl.when(t == pl.num_programs(0) - 1)
    def _():
        _for_each_chunk(cnt_ref, t, nk, lambda e, k: chunk_copy(t, e, k).wait())


def _dispatch(off, cnt, hb, route_t, utri, n_rows):
    N, D = hb.shape
    T = ROW_TILE
    nk = T // MOE_CHUNK
    n_slots = N_EXPERTS * nk
    xs_init = jnp.zeros((n_rows, D), BF16)
    return pl.pallas_call(
        _dispatch_kernel,
        grid_spec=pltpu.PrefetchScalarGridSpec(
            num_scalar_prefetch=2,
            grid=(N // T,),
            in_specs=[
                pl.BlockSpec((T, D), lambda t, off, cnt: (t, 0)),
                pl.BlockSpec((LANE, T), lambda t, off, cnt: (0, t)),
                pl.BlockSpec((T, T), lambda t, off, cnt: (0, 0)),
                pl.BlockSpec(memory_space=pl.ANY),
            ],
            out_specs=pl.BlockSpec(memory_space=pl.ANY),
            scratch_shapes=[
                pltpu.VMEM((n_slots, MOE_CHUNK, D), BF16),
                pltpu.SemaphoreType.DMA((n_slots,)),
            ],
        ),
        out_shape=jax.ShapeDtypeStruct((n_rows, D), BF16),
        input_output_aliases={5: 0},
        compiler_params=_params("arbitrary"),
        name="moe_dispatch",
    )(off, cnt, hb, route_t, utri, xs_init)


def _experts_kernel(te_ref, na_ref, xs_ref, wg_ref, wu_ref, wd_ref, out_ref, acc_ref):
    i = pl.program_id(0)
    f = pl.program_id(1)
    last = pl.num_programs(1) - 1
    active = i < na_ref[0]

    @pl.when(active)
    def _():
        @pl.when(f == 0)
        def _():
            acc_ref[...] = jnp.zeros(acc_ref.shape, F32)

        acc_ref[...] += _swiglu_tile(xs_ref[...], wg_ref[0], wu_ref[0], wd_ref[0])

        @pl.when(f == last)
        def _():
            out_ref[...] = acc_ref[...].astype(out_ref.dtype)

    @pl.when(jnp.logical_not(active) & (f == last))
    def _():
        out_ref[...] = jnp.zeros(out_ref.shape, out_ref.dtype)


def _experts(tile_expert, n_active, xs, wg, wu, wd):
    P, D = xs.shape
    T = ROW_TILE
    nf, tf = _ffn_splits(wg.shape[2])
    fidx = lambda i, f, na: jnp.where(i < na[0], f, nf - 1)
    return pl.pallas_call(
        _experts_kernel,
        grid_spec=pltpu.PrefetchScalarGridSpec(
            num_scalar_prefetch=2,
            grid=(P // T, nf),
            in_specs=[
                pl.BlockSpec((T, D), lambda i, f, te, na: (i, 0)),
                pl.BlockSpec((1, D, tf), lambda i, f, te, na: (te[i], 0, fidx(i, f, na))),
                pl.BlockSpec((1, D, tf), lambda i, f, te, na: (te[i], 0, fidx(i, f, na))),
                pl.BlockSpec((1, tf, D), lambda i, f, te, na: (te[i], fidx(i, f, na), 0)),
            ],
            out_specs=pl.BlockSpec((T, D), lambda i, f, te, na: (i, 0)),
            scratch_shapes=[pltpu.VMEM((T, D), F32)],
        ),
        out_shape=jax.ShapeDtypeStruct((P, D), BF16),
        compiler_params=_params("arbitrary", "arbitrary"),
        name="moe_experts",
    )(tile_expert, n_active, xs, wg, wu, wd)


def _combine_kernel(off_ref, cnt_ref, x_ref, route_ref, ltri_ref, ys_hbm, yp_ref, ysm_ref,
                    buf_ref, acc_ref, sem):
    t = pl.program_id(0)
    nt = pl.num_programs(0) - 1
    T = x_ref.shape[0]
    S = MOE_CHUNK
    nk = T // S

    def chunk_copy(e, k):
        start = pl.multiple_of(off_ref[t * N_EXPERTS + e] + k * S, MOE_ROW_ALIGN)
        slot = e * nk + k
        return pltpu.make_async_copy(ys_hbm.at[pl.ds(start, S), :], buf_ref.at[slot], sem.at[slot])

    _for_each_chunk(cnt_ref, t, nk, lambda e, k: chunk_copy(e, k).start())

    route = route_ref[...]
    e1, e2, g1, g2 = (route[:, j:j + 1] for j in range(2 * TOP_K))
    lane = lax.broadcasted_iota(jnp.int32, route.shape, 1).astype(F32)
    member = (lane == e1) | (lane == e2)
    rank = _dot(ltri_ref[...], member.astype(BF16))
    rank = jnp.where(member, rank, 0.0)
    col = lax.broadcasted_iota(jnp.int32, (1, S), 1)
    acc_ref[...] = x_ref[...]

    def absorb(e, k):
        chunk_copy(e, k).wait()
        sel = rank[:, e:e + 1] == (col + (k * S + 1)).astype(F32)
        gate = jnp.where(e1 == e, g1, 0.0) + jnp.where(e2 == e, g2, 0.0)
        acc_ref[...] += gate * _dot(sel.astype(BF16), buf_ref[e * nk + k])

    _for_each_chunk(cnt_ref, t, nk, absorb)

    @pl.when(t < nt)
    def _():
        yp_ref[...] = acc_ref[...]

    @pl.when(t == nt)
    def _():
        ysm_ref[...] = acc_ref[...]


def _combine(off, cnt, x3, route, ltri, ys, n_sample):
    N, D = x3.shape
    T = ROW_TILE
    nt = N // T - 1
    n_slots = N_EXPERTS * (T // MOE_CHUNK)
    return pl.pallas_call(
        _combine_kernel,
        grid_spec=pltpu.PrefetchScalarGridSpec(
            num_scalar_prefetch=2,
            grid=(nt + 1,),
            in_specs=[
                pl.BlockSpec((T, D), lambda t, off, cnt: (t, 0)),
                pl.BlockSpec((T, LANE), lambda t, off, cnt: (t, 0)),
                pl.BlockSpec((T, T), lambda t, off, cnt: (0, 0)),
                pl.BlockSpec(memory_space=pl.ANY),
            ],
            out_specs=[
                pl.BlockSpec((T, D), lambda t, off, cnt: (jnp.minimum(t, nt - 1), 0)),
                pl.BlockSpec((T, D), lambda t, off, cnt: (0, 0)),
            ],
            scratch_shapes=[
                pltpu.VMEM((n_slots, MOE_CHUNK, D), BF16),
                pltpu.VMEM((T, D), F32),
                pltpu.SemaphoreType.DMA((n_slots,)),
            ],
        ),
        out_shape=[
            jax.ShapeDtypeStruct((N - n_sample, D), F32),
            jax.ShapeDtypeStruct((n_sample, D), F32),
        ],
        compiler_params=_params("arbitrary"),
        name="moe_combine",
    )(off, cnt, x3, route, ltri, ys)


def _routing_tables(route, tile):
    n = route.shape[0]
    n_tok_tiles = n // tile
    experts = route[:, :TOP_K].astype(jnp.int32)
    onehot = (experts[:, :, None] == jnp.arange(N_EXPERTS)[None, None, :]).astype(jnp.int32).sum(axis=1)
    cnt = onehot.reshape(n_tok_tiles, tile, N_EXPERTS).sum(axis=1)
    span = (cnt + MOE_ROW_ALIGN - 1) // MOE_ROW_ALIGN * MOE_ROW_ALIGN
    totals = span.sum(axis=0)
    tiles_per = (totals + MOE_CHUNK + tile - 1) // tile
    tile_end = jnp.cumsum(tiles_per)
    group_off = (tile_end - tiles_per) * tile
    off = group_off[None, :] + jnp.cumsum(span, axis=0) - span
    max_rows = n * TOP_K + N_EXPERTS * (MOE_CHUNK + n_tok_tiles * (MOE_ROW_ALIGN - 1))
    n_row_tiles = max_rows // tile + N_EXPERTS
    n_active = tile_end[-1:].astype(jnp.int32)
    tile_ids = jnp.minimum(jnp.arange(n_row_tiles, dtype=jnp.int32), n_active[0] - 1)
    tile_expert = jnp.sum((tile_ids[:, None] >= tile_end[None, :]).astype(jnp.int32), axis=1)
    return (off.reshape(-1).astype(jnp.int32), cnt.reshape(-1).astype(jnp.int32),
            tile_expert, n_active, n_row_tiles * tile)


def _group_matrices(d_model):
    n_groups = d_model // HEAD_DIM
    gsum = np.zeros((d_model, LANE), np.float32)
    gsum[np.arange(d_model), np.arange(d_model) // HEAD_DIM] = 1.0
    assert n_groups <= LANE
    return jnp.asarray(gsum, BF16), jnp.asarray(gsum.T.copy(), BF16)


def kernel(x_prompt, x_sample, cache_k, cache_v, state_pool, g_pool_norm, w_pool, pool_scale, g_attn, w_q, g_qn, lambda_q1, lambda_k1, lambda_q2, lambda_k2, g_subln, w_o, g_kv, w_k, w_v, g_kn, rel_bias, g_ffn, w_gate_dense, w_up_dense, w_down_dense, w_router, w_gate_moe, w_up_moe, w_down_moe):
    Bp, Lp, D = x_prompt.shape
    Bs, Ls, _ = x_sample.shape
    past = cache_k.shape[1]
    n_heads = D // V_DIM
    n_sample = Bs * Ls
    assert Bp == 1 and n_sample == ROW_TILE and Lp % ROW_TILE == 0
    assert g_pool_norm.shape[0] == 1 and g_attn.shape[0] == 1
    bf = lambda a: a.astype(BF16)
    row = lambda a: a.reshape(1, -1)

    x1, pool_p, pool_s = _pool_layer(
        x_prompt.reshape(Lp, D), x_sample.reshape(n_sample, D), state_pool[0],
        row(g_pool_norm[0]), bf(w_pool[0]), row(pool_scale[0]), past)
    x2 = _dense_ffn(x1, row(g_ffn[0]), bf(w_gate_dense[0]), bf(w_up_dense[0]), bf(w_down_dense[0]))

    layer = 1
    lambda_init = 0.8 - 0.6 * math.exp(-0.3 * layer)
    lam = (jnp.exp(jnp.sum(lambda_q1[0] * lambda_k1[0])) - jnp.exp(jnp.sum(lambda_q2[0] * lambda_k2[0]))
           + lambda_init).reshape(1)
    gsum, gbc = _group_matrices(D)
    n_groups = D // HEAD_DIM
    k_p, v_p, k_s, v_s, qh, kh, vh, q_s = _qkv_proj(
        x2, row(g_kv), row(g_attn[0]), bf(w_k), bf(w_v), bf(w_q[0]),
        row(jnp.tile(g_kn, n_groups)), row(jnp.tile(g_qn[0], n_groups) * (HEAD_DIM ** -0.5 * LOG2E)),
        gsum, gbc, n_sample)
    out_scale = 1.0 - lambda_init
    g_sub = row(g_subln[0])
    score_bound = LOG2E * (math.sqrt(HEAD_DIM) * jnp.max(jnp.abs(g_qn[0])) * jnp.max(jnp.abs(g_kn))
                           + jnp.max(jnp.abs(rel_bias)))
    o_p = lax.cond(
        score_bound <= UNSHIFTED_SCORE_LIMIT,
        functools.partial(_attn_prompt, out_scale=out_scale, online=False),
        functools.partial(_attn_prompt, out_scale=out_scale, online=True),
        rel_bias, lam, qh, kh, vh, g_sub)

    q4 = q_s.reshape(Bs, Ls, n_groups, HEAD_DIM).transpose(0, 2, 1, 3)
    eye = jnp.eye(n_groups, dtype=BF16)
    qbd = (q4[:, :, :, None, :] * eye[None, :, None, :, None]).reshape(Bs, n_groups * Ls, D)
    tab = jnp.repeat(rel_bias.T, 2 * Ls, axis=0) * LOG2E
    o_s = _attn_sample(lam, qbd, cache_k, cache_v, k_s, v_s, tab, g_sub, out_scale)

    wr = jnp.pad(w_router[0], ((0, 0), (0, LANE - N_EXPERTS)))
    wr_hi = bf(wr)
    wr_lo = bf(wr - wr_hi.astype(F32))
    x3, h_moe, route, route_t = _oproj_router(o_p, o_s, x2, bf(w_o[0]), row(g_ffn[1]), wr_hi, wr_lo)
    off, cnt, tile_expert, n_active, n_rows = _routing_tables(route, ROW_TILE)
    ltri = jnp.asarray(np.tril(np.ones((ROW_TILE, ROW_TILE), np.float32)), BF16)
    xs = _dispatch(off, cnt, h_moe, route_t, ltri.T, n_rows)
    ys = _experts(tile_expert, n_active, xs, bf(w_gate_moe[0]), bf(w_up_moe[0]), bf(w_down_moe[0]))
    y_p, y_s = _combine(off, cnt, x3, route, ltri, ys, n_sample)

    return (y_p.reshape(Bp, Lp, D), y_s.reshape(Bs, Ls, D),
            k_p.reshape(Bp, Lp, n_heads, 2, HEAD_DIM), v_p.reshape(Bp, Lp, n_heads, V_DIM),
            pool_p.reshape(1, Bp, POOL_STATE, D),
            k_s.reshape(Bs, Ls, n_heads, 2, HEAD_DIM), v_s.reshape(Bs, Ls, n_heads, V_DIM),
            pool_s.reshape(1, Bs, POOL_STATE, D))
```

```python
import functools
import math

import numpy as np
import jax
import jax.numpy as jnp
from jax import lax
from jax.experimental import pallas as pl
from jax.experimental.pallas import tpu as pltpu

EPS = 1e-6
CHUNK = 64
POOL_WINDOWS = (2, 4, 8, 16)
POOL_STATE = max(POOL_WINDOWS) - 1
HEAD_DIM = 64
V_DIM = 2 * HEAD_DIM
N_EXPERTS = 8
TOP_K = 2
MAX_EXACT = 8
BUCKET_UPPER = (1, 2, 3, 4, 5, 6, 7, 8, 12, 16, 23, 32, 46, 64, 91)
FAR_DISTANCE = 128
NEG = -1e30
LOG2E = math.log2(math.e)
UNSHIFTED_SCORE_LIMIT = 80.0

ROW_TILE = 512
FAR_TILES_PER_TRIP = 4
LANE = 128
VMEM_LIMIT = 56 * 1024 * 1024

F32 = jnp.float32
BF16 = jnp.bfloat16


def _dot(a, b):
    return jnp.dot(a, b, preferred_element_type=F32)


def _dot_nt(a, b):
    return lax.dot_general(a, b, (((1,), (1,)), ((), ())), preferred_element_type=F32)


def _rms_unit(x):
    return x * lax.rsqrt(jnp.mean(x * x, axis=-1, keepdims=True) + EPS)


def _rel_bias_tile(rel, tab):
    n = jnp.abs(rel)
    neg = tab(15)
    pos = tab(31)
    for b in range(14, -1, -1):
        lt = n < BUCKET_UPPER[b]
        neg = jnp.where(lt, tab(b), neg)
        pos = jnp.where(lt, tab(16 + b), pos)
    return jnp.where(rel > 0, pos, neg)


def _params(*sem):
    return pltpu.CompilerParams(dimension_semantics=sem, vmem_limit_bytes=VMEM_LIMIT)


def _pool_kernel(xp_ref, xs_ref, st_ref, g_ref, w_ref, sc_ref,
                 x1_ref, pp_ref, ps_ref, ext_ref, ext3_ref, *, past_len):
    i = pl.program_id(0)
    nt = pl.num_programs(0) - 1
    T, D = xp_ref.shape
    gw = D // len(POOL_WINDOWS)

    @pl.when(i < nt)
    def _prompt():
        x = xp_ref[...]
        h = _rms_unit(x) * g_ref[...]

        @pl.when(i == 0)
        def _():
            ext_ref[0:16, :] = jnp.zeros((16, D), F32)

        ext_ref[16:16 + T, :] = h
        row = i * T + lax.broadcasted_iota(jnp.int32, (T, 1), 0)
        parts = []
        for gi, w in enumerate(POOL_WINDOWS):
            c0 = gi * gw
            s = ext_ref[16:16 + T, c0:c0 + gw]
            for j in range(1, w):
                s = s + ext_ref[16 - j:16 - j + T, c0:c0 + gw]
            cnt = jnp.minimum(w, row + 1).astype(F32)
            pooled = s / cnt - h[:, c0:c0 + gw]
            parts.append(_dot(pooled.astype(BF16), w_ref[gi]))
        mix = jnp.concatenate(parts, axis=-1) * sc_ref[...]
        x1_ref[...] = x + mix
        tail = ext_ref[T:T + 16, :]
        ext_ref[0:16, :] = tail

        @pl.when(i == nt - 1)
        def _():
            pp_ref[...] = tail[1:16, :]

    @pl.when(i == nt)
    def _sample():
        B = st_ref.shape[0]
        L = T // B
        x = xs_ref[...]
        h = _rms_unit(x) * g_ref[...]
        ext3_ref[:, 1:16, :] = st_ref[...]
        ext3_ref[:, 16:16 + L, :] = h.reshape(B, L, D)
        t = lax.broadcasted_iota(jnp.int32, (1, L, 1), 1)
        parts = []
        for gi, w in enumerate(POOL_WINDOWS):
            c0 = gi * gw
            s = ext3_ref[:, 16:16 + L, c0:c0 + gw]
            for j in range(1, w):
                s = s + ext3_ref[:, 16 - j:16 - j + L, c0:c0 + gw]
            cnt = jnp.minimum(w, past_len + t + 1).astype(F32)
            pooled = (s / cnt).reshape(T, gw) - h[:, c0:c0 + gw]
            parts.append(_dot(pooled.astype(BF16), w_ref[gi]))
        mix = jnp.concatenate(parts, axis=-1) * sc_ref[...]
        x1_ref[...] = x + mix
        ps_ref[...] = ext3_ref[:, 16 + L - POOL_STATE:16 + L, :]


def _pool_layer(xp, xs, state, g, w, sc, past_len):
    Lp, D = xp.shape
    T = ROW_TILE
    nt = Lp // T
    B = state.shape[0]
    L = xs.shape[0] // B
    return pl.pallas_call(
        functools.partial(_pool_kernel, past_len=past_len),
        grid=(nt + 1,),
        in_specs=[
            pl.BlockSpec((T, D), lambda i: (jnp.minimum(i, nt - 1), 0)),
            pl.BlockSpec((T, D), lambda i: (0, 0)),
            pl.BlockSpec((B, POOL_STATE, D), lambda i: (0, 0, 0)),
            pl.BlockSpec((1, D), lambda i: (0, 0)),
            pl.BlockSpec(w.shape, lambda i: (0, 0, 0)),
            pl.BlockSpec((1, D), lambda i: (0, 0)),
        ],
        out_specs=[
            pl.BlockSpec((T, D), lambda i: (i, 0)),
            pl.BlockSpec((POOL_STATE, D), lambda i: (0, 0)),
            pl.BlockSpec((B, POOL_STATE, D), lambda i: (0, 0, 0)),
        ],
        out_shape=[
            jax.ShapeDtypeStruct((Lp + T, D), F32),
            jax.ShapeDtypeStruct((POOL_STATE, D), F32),
            jax.ShapeDtypeStruct((B, POOL_STATE, D), F32),
        ],
        scratch_shapes=[
            pltpu.VMEM((16 + T, D), F32),
            pltpu.VMEM((B, 16 + L, D), F32),
        ],
        compiler_params=_params("arbitrary"),
        name="pool_mixer",
    )(xp, xs, state, g, w, sc)


def _swiglu_tile(hb, wg, wu, wd):
    gt = _dot(hb, wg)
    ut = _dot(hb, wu)
    a = gt * jax.nn.sigmoid(gt) * ut
    return _dot(a.astype(BF16), wd)


def _ffn_kernel(x_ref, g_ref, wg_ref, wu_ref, wd_ref, out_ref, hb_ref, acc_ref):
    f = pl.program_id(1)

    @pl.when(f == 0)
    def _():
        x = x_ref[...]
        hb_ref[...] = (_rms_unit(x) * g_ref[...]).astype(BF16)
        acc_ref[...] = x

    acc_ref[...] += _swiglu_tile(hb_ref[...], wg_ref[...], wu_ref[...], wd_ref[...])

    @pl.when(f == pl.num_programs(1) - 1)
    def _():
        out_ref[...] = acc_ref[...]


def _ffn_splits(d_ff):
    nf = 2 if (d_ff // 2) % LANE == 0 else 1
    return nf, d_ff // nf


def _dense_ffn(x, g, wg, wu, wd):
    N, D = x.shape
    T = ROW_TILE
    nf, tf = _ffn_splits(wg.shape[1])
    return pl.pallas_call(
        _ffn_kernel,
        grid=(N // T, nf),
        in_specs=[
            pl.BlockSpec((T, D), lambda i, f: (i, 0)),
            pl.BlockSpec((1, D), lambda i, f: (0, 0)),
            pl.BlockSpec((D, tf), lambda i, f: (0, f)),
            pl.BlockSpec((D, tf), lambda i, f: (0, f)),
            pl.BlockSpec((tf, D), lambda i, f: (f, 0)),
        ],
        out_specs=pl.BlockSpec((T, D), lambda i, f: (i, 0)),
        out_shape=jax.ShapeDtypeStruct((N, D), F32),
        scratch_shapes=[pltpu.VMEM((T, D), BF16), pltpu.VMEM((T, D), F32)],
        compiler_params=_params("arbitrary", "arbitrary"),
        name="dense_swiglu",
    )(x, g, wg, wu, wd)


def _qkv_kernel(x_ref, gkv_ref, gq_ref, wk_ref, wv_ref, wq_ref, gkn_ref, gqn_ref,
                gsum_ref, gbc_ref,
                kp_ref, vp_ref, ks_ref, vs_ref, qh_ref, kh_ref, vh_ref, qs_ref):
    i = pl.program_id(0)
    nt = pl.num_programs(0) - 1
    n_heads = qh_ref.shape[0]

    xn = _rms_unit(x_ref[...])
    hkv = (xn * gkv_ref[...]).astype(BF16)
    hq = (xn * gq_ref[...]).astype(BF16)

    def head_norm(y, g):
        ssq = _dot((y * y).astype(BF16), gsum_ref[...])
        rs = lax.rsqrt(ssq * (1.0 / HEAD_DIM) + EPS)
        rs_hi = rs.astype(BF16)
        rs_lo = (rs - rs_hi.astype(F32)).astype(BF16)
        rsb = _dot(rs_hi, gbc_ref[...]) + _dot(rs_lo, gbc_ref[...])
        return y * rsb * g

    k = head_norm(_dot(hkv, wk_ref[...]), gkn_ref[...])
    v = _dot(hkv, wv_ref[...])
    q = head_norm(_dot(hq, wq_ref[...]), gqn_ref[...])

    @pl.when(i < nt)
    def _():
        kp_ref[...] = k
        vp_ref[...] = v
        for h in range(n_heads):
            sl = slice(h * V_DIM, (h + 1) * V_DIM)
            qh_ref[h] = q[:, sl].astype(BF16)
            kh_ref[h] = k[:, sl].T.astype(BF16)
            vh_ref[h] = v[:, sl].astype(BF16)

    @pl.when(i == nt)
    def _():
        ks_ref[...] = k
        vs_ref[...] = v
        qs_ref[...] = q.astype(BF16)


def _qkv_proj(x, gkv, gq, wk, wv, wq, gkn_t, gqn_t, gsum, gbc, n_sample):
    N, D = x.shape
    T = ROW_TILE
    nt = N // T - 1
    Lp = N - n_sample
    n_heads = D // V_DIM
    const2 = lambda i: (0, 0)
    prow = lambda i: (jnp.minimum(i, nt - 1), 0)
    phead = lambda i: (0, jnp.minimum(i, nt - 1), 0)
    return pl.pallas_call(
        _qkv_kernel,
        grid=(nt + 1,),
        in_specs=[
            pl.BlockSpec((T, D), lambda i: (i, 0)),
            pl.BlockSpec((1, D), const2),
            pl.BlockSpec((1, D), const2),
            pl.BlockSpec((D, D), const2),
            pl.BlockSpec((D, D), const2),
            pl.BlockSpec((D, D), const2),
            pl.BlockSpec((1, D), const2),
            pl.BlockSpec((1, D), const2),
            pl.BlockSpec(gsum.shape, const2),
            pl.BlockSpec(gbc.shape, const2),
        ],
        out_specs=[
            pl.BlockSpec((T, D), prow),
            pl.BlockSpec((T, D), prow),
            pl.BlockSpec((T, D), const2),
            pl.BlockSpec((T, D), const2),
            pl.BlockSpec((n_heads, T, V_DIM), phead),
            pl.BlockSpec((n_heads, V_DIM, T), lambda i: (0, 0, jnp.minimum(i, nt - 1))),
            pl.BlockSpec((n_heads, T, V_DIM), phead),
            pl.BlockSpec((T, D), const2),
        ],
        out_shape=[
            jax.ShapeDtypeStruct((Lp, D), F32),
            jax.ShapeDtypeStruct((Lp, D), F32),
            jax.ShapeDtypeStruct((n_sample, D), F32),
            jax.ShapeDtypeStruct((n_sample, D), F32),
            jax.ShapeDtypeStruct((n_heads, Lp, V_DIM), BF16),
            jax.ShapeDtypeStruct((n_heads, V_DIM, Lp), BF16),
            jax.ShapeDtypeStruct((n_heads, Lp, V_DIM), BF16),
            jax.ShapeDtypeStruct((n_sample, D), BF16),
        ],
        compiler_params=_params("arbitrary"),
        name="qkv_proj",
    )(x, gkv, gq, wk, wv, wq, gkn_t, gqn_t, gsum, gbc)


def _diff_out(o0, o1, lam, g, out_scale):
    o = o0 - lam * o1
    return _rms_unit(o) * g * out_scale


def _lane_partial_sum(p):
    out = p[:, 0:LANE]
    for k in range(1, p.shape[1] // LANE):
        out = out + p[:, k * LANE:(k + 1) * LANE]
    return out


def _attn_prompt_kernel(bias_ref, lam_ref, q_ref, kt_ref, v_ref, g_ref, o_ref,
                        bd_ref, bs_ref, l_ref, acc_ref, *m_scratch, out_scale):
    h = pl.program_id(0)
    i = pl.program_id(1)
    T = q_ref.shape[1]
    online = bool(m_scratch)

    @pl.when(i == 0)
    def _():
        row = lax.broadcasted_iota(jnp.int32, (T, T), 0)
        col = lax.broadcasted_iota(jnp.int32, (T, T), 1)
        tab = lambda b: bias_ref[b, h] * LOG2E
        visible = (col // CHUNK) <= (row // CHUNK)
        bd_ref[...] = jnp.where(visible, _rel_bias_tile(col - row, tab), NEG)
        bs_ref[...] = _rel_bias_tile(col - row - T, tab)

    if online:
        m_ref, = m_scratch
        m_ref[...] = jnp.full(m_ref.shape, NEG, F32)
    l_ref[...] = jnp.zeros(l_ref.shape, F32)
    acc_ref[...] = jnp.zeros(acc_ref.shape, F32)
    q = q_ref[0]
    qc = (q[:, :HEAD_DIM], q[:, HEAD_DIM:])

    def update(j, bias, width=T):
        start = pl.multiple_of(j * T, T)
        vt = v_ref[0, pl.ds(start, width), :]
        for c in range(2):
            kt = kt_ref[0, c * HEAD_DIM:(c + 1) * HEAD_DIM, pl.ds(start, width)]
            s = _dot(qc[c], kt) + bias
            if online:
                m_old = m_ref[c]
                m_new = jnp.maximum(m_old, jnp.max(s, axis=-1, keepdims=True))
                alpha = jnp.exp2(m_old - m_new)
                p = jnp.exp2(s - m_new)
                l_ref[c] = alpha * l_ref[c] + _lane_partial_sum(p)
                acc_ref[c] = alpha * acc_ref[c] + _dot(p.astype(BF16), vt)
                m_ref[c] = m_new
            else:
                p = jnp.exp2(s)
                l_ref[c] += _lane_partial_sum(p)
                acc_ref[c] += _dot(p.astype(BF16), vt)

    far_bias = bias_ref[15, h] * LOG2E

    n_far = jnp.maximum(i - 1, 0)

    def far_body(jj, carry):
        update(FAR_TILES_PER_TRIP * jj, far_bias, width=FAR_TILES_PER_TRIP * T)
        return carry

    def far_rest(j, carry):
        update(j, far_bias)
        return carry

    n_trips = n_far // FAR_TILES_PER_TRIP
    lax.fori_loop(0, n_trips, far_body, 0)
    lax.fori_loop(n_trips * FAR_TILES_PER_TRIP, n_far, far_rest, 0)

    @pl.when(i >= 1)
    def _():
        update(i - 1, bs_ref[...])

    update(i, bd_ref[...])

    o0 = acc_ref[0] / jnp.sum(l_ref[0], axis=-1, keepdims=True)
    o1 = acc_ref[1] / jnp.sum(l_ref[1], axis=-1, keepdims=True)
    o_ref[...] = _diff_out(o0, o1, lam_ref[0], g_ref[...], out_scale).astype(o_ref.dtype)


def _attn_prompt(rel_bias, lam, qh, kth, vh, g_subln, out_scale, online):
    n_heads, Lp, _ = qh.shape
    T = ROW_TILE
    assert T % CHUNK == 0 and T >= FAR_DISTANCE
    m_scratch = [pltpu.VMEM((2, T, 1), F32)] if online else []
    return pl.pallas_call(
        functools.partial(_attn_prompt_kernel, out_scale=out_scale),
        grid=(n_heads, Lp // T),
        in_specs=[
            pl.BlockSpec(memory_space=pltpu.SMEM),
            pl.BlockSpec(memory_space=pltpu.SMEM),
            pl.BlockSpec((1, T, V_DIM), lambda h, i: (h, i, 0)),
            pl.BlockSpec((1, V_DIM, Lp), lambda h, i: (h, 0, 0)),
            pl.BlockSpec((1, Lp, V_DIM), lambda h, i: (h, 0, 0)),
            pl.BlockSpec((1, V_DIM), lambda h, i: (0, 0)),
        ],
        out_specs=pl.BlockSpec((T, V_DIM), lambda h, i: (i, h)),
        out_shape=jax.ShapeDtypeStruct((Lp, n_heads * V_DIM), BF16),
        scratch_shapes=[
            pltpu.VMEM((T, T), F32),
            pltpu.VMEM((T, T), F32),
            pltpu.VMEM((2, T, LANE), F32),
            pltpu.VMEM((2, T, V_DIM), F32),
        ] + m_scratch,
        compiler_params=_params("arbitrary", "arbitrary"),
        name="attn_prompt_online" if online else "attn_prompt",
    )(rel_bias, lam, qh, kth, vh, g_subln)


def _attn_sample_kernel(lam_ref, q_ref, ckt_ref, cv_ref, kn_ref, vn_ref, tab_ref, g_ref,
                        o_ref, nearb_ref, newb_ref, qbd_ref, m_ref, l_ref, acc_ref, *, out_scale):
    b = pl.program_id(0)
    kb = pl.program_id(1)
    nkb = pl.num_programs(1)
    R, Tk = nearb_ref.shape
    L = kn_ref.shape[0]
    D = q_ref.shape[1]
    n_heads = D // V_DIM

    @pl.when((b == 0) & (kb == 0))
    def _():
        tab = lambda bkt: tab_ref[:, bkt:bkt + 1]
        t_near = lax.broadcasted_iota(jnp.int32, (R, Tk), 0) % L
        col = lax.broadcasted_iota(jnp.int32, (R, Tk), 1)
        nearb_ref[...] = _rel_bias_tile(col - Tk - t_near, tab)
        t_new = lax.broadcasted_iota(jnp.int32, (R, L), 0) % L
        col_new = lax.broadcasted_iota(jnp.int32, (R, L), 1)
        newb_ref[...] = _rel_bias_tile(col_new - t_new, tab)

    @pl.when(kb == 0)
    def _():
        m_ref[...] = jnp.full(m_ref.shape, NEG, F32)
        l_ref[...] = jnp.zeros(l_ref.shape, F32)
        acc_ref[...] = jnp.zeros(acc_ref.shape, F32)
        q_rows = jnp.concatenate([q_ref[...]] * (R // L), axis=0)
        row_group = lax.broadcasted_iota(jnp.int32, (R, D), 0) // L
        col_group = lax.broadcasted_iota(jnp.int32, (R, D), 1) // HEAD_DIM
        qbd_ref[...] = jnp.where(row_group == col_group, q_rows, jnp.zeros_like(q_rows))

    rows_per_head = 2 * L

    def update(s, value_of_head):
        m_old = m_ref[...]
        m_new = jnp.maximum(m_old, jnp.max(s, axis=-1, keepdims=True))
        alpha = jnp.exp2(m_old - m_new)
        p = jnp.exp2(s - m_new)
        l_ref[...] = alpha * l_ref[...] + jnp.sum(p, axis=-1, keepdims=True)
        pb = p.astype(BF16)
        for h in range(n_heads):
            rs = slice(h * rows_per_head, (h + 1) * rows_per_head)
            acc_ref[rs, :] = alpha[rs, :] * acc_ref[rs, :] + _dot(pb[rs, :], value_of_head(h))
        m_ref[...] = m_new

    qbd = qbd_ref[...]
    s = _dot(qbd, ckt_ref[0].astype(BF16))
    cache_value = lambda h: cv_ref[0, pl.ds(h, Tk, stride=n_heads), :].astype(BF16)

    @pl.when(kb < nkb - 1)
    def _():
        update(s + tab_ref[:, 15:16], cache_value)

    @pl.when(kb == nkb - 1)
    def _():
        update(s + nearb_ref[...], cache_value)
        s_new = _dot_nt(qbd, kn_ref[...].astype(BF16)) + newb_ref[...]
        update(s_new, lambda h: vn_ref[:, h * V_DIM:(h + 1) * V_DIM].astype(BF16))
        lam = lam_ref[0]
        o = acc_ref[...] / l_ref[...]
        for h in range(n_heads):
            r0 = h * rows_per_head
            o_ref[:, h * V_DIM:(h + 1) * V_DIM] = _diff_out(
                o[r0:r0 + L, :], o[r0 + L:r0 + 2 * L, :], lam, g_ref[...], out_scale).astype(o_ref.dtype)


def _attn_sample(lam, q, cache_k, cache_v, k_new, v_new, tab, g_subln, out_scale):
    B, past, n_heads = cache_v.shape[:3]
    D = n_heads * V_DIM
    L = k_new.shape[0] // B
    R = (D // HEAD_DIM) * L
    Tk = min(1024, past)
    assert past % Tk == 0 and Tk >= FAR_DISTANCE + L
    cache_kt = jnp.transpose(cache_k, (0, 2, 3, 4, 1)).reshape(B, D, past)
    return pl.pallas_call(
        functools.partial(_attn_sample_kernel, out_scale=out_scale),
        grid=(B, past // Tk),
        in_specs=[
            pl.BlockSpec(memory_space=pltpu.SMEM),
            pl.BlockSpec((L, D), lambda b, k: (b, 0)),
            pl.BlockSpec((1, D, Tk), lambda b, k: (b, 0, k)),
            pl.BlockSpec((1, Tk * n_heads, V_DIM), lambda b, k: (b, k, 0)),
            pl.BlockSpec((L, D), lambda b, k: (b, 0)),
            pl.BlockSpec((L, D), lambda b, k: (b, 0)),
            pl.BlockSpec(tab.shape, lambda b, k: (0, 0)),
            pl.BlockSpec((1, V_DIM), lambda b, k: (0, 0)),
        ],
        out_specs=pl.BlockSpec((L, D), lambda b, k: (b, 0)),
        out_shape=jax.ShapeDtypeStruct((B * L, D), BF16),
        scratch_shapes=[
            pltpu.VMEM((R, Tk), F32),
            pltpu.VMEM((R, L), F32),
            pltpu.VMEM((R, D), BF16),
            pltpu.VMEM((R, 1), F32),
            pltpu.VMEM((R, 1), F32),
            pltpu.VMEM((R, V_DIM), F32),
        ],
        compiler_params=_params("arbitrary", "arbitrary"),
        name="attn_sample",
    )(lam, q, cache_kt, cache_v.reshape(B, past * n_heads, V_DIM), k_new, v_new, tab, g_subln)


def _oproj_router_kernel(op_ref, os_ref, x_ref, wo_ref, g_ref, wrh_ref, wrl_ref,
                         x3_ref, h_ref, route_ref, route_t_ref):
    i = pl.program_id(0)
    nt = pl.num_programs(0) - 1
    o = jnp.where(i == nt, os_ref[...], op_ref[...])
    x3 = x_ref[...] + _dot(o, wo_ref[...])
    x3_ref[...] = x3
    h = _rms_unit(x3) * g_ref[...]
    h_hi = h.astype(BF16)
    h_ref[...] = h_hi
    h_lo = (h - h_hi.astype(F32)).astype(BF16)
    logits = _dot(h_hi, wrh_ref[...]) + (_dot(h_lo, wrh_ref[...]) + _dot(h_hi, wrl_ref[...]))
    lane = lax.broadcasted_iota(jnp.int32, logits.shape, 1)
    lg = jnp.where(lane < N_EXPERTS, logits, -jnp.inf)
    m1 = jnp.max(lg, axis=-1, keepdims=True)
    i1 = jnp.min(jnp.where(lg == m1, lane, LANE), axis=-1, keepdims=True)
    lg2 = jnp.where(lane == i1, -jnp.inf, lg)
    m2 = jnp.max(lg2, axis=-1, keepdims=True)
    i2 = jnp.min(jnp.where(lg2 == m2, lane, LANE), axis=-1, keepdims=True)
    e2 = jnp.exp(m2 - m1)
    den = 1.0 + e2
    g1 = 1.0 / den
    g2 = e2 / den
    route = jnp.where(lane == 0, i1.astype(F32),
                      jnp.where(lane == 1, i2.astype(F32),
                                jnp.where(lane == 2, g1,
                                          jnp.where(lane == 3, g2, 0.0))))
    route_ref[...] = route
    route_t_ref[...] = route.T


def _oproj_router(o_p, o_s, x, wo, g, wr_hi, wr_lo):
    N, D = x.shape
    T = ROW_TILE
    nt = N // T - 1
    const2 = lambda i: (0, 0)
    row = lambda i: (i, 0)
    return pl.pallas_call(
        _oproj_router_kernel,
        grid=(nt + 1,),
        in_specs=[
            pl.BlockSpec((T, D), lambda i: (jnp.minimum(i, nt - 1), 0)),
            pl.BlockSpec((T, D), const2),
            pl.BlockSpec((T, D), row),
            pl.BlockSpec((D, D), const2),
            pl.BlockSpec((1, D), const2),
            pl.BlockSpec((D, LANE), const2),
            pl.BlockSpec((D, LANE), const2),
        ],
        out_specs=[
            pl.BlockSpec((T, D), row),
            pl.BlockSpec((T, D), row),
            pl.BlockSpec((T, LANE), row),
            pl.BlockSpec((LANE, T), lambda i: (0, i)),
        ],
        out_shape=[
            jax.ShapeDtypeStruct((N, D), F32),
            jax.ShapeDtypeStruct((N, D), BF16),
            jax.ShapeDtypeStruct((N, LANE), F32),
            jax.ShapeDtypeStruct((LANE, N), F32),
        ],
        compiler_params=_params("arbitrary"),
        name="oproj_router",
    )(o_p, o_s, x, wo, g, wr_hi, wr_lo)


MOE_CHUNK = 128
MOE_ROW_ALIGN = 16


def _for_each_chunk(cnt_ref, step, n_chunks, fn):
    for e in range(N_EXPERTS):
        for k in range(n_chunks):
            @pl.when(cnt_ref[step * N_EXPERTS + e] > k * MOE_CHUNK)
            def _(e=e, k=k):
                fn(e, k)


def _dispatch_kernel(off_ref, cnt_ref, hb_ref, rt_ref, utri_ref, xs_in_ref, xs_ref, stage_ref, sem):
    del xs_in_ref
    t = pl.program_id(0)
    T = hb_ref.shape[0]
    S = MOE_CHUNK
    nk = T // S

    def chunk_copy(step, e, k):
        start = pl.multiple_of(off_ref[step * N_EXPERTS + e] + k * S, MOE_ROW_ALIGN)
        slot = e * nk + k
        return pltpu.make_async_copy(stage_ref.at[slot], xs_ref.at[pl.ds(start, S), :], sem.at[slot])

    @pl.when(t > 0)
    def _():
        _for_each_chunk(cnt_ref, t - 1, nk, lambda e, k: chunk_copy(t - 1, e, k).wait())

    rt = rt_ref[...]
    expert_id = lax.broadcasted_iota(jnp.int32, (N_EXPERTS, T), 0).astype(F32)
    member = (rt[0:1, :] == expert_id) | (rt[1:2, :] == expert_id)
    rank = _dot(member.astype(BF16), utri_ref[...])
    rank = jnp.where(member, rank, 0.0)
    hb = hb_ref[...]
    row = lax.broadcasted_iota(jnp.int32, (S, 1), 0)

    def emit(e, k):
        sel = rank[e:e + 1, :] == (row + (k * S + 1)).astype(F32)
        stage_ref[e * nk + k] = _dot(sel.astype(BF16), hb).astype(BF16)
        chunk_copy(t, e, k).start()

    _for_each_chunk(cnt_ref, t, nk, emit)

    @pl.when(t == pl.num_programs(0) - 1)
    def _():
        _for_each_chunk(cnt_ref, t, nk, lambda e, k: chunk_copy(t, e, k).wait())


def _dispatch(off, cnt, hb, route_t, utri, n_rows):
    N, D = hb.shape
    T = ROW_TILE
    nk = T // MOE_CHUNK
    n_slots = N_EXPERTS * nk
    xs_init = jnp.zeros((n_rows, D), BF16)
    return pl.pallas_call(
        _dispatch_kernel,
        grid_spec=pltpu.PrefetchScalarGridSpec(
            num_scalar_prefetch=2,
            grid=(N // T,),
            in_specs=[
                pl.BlockSpec((T, D), lambda t, off, cnt: (t, 0)),
                pl.BlockSpec((LANE, T), lambda t, off, cnt: (0, t)),
                pl.BlockSpec((T, T), lambda t, off, cnt: (0, 0)),
                pl.BlockSpec(memory_space=pl.ANY),
            ],
            out_specs=pl.BlockSpec(memory_space=pl.ANY),
            scratch_shapes=[
                pltpu.VMEM((n_slots, MOE_CHUNK, D), BF16),
                pltpu.SemaphoreType.DMA((n_slots,)),
            ],
        ),
        out_shape=jax.ShapeDtypeStruct((n_rows, D), BF16),
        input_output_aliases={5: 0},
        compiler_params=_params("arbitrary"),
        name="moe_dispatch",
    )(off, cnt, hb, route_t, utri, xs_init)


def _experts_kernel(te_ref, na_ref, xs_ref, wg_ref, wu_ref, wd_ref, out_ref, acc_ref):
    i = pl.program_id(0)
    f = pl.program_id(1)
    last = pl.num_programs(1) - 1
    active = i < na_ref[0]

    @pl.when(active)
    def _():
        @pl.when(f == 0)
        def _():
            acc_ref[...] = jnp.zeros(acc_ref.shape, F32)

        acc_ref[...] += _swiglu_tile(xs_ref[...], wg_ref[0], wu_ref[0], wd_ref[0])

        @pl.when(f == last)
        def _():
            out_ref[...] = acc_ref[...].astype(out_ref.dtype)

    @pl.when(jnp.logical_not(active) & (f == last))
    def _():
        out_ref[...] = jnp.zeros(out_ref.shape, out_ref.dtype)


def _experts(tile_expert, n_active, xs, wg, wu, wd):
    P, D = xs.shape
    T = ROW_TILE
    nf, tf = _ffn_splits(wg.shape[2])
    fidx = lambda i, f, na: jnp.where(i < na[0], f, nf - 1)
    return pl.pallas_call(
        _experts_kernel,
        grid_spec=pltpu.PrefetchScalarGridSpec(
            num_scalar_prefetch=2,
            grid=(P // T, nf),
            in_specs=[
                pl.BlockSpec((T, D), lambda i, f, te, na: (i, 0)),
                pl.BlockSpec((1, D, tf), lambda i, f, te, na: (te[i], 0, fidx(i, f, na))),
                pl.BlockSpec((1, D, tf), lambda i, f, te, na: (te[i], 0, fidx(i, f, na))),
                pl.BlockSpec((1, tf, D), lambda i, f, te, na: (te[i], fidx(i, f, na), 0)),
            ],
            out_specs=pl.BlockSpec((T, D), lambda i, f, te, na: (i, 0)),
            scratch_shapes=[pltpu.VMEM((T, D), F32)],
        ),
        out_shape=jax.ShapeDtypeStruct((P, D), BF16),
        compiler_params=_params("arbitrary", "arbitrary"),
        name="moe_experts",
    )(tile_expert, n_active, xs, wg, wu, wd)


def _combine_kernel(off_ref, cnt_ref, x_ref, route_ref, ltri_ref, ys_hbm, yp_ref, ysm_ref,
                    buf_ref, acc_ref, sem):
    t = pl.program_id(0)
    nt = pl.num_programs(0) - 1
    T = x_ref.shape[0]
    S = MOE_CHUNK
    nk = T // S

    def chunk_copy(e, k):
        start = pl.multiple_of(off_ref[t * N_EXPERTS + e] + k * S, MOE_ROW_ALIGN)
        slot = e * nk + k
        return pltpu.make_async_copy(ys_hbm.at[pl.ds(start, S), :], buf_ref.at[slot], sem.at[slot])

    _for_each_chunk(cnt_ref, t, nk, lambda e, k: chunk_copy(e, k).start())

    route = route_ref[...]
    e1, e2, g1, g2 = (route[:, j:j + 1] for j in range(2 * TOP_K))
    lane = lax.broadcasted_iota(jnp.int32, route.shape, 1).astype(F32)
    member = (lane == e1) | (lane == e2)
    rank = _dot(ltri_ref[...], member.astype(BF16))
    rank = jnp.where(member, rank, 0.0)
    col = lax.broadcasted_iota(jnp.int32, (1, S), 1)
    acc_ref[...] = x_ref[...]

    def absorb(e, k):
        chunk_copy(e, k).wait()
        sel = rank[:, e:e + 1] == (col + (k * S + 1)).astype(F32)
        gate = jnp.where(e1 == e, g1, 0.0) + jnp.where(e2 == e, g2, 0.0)
        acc_ref[...] += gate * _dot(sel.astype(BF16), buf_ref[e * nk + k])

    _for_each_chunk(cnt_ref, t, nk, absorb)

    @pl.when(t < nt)
    def _():
        yp_ref[...] = acc_ref[...]

    @pl.when(t == nt)
    def _():
        ysm_ref[...] = acc_ref[...]


def _combine(off, cnt, x3, route, ltri, ys, n_sample):
    N, D = x3.shape
    T = ROW_TILE
    nt = N // T - 1
    n_slots = N_EXPERTS * (T // MOE_CHUNK)
    return pl.pallas_call(
        _combine_kernel,
        grid_spec=pltpu.PrefetchScalarGridSpec(
            num_scalar_prefetch=2,
            grid=(nt + 1,),
            in_specs=[
                pl.BlockSpec((T, D), lambda t, off, cnt: (t, 0)),
                pl.BlockSpec((T, LANE), lambda t, off, cnt: (t, 0)),
                pl.BlockSpec((T, T), lambda t, off, cnt: (0, 0)),
                pl.BlockSpec(memory_space=pl.ANY),
            ],
            out_specs=[
                pl.BlockSpec((T, D), lambda t, off, cnt: (jnp.minimum(t, nt - 1), 0)),
                pl.BlockSpec((T, D), lambda t, off, cnt: (0, 0)),
            ],
            scratch_shapes=[
                pltpu.VMEM((n_slots, MOE_CHUNK, D), BF16),
                pltpu.VMEM((T, D), F32),
                pltpu.SemaphoreType.DMA((n_slots,)),
            ],
        ),
        out_shape=[
            jax.ShapeDtypeStruct((N - n_sample, D), F32),
            jax.ShapeDtypeStruct((n_sample, D), F32),
        ],
        compiler_params=_params("arbitrary"),
        name="moe_combine",
    )(off, cnt, x3, route, ltri, ys)


def _routing_tables(route, tile):
    n = route.shape[0]
    n_tok_tiles = n // tile
    experts = route[:, :TOP_K].astype(jnp.int32)
    onehot = (experts[:, :, None] == jnp.arange(N_EXPERTS)[None, None, :]).astype(jnp.int32).sum(axis=1)
    cnt = onehot.reshape(n_tok_tiles, tile, N_EXPERTS).sum(axis=1)
    span = (cnt + MOE_ROW_ALIGN - 1) // MOE_ROW_ALIGN * MOE_ROW_ALIGN
    totals = span.sum(axis=0)
    tiles_per = (totals + MOE_CHUNK + tile - 1) // tile
    tile_end = jnp.cumsum(tiles_per)
    group_off = (tile_end - tiles_per) * tile
    off = group_off[None, :] + jnp.cumsum(span, axis=0) - span
    max_rows = n * TOP_K + N_EXPERTS * (MOE_CHUNK + n_tok_tiles * (MOE_ROW_ALIGN - 1))
    n_row_tiles = max_rows // tile + N_EXPERTS
    n_active = tile_end[-1:].astype(jnp.int32)
    tile_ids = jnp.minimum(jnp.arange(n_row_tiles, dtype=jnp.int32), n_active[0] - 1)
    tile_expert = jnp.sum((tile_ids[:, None] >= tile_end[None, :]).astype(jnp.int32), axis=1)
    return (off.reshape(-1).astype(jnp.int32), cnt.reshape(-1).astype(jnp.int32),
            tile_expert, n_active, n_row_tiles * tile)


def _group_matrices(d_model):
    n_groups = d_model // HEAD_DIM
    gsum = np.zeros((d_model, LANE), np.float32)
    gsum[np.arange(d_model), np.arange(d_model) // HEAD_DIM] = 1.0
    assert n_groups <= LANE
    return jnp.asarray(gsum, BF16), jnp.asarray(gsum.T.copy(), BF16)


def kernel(x_prompt, x_sample, cache_k, cache_v, state_pool, g_pool_norm, w_pool, pool_scale, g_attn, w_q, g_qn, lambda_q1, lambda_k1, lambda_q2, lambda_k2, g_subln, w_o, g_kv, w_k, w_v, g_kn, rel_bias, g_ffn, w_gate_dense, w_up_dense, w_down_dense, w_router, w_gate_moe, w_up_moe, w_down_moe):
    Bp, Lp, D = x_prompt.shape
    Bs, Ls, _ = x_sample.shape
    past = cache_k.shape[1]
    n_heads = D // V_DIM
    n_sample = Bs * Ls
    assert Bp == 1 and n_sample == ROW_TILE and Lp % ROW_TILE == 0
    assert g_pool_norm.shape[0] == 1 and g_attn.shape[0] == 1
    bf = lambda a: a.astype(BF16)
    row = lambda a: a.reshape(1, -1)

    x1, pool_p, pool_s = _pool_layer(
        x_prompt.reshape(Lp, D), x_sample.reshape(n_sample, D), state_pool[0],
        row(g_pool_norm[0]), bf(w_pool[0]), row(pool_scale[0]), past)
    x2 = _dense_ffn(x1, row(g_ffn[0]), bf(w_gate_dense[0]), bf(w_up_dense[0]), bf(w_down_dense[0]))

    layer = 1
    lambda_init = 0.8 - 0.6 * math.exp(-0.3 * layer)
    lam = (jnp.exp(jnp.sum(lambda_q1[0] * lambda_k1[0])) - jnp.exp(jnp.sum(lambda_q2[0] * lambda_k2[0]))
           + lambda_init).reshape(1)
    gsum, gbc = _group_matrices(D)
    n_groups = D // HEAD_DIM
    k_p, v_p, k_s, v_s, qh, kh, vh, q_s = _qkv_proj(
        x2, row(g_kv), row(g_attn[0]), bf(w_k), bf(w_v), bf(w_q[0]),
        row(jnp.tile(g_kn, n_groups)), row(jnp.tile(g_qn[0], n_groups) * (HEAD_DIM ** -0.5 * LOG2E)),
        gsum, gbc, n_sample)
    out_scale = 1.0 - lambda_init
    g_sub = row(g_subln[0])
    score_bound = LOG2E * (math.sqrt(HEAD_DIM) * jnp.max(jnp.abs(g_qn[0])) * jnp.max(jnp.abs(g_kn))
                           + jnp.max(jnp.abs(rel_bias)))
    o_p = lax.cond(
        score_bound <= UNSHIFTED_SCORE_LIMIT,
        functools.partial(_attn_prompt, out_scale=out_scale, online=False),
        functools.partial(_attn_prompt, out_scale=out_scale, online=True),
        rel_bias, lam, qh, kh, vh, g_sub)

    tab = jnp.repeat(rel_bias.T, 2 * Ls, axis=0) * LOG2E
    o_s = _attn_sample(lam, q_s, cache_k, cache_v, k_s, v_s, tab, g_sub, out_scale)

    wr = jnp.pad(w_router[0], ((0, 0), (0, LANE - N_EXPERTS)))
    wr_hi = bf(wr)
    wr_lo = bf(wr - wr_hi.astype(F32))
    x3, h_moe, route, route_t = _oproj_router(o_p, o_s, x2, bf(w_o[0]), row(g_ffn[1]), wr_hi, wr_lo)
    off, cnt, tile_expert, n_active, n_rows = _routing_tables(route, ROW_TILE)
    ltri = jnp.asarray(np.tril(np.ones((ROW_TILE, ROW_TILE), np.float32)), BF16)
    xs = _dispatch(off, cnt, h_moe, route_t, ltri.T, n_rows)
    ys = _experts(tile_expert, n_active, xs, bf(w_gate_moe[0]), bf(w_up_moe[0]), bf(w_down_moe[0]))
    y_p, y_s = _combine(off, cnt, x3, route, ltri, ys, n_sample)

    return (y_p.reshape(Bp, Lp, D), y_s.reshape(Bs, Ls, D),
            k_p.reshape(Bp, Lp, n_heads, 2, HEAD_DIM), v_p.reshape(Bp, Lp, n_heads, V_DIM),
            pool_p.reshape(1, Bp, POOL_STATE, D),
            k_s.reshape(Bs, Ls, n_heads, 2, HEAD_DIM), v_s.reshape(Bs, Ls, n_heads, V_DIM),
            pool_s.reshape(1, Bs, POOL_STATE, D))
```

```python
import functools
import math

import numpy as np
import jax
import jax.numpy as jnp
from jax import lax
from jax.experimental import pallas as pl
from jax.experimental.pallas import tpu as pltpu

EPS = 1e-6
CHUNK = 64
POOL_WINDOWS = (2, 4, 8, 16)
POOL_STATE = max(POOL_WINDOWS) - 1
HEAD_DIM = 64
V_DIM = 2 * HEAD_DIM
N_EXPERTS = 8
TOP_K = 2
MAX_EXACT = 8
BUCKET_UPPER = (1, 2, 3, 4, 5, 6, 7, 8, 12, 16, 23, 32, 46, 64, 91)
FAR_DISTANCE = 128
NEG = -1e30
LOG2E = math.log2(math.e)
UNSHIFTED_SCORE_LIMIT = 80.0

ROW_TILE = 512
FAR_TILES_PER_TRIP = 4
LANE = 128
VMEM_LIMIT = 56 * 1024 * 1024

F32 = jnp.float32
BF16 = jnp.bfloat16


def _dot(a, b):
    return jnp.dot(a, b, preferred_element_type=F32)


def _dot_nt(a, b):
    return lax.dot_general(a, b, (((1,), (1,)), ((), ())), preferred_element_type=F32)


def _rms_unit(x):
    return x * lax.rsqrt(jnp.mean(x * x, axis=-1, keepdims=True) + EPS)


def _rel_bias_tile(rel, tab):
    n = jnp.abs(rel)
    neg = tab(15)
    pos = tab(31)
    for b in range(14, -1, -1):
        lt = n < BUCKET_UPPER[b]
        neg = jnp.where(lt, tab(b), neg)
        pos = jnp.where(lt, tab(16 + b), pos)
    return jnp.where(rel > 0, pos, neg)


def _params(*sem):
    return pltpu.CompilerParams(dimension_semantics=sem, vmem_limit_bytes=VMEM_LIMIT)


def _pool_kernel(xp_ref, xs_ref, st_ref, g_ref, w_ref, sc_ref,
                 x1_ref, pp_ref, ps_ref, ext_ref, ext3_ref, *, past_len):
    i = pl.program_id(0)
    nt = pl.num_programs(0) - 1
    T, D = xp_ref.shape
    gw = D // len(POOL_WINDOWS)

    @pl.when(i < nt)
    def _prompt():
        x = xp_ref[...]
        h = _rms_unit(x) * g_ref[...]

        @pl.when(i == 0)
        def _():
            ext_ref[0:16, :] = jnp.zeros((16, D), F32)

        ext_ref[16:16 + T, :] = h
        row = i * T + lax.broadcasted_iota(jnp.int32, (T, 1), 0)
        parts = []
        for gi, w in enumerate(POOL_WINDOWS):
            c0 = gi * gw
            s = ext_ref[16:16 + T, c0:c0 + gw]
            for j in range(1, w):
                s = s + ext_ref[16 - j:16 - j + T, c0:c0 + gw]
            cnt = jnp.minimum(w, row + 1).astype(F32)
            pooled = s / cnt - h[:, c0:c0 + gw]
            parts.append(_dot(pooled.astype(BF16), w_ref[gi]))
        mix = jnp.concatenate(parts, axis=-1) * sc_ref[...]
        x1_ref[...] = x + mix
        tail = ext_ref[T:T + 16, :]
        ext_ref[0:16, :] = tail

        @pl.when(i == nt - 1)
        def _():
            pp_ref[...] = tail[1:16, :]

    @pl.when(i == nt)
    def _sample():
        B = st_ref.shape[0]
        L = T // B
        x = xs_ref[...]
        h = _rms_unit(x) * g_ref[...]
        ext3_ref[:, 1:16, :] = st_ref[...]
        ext3_ref[:, 16:16 + L, :] = h.reshape(B, L, D)
        t = lax.broadcasted_iota(jnp.int32, (1, L, 1), 1)
        parts = []
        for gi, w in enumerate(POOL_WINDOWS):
            c0 = gi * gw
            s = ext3_ref[:, 16:16 + L, c0:c0 + gw]
            for j in range(1, w):
                s = s + ext3_ref[:, 16 - j:16 - j + L, c0:c0 + gw]
            cnt = jnp.minimum(w, past_len + t + 1).astype(F32)
            pooled = (s / cnt).reshape(T, gw) - h[:, c0:c0 + gw]
            parts.append(_dot(pooled.astype(BF16), w_ref[gi]))
        mix = jnp.concatenate(parts, axis=-1) * sc_ref[...]
        x1_ref[...] = x + mix
        ps_ref[...] = ext3_ref[:, 16 + L - POOL_STATE:16 + L, :]


def _pool_layer(xp, xs, state, g, w, sc, past_len):
    Lp, D = xp.shape
    T = ROW_TILE
    nt = Lp // T
    B = state.shape[0]
    L = xs.shape[0] // B
    return pl.pallas_call(
        functools.partial(_pool_kernel, past_len=past_len),
        grid=(nt + 1,),
        in_specs=[
            pl.BlockSpec((T, D), lambda i: (jnp.minimum(i, nt - 1), 0)),
            pl.BlockSpec((T, D), lambda i: (0, 0)),
            pl.BlockSpec((B, POOL_STATE, D), lambda i: (0, 0, 0)),
            pl.BlockSpec((1, D), lambda i: (0, 0)),
            pl.BlockSpec(w.shape, lambda i: (0, 0, 0)),
            pl.BlockSpec((1, D), lambda i: (0, 0)),
        ],
        out_specs=[
            pl.BlockSpec((T, D), lambda i: (i, 0)),
            pl.BlockSpec((POOL_STATE, D), lambda i: (0, 0)),
            pl.BlockSpec((B, POOL_STATE, D), lambda i: (0, 0, 0)),
        ],
        out_shape=[
            jax.ShapeDtypeStruct((Lp + T, D), F32),
            jax.ShapeDtypeStruct((POOL_STATE, D), F32),
            jax.ShapeDtypeStruct((B, POOL_STATE, D), F32),
        ],
        scratch_shapes=[
            pltpu.VMEM((16 + T, D), F32),
            pltpu.VMEM((B, 16 + L, D), F32),
        ],
        compiler_params=_params("arbitrary"),
        name="pool_mixer",
    )(xp, xs, state, g, w, sc)


def _swiglu_tile(hb, wg, wu, wd):
    gt = _dot(hb, wg)
    ut = _dot(hb, wu)
    a = gt * jax.nn.sigmoid(gt) * ut
    return _dot(a.astype(BF16), wd)


def _ffn_kernel(x_ref, g_ref, wg_ref, wu_ref, wd_ref, out_ref, hb_ref, acc_ref):
    f = pl.program_id(1)

    @pl.when(f == 0)
    def _():
        x = x_ref[...]
        hb_ref[...] = (_rms_unit(x) * g_ref[...]).astype(BF16)
        acc_ref[...] = x

    acc_ref[...] += _swiglu_tile(hb_ref[...], wg_ref[...], wu_ref[...], wd_ref[...])

    @pl.when(f == pl.num_programs(1) - 1)
    def _():
        out_ref[...] = acc_ref[...]


def _ffn_splits(d_ff):
    nf = 2 if (d_ff // 2) % LANE == 0 else 1
    return nf, d_ff // nf


def _dense_ffn(x, g, wg, wu, wd):
    N, D = x.shape
    T = ROW_TILE
    nf, tf = _ffn_splits(wg.shape[1])
    return pl.pallas_call(
        _ffn_kernel,
        grid=(N // T, nf),
        in_specs=[
            pl.BlockSpec((T, D), lambda i, f: (i, 0)),
            pl.BlockSpec((1, D), lambda i, f: (0, 0)),
            pl.BlockSpec((D, tf), lambda i, f: (0, f)),
            pl.BlockSpec((D, tf), lambda i, f: (0, f)),
            pl.BlockSpec((tf, D), lambda i, f: (f, 0)),
        ],
        out_specs=pl.BlockSpec((T, D), lambda i, f: (i, 0)),
        out_shape=jax.ShapeDtypeStruct((N, D), F32),
        scratch_shapes=[pltpu.VMEM((T, D), BF16), pltpu.VMEM((T, D), F32)],
        compiler_params=_params("arbitrary", "arbitrary"),
        name="dense_swiglu",
    )(x, g, wg, wu, wd)


def _qkv_kernel(x_ref, gkv_ref, gq_ref, wk_ref, wv_ref, wq_ref, gkn_ref, gqn_ref,
                gsum_ref, gbc_ref,
                kp_ref, vp_ref, ks_ref, vs_ref, qh_ref, kh_ref, vh_ref, qs_ref):
    i = pl.program_id(0)
    nt = pl.num_programs(0) - 1
    n_heads = qh_ref.shape[0]

    xn = _rms_unit(x_ref[...])
    hkv = (xn * gkv_ref[...]).astype(BF16)
    hq = (xn * gq_ref[...]).astype(BF16)

    def head_norm(y, g):
        ssq = _dot((y * y).astype(BF16), gsum_ref[...])
        rs = lax.rsqrt(ssq * (1.0 / HEAD_DIM) + EPS)
        rs_hi = rs.astype(BF16)
        rs_lo = (rs - rs_hi.astype(F32)).astype(BF16)
        rsb = _dot(rs_hi, gbc_ref[...]) + _dot(rs_lo, gbc_ref[...])
        return y * rsb * g

    k = head_norm(_dot(hkv, wk_ref[...]), gkn_ref[...])
    v = _dot(hkv, wv_ref[...])
    q = head_norm(_dot(hq, wq_ref[...]), gqn_ref[...])

    @pl.when(i < nt)
    def _():
        kp_ref[...] = k
        vp_ref[...] = v
        for h in range(n_heads):
            sl = slice(h * V_DIM, (h + 1) * V_DIM)
            qh_ref[h] = q[:, sl].T.astype(BF16)
            kh_ref[h] = k[:, sl].astype(BF16)
            vh_ref[h] = v[:, sl].T.astype(BF16)

    @pl.when(i == nt)
    def _():
        ks_ref[...] = k
        vs_ref[...] = v
        qs_ref[...] = q.astype(BF16)


def _qkv_proj(x, gkv, gq, wk, wv, wq, gkn_t, gqn_t, gsum, gbc, n_sample):
    N, D = x.shape
    T = ROW_TILE
    nt = N // T - 1
    Lp = N - n_sample
    n_heads = D // V_DIM
    const2 = lambda i: (0, 0)
    prow = lambda i: (jnp.minimum(i, nt - 1), 0)
    phead = lambda i: (0, jnp.minimum(i, nt - 1), 0)
    pheadt = lambda i: (0, 0, jnp.minimum(i, nt - 1))
    return pl.pallas_call(
        _qkv_kernel,
        grid=(nt + 1,),
        in_specs=[
            pl.BlockSpec((T, D), lambda i: (i, 0)),
            pl.BlockSpec((1, D), const2),
            pl.BlockSpec((1, D), const2),
            pl.BlockSpec((D, D), const2),
            pl.BlockSpec((D, D), const2),
            pl.BlockSpec((D, D), const2),
            pl.BlockSpec((1, D), const2),
            pl.BlockSpec((1, D), const2),
            pl.BlockSpec(gsum.shape, const2),
            pl.BlockSpec(gbc.shape, const2),
        ],
        out_specs=[
            pl.BlockSpec((T, D), prow),
            pl.BlockSpec((T, D), prow),
            pl.BlockSpec((T, D), const2),
            pl.BlockSpec((T, D), const2),
            pl.BlockSpec((n_heads, V_DIM, T), pheadt),
            pl.BlockSpec((n_heads, T, V_DIM), phead),
            pl.BlockSpec((n_heads, V_DIM, T), pheadt),
            pl.BlockSpec((T, D), const2),
        ],
        out_shape=[
            jax.ShapeDtypeStruct((Lp, D), F32),
            jax.ShapeDtypeStruct((Lp, D), F32),
            jax.ShapeDtypeStruct((n_sample, D), F32),
            jax.ShapeDtypeStruct((n_sample, D), F32),
            jax.ShapeDtypeStruct((n_heads, V_DIM, Lp), BF16),
            jax.ShapeDtypeStruct((n_heads, Lp, V_DIM), BF16),
            jax.ShapeDtypeStruct((n_heads, V_DIM, Lp), BF16),
            jax.ShapeDtypeStruct((n_sample, D), BF16),
        ],
        compiler_params=_params("arbitrary"),
        name="qkv_proj",
    )(x, gkv, gq, wk, wv, wq, gkn_t, gqn_t, gsum, gbc)


def _diff_out(o0, o1, lam, g, out_scale):
    o = o0 - lam * o1
    return _rms_unit(o) * g * out_scale


def _sublane_partial_sum(p):
    return jnp.sum(p.reshape(p.shape[0] // 8, 8, p.shape[1]), axis=0)


def _attn_prompt_kernel(bias_ref, lam_ref, qt_ref, k_ref, vt_ref, g_ref, o_ref,
                        bn_ref, l_ref, acc_ref, *m_scratch, out_scale):
    h = pl.program_id(0)
    i = pl.program_id(1)
    T = qt_ref.shape[2]
    online = bool(m_scratch)

    @pl.when(i == 0)
    def _():
        key = lax.broadcasted_iota(jnp.int32, (T, T), 0)
        qry = lax.broadcasted_iota(jnp.int32, (T, T), 1)
        tab = lambda b: bias_ref[b, h] * LOG2E
        visible = (key // CHUNK) <= (qry // CHUNK)
        bn_ref[0:T, :] = _rel_bias_tile(key - qry - T, tab)
        bn_ref[T:2 * T, :] = jnp.where(visible, _rel_bias_tile(key - qry, tab), NEG)

    if online:
        m_ref, = m_scratch
        m_ref[...] = jnp.full(m_ref.shape, NEG, F32)
    l_ref[...] = jnp.zeros(l_ref.shape, F32)
    acc_ref[...] = jnp.zeros(acc_ref.shape, F32)
    qt = qt_ref[0]
    dim = lax.broadcasted_iota(jnp.int32, qt.shape, 0)
    zero = jnp.zeros_like(qt)
    qc = (jnp.where(dim < HEAD_DIM, qt, zero), jnp.where(dim >= HEAD_DIM, qt, zero))

    def update(j, bias, width=T):
        start = pl.multiple_of(j * T, T)
        kt = k_ref[0, pl.ds(start, width), :]
        vt = vt_ref[0, :, pl.ds(start, width)]
        for c in range(2):
            s = _dot(kt, qc[c]) + bias
            if online:
                m_old = m_ref[c]
                m_new = jnp.maximum(m_old, jnp.max(s, axis=0, keepdims=True))
                alpha = jnp.exp2(m_old - m_new)
                p = jnp.exp2(s - m_new)
                l_ref[c] = alpha * l_ref[c] + _sublane_partial_sum(p)
                acc_ref[c] = alpha * acc_ref[c] + _dot(vt, p.astype(BF16))
                m_ref[c] = m_new
            else:
                p = jnp.exp2(s)
                l_ref[c] += _sublane_partial_sum(p)
                acc_ref[c] += _dot(vt, p.astype(BF16))

    far_bias = bias_ref[15, h] * LOG2E

    n_far = jnp.maximum(i - 1, 0)

    def far_body(jj, carry):
        update(FAR_TILES_PER_TRIP * jj, far_bias, width=FAR_TILES_PER_TRIP * T)
        return carry

    n_trips = n_far // FAR_TILES_PER_TRIP
    lax.fori_loop(0, n_trips, far_body, 0)
    done = n_trips * FAR_TILES_PER_TRIP

    @pl.when(n_far - done >= 2)
    def _():
        update(done, far_bias, width=2 * T)

    @pl.when((n_far - done) % 2 == 1)
    def _():
        update(n_far - 1, far_bias)

    @pl.when(i >= 1)
    def _():
        update(i - 1, bn_ref[...], width=2 * T)

    @pl.when(i == 0)
    def _():
        update(0, bn_ref[T:2 * T, :])

    o0 = acc_ref[0] / jnp.sum(l_ref[0], axis=0, keepdims=True)
    o1 = acc_ref[1] / jnp.sum(l_ref[1], axis=0, keepdims=True)
    o = o0 - lam_ref[0] * o1
    y = o * lax.rsqrt(jnp.mean(o * o, axis=0, keepdims=True) + EPS) * g_ref[...] * out_scale
    o_ref[...] = y.T.astype(o_ref.dtype)


def _attn_prompt(rel_bias, lam, qth, kh, vth, g_subln, out_scale, online):
    n_heads, Lp, _ = kh.shape
    T = ROW_TILE
    assert T % CHUNK == 0 and T >= FAR_DISTANCE and FAR_TILES_PER_TRIP == 4
    m_scratch = [pltpu.VMEM((2, 1, T), F32)] if online else []
    return pl.pallas_call(
        functools.partial(_attn_prompt_kernel, out_scale=out_scale),
        grid=(n_heads, Lp // T),
        in_specs=[
            pl.BlockSpec(memory_space=pltpu.SMEM),
            pl.BlockSpec(memory_space=pltpu.SMEM),
            pl.BlockSpec((1, V_DIM, T), lambda h, i: (h, 0, i)),
            pl.BlockSpec((1, Lp, V_DIM), lambda h, i: (h, 0, 0)),
            pl.BlockSpec((1, V_DIM, Lp), lambda h, i: (h, 0, 0)),
            pl.BlockSpec((V_DIM, 1), lambda h, i: (0, 0)),
        ],
        out_specs=pl.BlockSpec((T, V_DIM), lambda h, i: (i, h)),
        out_shape=jax.ShapeDtypeStruct((Lp, n_heads * V_DIM), BF16),
        scratch_shapes=[
            pltpu.VMEM((2 * T, T), F32),
            pltpu.VMEM((2, 8, T), F32),
            pltpu.VMEM((2, V_DIM, T), F32),
        ] + m_scratch,
        compiler_params=_params("arbitrary", "arbitrary"),
        name="attn_prompt_online" if online else "attn_prompt",
    )(rel_bias, lam, qth, kh, vth, g_subln)


def _attn_sample_kernel(lam_ref, q_ref, ckt_ref, cv_ref, kn_ref, vn_ref, tab_ref, g_ref,
                        o_ref, nearb_ref, newb_ref, qbd_ref, m_ref, l_ref, acc_ref, *, out_scale):
    b = pl.program_id(0)
    kb = pl.program_id(1)
    nkb = pl.num_programs(1)
    R, Tk = nearb_ref.shape
    L = kn_ref.shape[0]
    D = q_ref.shape[1]
    n_heads = D // V_DIM

    @pl.when((b == 0) & (kb == 0))
    def _():
        tab = lambda bkt: tab_ref[:, bkt:bkt + 1]
        t_near = lax.broadcasted_iota(jnp.int32, (R, Tk), 0) % L
        col = lax.broadcasted_iota(jnp.int32, (R, Tk), 1)
        nearb_ref[...] = _rel_bias_tile(col - Tk - t_near, tab)
        t_new = lax.broadcasted_iota(jnp.int32, (R, L), 0) % L
        col_new = lax.broadcasted_iota(jnp.int32, (R, L), 1)
        newb_ref[...] = _rel_bias_tile(col_new - t_new, tab)

    @pl.when(kb == 0)
    def _():
        m_ref[...] = jnp.full(m_ref.shape, NEG, F32)
        l_ref[...] = jnp.zeros(l_ref.shape, F32)
        acc_ref[...] = jnp.zeros(acc_ref.shape, F32)
        q_rows = jnp.concatenate([q_ref[...]] * (R // L), axis=0)
        row_group = lax.broadcasted_iota(jnp.int32, (R, D), 0) // L
        col_group = lax.broadcasted_iota(jnp.int32, (R, D), 1) // HEAD_DIM
        qbd_ref[...] = jnp.where(row_group == col_group, q_rows, jnp.zeros_like(q_rows))

    rows_per_head = 2 * L

    def update(s, value_of_head):
        m_old = m_ref[...]
        m_new = jnp.maximum(m_old, jnp.max(s, axis=-1, keepdims=True))
        alpha = jnp.exp2(m_old - m_new)
        p = jnp.exp2(s - m_new)
        l_ref[...] = alpha * l_ref[...] + jnp.sum(p, axis=-1, keepdims=True)
        pb = p.astype(BF16)
        for h in range(n_heads):
            rs = slice(h * rows_per_head, (h + 1) * rows_per_head)
            acc_ref[rs, :] = alpha[rs, :] * acc_ref[rs, :] + _dot(pb[rs, :], value_of_head(h))
        m_ref[...] = m_new

    qbd = qbd_ref[...]
    s = _dot(qbd, ckt_ref[0].astype(BF16))
    cache_value = lambda h: cv_ref[0, pl.ds(h, Tk, stride=n_heads), :].astype(BF16)

    @pl.when(kb < nkb - 1)
    def _():
        update(s + tab_ref[:, 15:16], cache_value)

    @pl.when(kb == nkb - 1)
    def _():
        update(s + nearb_ref[...], cache_value)
        s_new = _dot_nt(qbd, kn_ref[...].astype(BF16)) + newb_ref[...]
        update(s_new, lambda h: vn_ref[:, h * V_DIM:(h + 1) * V_DIM].astype(BF16))
        lam = lam_ref[0]
        o = acc_ref[...] / l_ref[...]
        for h in range(n_heads):
            r0 = h * rows_per_head
            o_ref[:, h * V_DIM:(h + 1) * V_DIM] = _diff_out(
                o[r0:r0 + L, :], o[r0 + L:r0 + 2 * L, :], lam, g_ref[...], out_scale).astype(o_ref.dtype)


def _attn_sample(lam, q, cache_k, cache_v, k_new, v_new, tab, g_subln, out_scale):
    B, past, n_heads = cache_v.shape[:3]
    D = n_heads * V_DIM
    L = k_new.shape[0] // B
    R = (D // HEAD_DIM) * L
    Tk = min(1024, past)
    assert past % Tk == 0 and Tk >= FAR_DISTANCE + L
    cache_kt = jnp.transpose(cache_k, (0, 2, 3, 4, 1)).reshape(B, D, past)
    return pl.pallas_call(
        functools.partial(_attn_sample_kernel, out_scale=out_scale),
        grid=(B, past // Tk),
        in_specs=[
            pl.BlockSpec(memory_space=pltpu.SMEM),
            pl.BlockSpec((L, D), lambda b, k: (b, 0)),
            pl.BlockSpec((1, D, Tk), lambda b, k: (b, 0, k)),
            pl.BlockSpec((1, Tk * n_heads, V_DIM), lambda b, k: (b, k, 0)),
            pl.BlockSpec((L, D), lambda b, k: (b, 0)),
            pl.BlockSpec((L, D), lambda b, k: (b, 0)),
            pl.BlockSpec(tab.shape, lambda b, k: (0, 0)),
            pl.BlockSpec((1, V_DIM), lambda b, k: (0, 0)),
        ],
        out_specs=pl.BlockSpec((L, D), lambda b, k: (b, 0)),
        out_shape=jax.ShapeDtypeStruct((B * L, D), BF16),
        scratch_shapes=[
            pltpu.VMEM((R, Tk), F32),
            pltpu.VMEM((R, L), F32),
            pltpu.VMEM((R, D), BF16),
            pltpu.VMEM((R, 1), F32),
            pltpu.VMEM((R, 1), F32),
            pltpu.VMEM((R, V_DIM), F32),
        ],
        compiler_params=_params("arbitrary", "arbitrary"),
        name="attn_sample",
    )(lam, q, cache_kt, cache_v.reshape(B, past * n_heads, V_DIM), k_new, v_new, tab, g_subln)


def _oproj_router_kernel(op_ref, os_ref, x_ref, wo_ref, g_ref, wrh_ref, wrl_ref,
                         x3_ref, h_ref, route_ref, route_t_ref):
    i = pl.program_id(0)
    nt = pl.num_programs(0) - 1
    o = jnp.where(i == nt, os_ref[...], op_ref[...])
    x3 = x_ref[...] + _dot(o, wo_ref[...])
    x3_ref[...] = x3
    h = _rms_unit(x3) * g_ref[...]
    h_hi = h.astype(BF16)
    h_ref[...] = h_hi
    h_lo = (h - h_hi.astype(F32)).astype(BF16)
    logits = _dot(h_hi, wrh_ref[...]) + (_dot(h_lo, wrh_ref[...]) + _dot(h_hi, wrl_ref[...]))
    lane = lax.broadcasted_iota(jnp.int32, logits.shape, 1)
    lg = jnp.where(lane < N_EXPERTS, logits, -jnp.inf)
    m1 = jnp.max(lg, axis=-1, keepdims=True)
    i1 = jnp.min(jnp.where(lg == m1, lane, LANE), axis=-1, keepdims=True)
    lg2 = jnp.where(lane == i1, -jnp.inf, lg)
    m2 = jnp.max(lg2, axis=-1, keepdims=True)
    i2 = jnp.min(jnp.where(lg2 == m2, lane, LANE), axis=-1, keepdims=True)
    e2 = jnp.exp(m2 - m1)
    den = 1.0 + e2
    g1 = 1.0 / den
    g2 = e2 / den
    route = jnp.where(lane == 0, i1.astype(F32),
                      jnp.where(lane == 1, i2.astype(F32),
                                jnp.where(lane == 2, g1,
                                          jnp.where(lane == 3, g2, 0.0))))
    route_ref[...] = route
    route_t_ref[...] = route.T


def _oproj_router(o_p, o_s, x, wo, g, wr_hi, wr_lo):
    N, D = x.shape
    T = ROW_TILE
    nt = N // T - 1
    const2 = lambda i: (0, 0)
    row = lambda i: (i, 0)
    return pl.pallas_call(
        _oproj_router_kernel,
        grid=(nt + 1,),
        in_specs=[
            pl.BlockSpec((T, D), lambda i: (jnp.minimum(i, nt - 1), 0)),
            pl.BlockSpec((T, D), const2),
            pl.BlockSpec((T, D), row),
            pl.BlockSpec((D, D), const2),
            pl.BlockSpec((1, D), const2),
            pl.BlockSpec((D, LANE), const2),
            pl.BlockSpec((D, LANE), const2),
        ],
        out_specs=[
            pl.BlockSpec((T, D), row),
            pl.BlockSpec((T, D), row),
            pl.BlockSpec((T, LANE), row),
            pl.BlockSpec((LANE, T), lambda i: (0, i)),
        ],
        out_shape=[
            jax.ShapeDtypeStruct((N, D), F32),
            jax.ShapeDtypeStruct((N, D), BF16),
            jax.ShapeDtypeStruct((N, LANE), F32),
            jax.ShapeDtypeStruct((LANE, N), F32),
        ],
        compiler_params=_params("arbitrary"),
        name="oproj_router",
    )(o_p, o_s, x, wo, g, wr_hi, wr_lo)


MOE_CHUNK = 128
MOE_ROW_ALIGN = 16


def _for_each_chunk(cnt_ref, step, n_chunks, fn):
    for e in range(N_EXPERTS):
        for k in range(n_chunks):
            @pl.when(cnt_ref[step * N_EXPERTS + e] > k * MOE_CHUNK)
            def _(e=e, k=k):
                fn(e, k)


def _dispatch_kernel(off_ref, cnt_ref, hb_ref, rt_ref, utri_ref, xs_in_ref, xs_ref, stage_ref, sem):
    del xs_in_ref
    t = pl.program_id(0)
    T = hb_ref.shape[0]
    S = MOE_CHUNK
    nk = T // S

    def chunk_copy(step, e, k):
        start = pl.multiple_of(off_ref[step * N_EXPERTS + e] + k * S, MOE_ROW_ALIGN)
        slot = e * nk + k
        return pltpu.make_async_copy(stage_ref.at[slot], xs_ref.at[pl.ds(start, S), :], sem.at[slot])

    @pl.when(t > 0)
    def _():
        _for_each_chunk(cnt_ref, t - 1, nk, lambda e, k: chunk_copy(t - 1, e, k).wait())

    rt = rt_ref[...]
    expert_id = lax.broadcasted_iota(jnp.int32, (N_EXPERTS, T), 0).astype(F32)
    member = (rt[0:1, :] == expert_id) | (rt[1:2, :] == expert_id)
    rank = _dot(member.astype(BF16), utri_ref[...])
    rank = jnp.where(member, rank, 0.0)
    hb = hb_ref[...]
    row = lax.broadcasted_iota(jnp.int32, (S, 1), 0)

    def emit(e, k):
        sel = rank[e:e + 1, :] == (row + (k * S + 1)).astype(F32)
        stage_ref[e * nk + k] = _dot(sel.astype(BF16), hb).astype(BF16)
        chunk_copy(t, e, k).start()

    _for_each_chunk(cnt_ref, t, nk, emit)

    @pl.when(t == pl.num_programs(0) - 1)
    def _():
        _for_each_chunk(cnt_ref, t, nk, lambda e, k: chunk_copy(t, e, k).wait())


def _dispatch(off, cnt, hb, route_t, utri, n_rows):
    N, D = hb.shape
    T = ROW_TILE
    nk = T // MOE_CHUNK
    n_slots = N_EXPERTS * nk
    xs_init = jnp.zeros((n_rows, D), BF16)
    return pl.pallas_call(
        _dispatch_kernel,
        grid_spec=pltpu.PrefetchScalarGridSpec(
            num_scalar_prefetch=2,
            grid=(N // T,),
            in_specs=[
                pl.BlockSpec((T, D), lambda t, off, cnt: (t, 0)),
                pl.BlockSpec((LANE, T), lambda t, off, cnt: (0, t)),
                pl.BlockSpec((T, T), lambda t, off, cnt: (0, 0)),
                pl.BlockSpec(memory_space=pl.ANY),
            ],
            out_specs=pl.BlockSpec(memory_space=pl.ANY),
            scratch_shapes=[
                pltpu.VMEM((n_slots, MOE_CHUNK, D), BF16),
                pltpu.SemaphoreType.DMA((n_slots,)),
            ],
        ),
        out_shape=jax.ShapeDtypeStruct((n_rows, D), BF16),
        input_output_aliases={5: 0},
        compiler_params=_params("arbitrary"),
        name="moe_dispatch",
    )(off, cnt, hb, route_t, utri, xs_init)


def _experts_kernel(te_ref, na_ref, xs_ref, wg_ref, wu_ref, wd_ref, out_ref, acc_ref):
    i = pl.program_id(0)
    f = pl.program_id(1)
    last = pl.num_programs(1) - 1
    active = i < na_ref[0]

    @pl.when(active)
    def _():
        @pl.when(f == 0)
        def _():
            acc_ref[...] = jnp.zeros(acc_ref.shape, F32)

        acc_ref[...] += _swiglu_tile(xs_ref[...], wg_ref[0], wu_ref[0], wd_ref[0])

        @pl.when(f == last)
        def _():
            out_ref[...] = acc_ref[...].astype(out_ref.dtype)

    @pl.when(jnp.logical_not(active) & (f == last))
    def _():
        out_ref[...] = jnp.zeros(out_ref.shape, out_ref.dtype)


def _experts(tile_expert, n_active, xs, wg, wu, wd):
    P, D = xs.shape
    T = ROW_TILE
    nf, tf = _ffn_splits(wg.shape[2])
    fidx = lambda i, f, na: jnp.where(i < na[0], f, nf - 1)
    return pl.pallas_call(
        _experts_kernel,
        grid_spec=pltpu.PrefetchScalarGridSpec(
            num_scalar_prefetch=2,
            grid=(P // T, nf),
            in_specs=[
                pl.BlockSpec((T, D), lambda i, f, te, na: (i, 0)),
                pl.BlockSpec((1, D, tf), lambda i, f, te, na: (te[i], 0, fidx(i, f, na))),
                pl.BlockSpec((1, D, tf), lambda i, f, te, na: (te[i], 0, fidx(i, f, na))),
                pl.BlockSpec((1, tf, D), lambda i, f, te, na: (te[i], fidx(i, f, na), 0)),
            ],
            out_specs=pl.BlockSpec((T, D), lambda i, f, te, na: (i, 0)),
            scratch_shapes=[pltpu.VMEM((T, D), F32)],
        ),
        out_shape=jax.ShapeDtypeStruct((P, D), BF16),
        compiler_params=_params("arbitrary", "arbitrary"),
        name="moe_experts",
    )(tile_expert, n_active, xs, wg, wu, wd)


def _combine_kernel(off_ref, cnt_ref, x_ref, route_ref, ltri_ref, ys_hbm, yp_ref, ysm_ref,
                    buf_ref, acc_ref, sem):
    t = pl.program_id(0)
    nt = pl.num_programs(0) - 1
    T = x_ref.shape[0]
    S = MOE_CHUNK
    nk = T // S

    def chunk_copy(e, k):
        start = pl.multiple_of(off_ref[t * N_EXPERTS + e] + k * S, MOE_ROW_ALIGN)
        slot = e * nk + k
        return pltpu.make_async_copy(ys_hbm.at[pl.ds(start, S), :], buf_ref.at[slot], sem.at[slot])

    _for_each_chunk(cnt_ref, t, nk, lambda e, k: chunk_copy(e, k).start())

    route = route_ref[...]
    e1, e2, g1, g2 = (route[:, j:j + 1] for j in range(2 * TOP_K))
    lane = lax.broadcasted_iota(jnp.int32, route.shape, 1).astype(F32)
    member = (lane == e1) | (lane == e2)
    rank = _dot(ltri_ref[...], member.astype(BF16))
    rank = jnp.where(member, rank, 0.0)
    col = lax.broadcasted_iota(jnp.int32, (1, S), 1)
    acc_ref[...] = x_ref[...]

    def absorb(e, k):
        chunk_copy(e, k).wait()
        sel = rank[:, e:e + 1] == (col + (k * S + 1)).astype(F32)
        gate = jnp.where(e1 == e, g1, 0.0) + jnp.where(e2 == e, g2, 0.0)
        acc_ref[...] += gate * _dot(sel.astype(BF16), buf_ref[e * nk + k])

    _for_each_chunk(cnt_ref, t, nk, absorb)

    @pl.when(t < nt)
    def _():
        yp_ref[...] = acc_ref[...]

    @pl.when(t == nt)
    def _():
        ysm_ref[...] = acc_ref[...]


def _combine(off, cnt, x3, route, ltri, ys, n_sample):
    N, D = x3.shape
    T = ROW_TILE
    nt = N // T - 1
    n_slots = N_EXPERTS * (T // MOE_CHUNK)
    return pl.pallas_call(
        _combine_kernel,
        grid_spec=pltpu.PrefetchScalarGridSpec(
            num_scalar_prefetch=2,
            grid=(nt + 1,),
            in_specs=[
                pl.BlockSpec((T, D), lambda t, off, cnt: (t, 0)),
                pl.BlockSpec((T, LANE), lambda t, off, cnt: (t, 0)),
                pl.BlockSpec((T, T), lambda t, off, cnt: (0, 0)),
                pl.BlockSpec(memory_space=pl.ANY),
            ],
            out_specs=[
                pl.BlockSpec((T, D), lambda t, off, cnt: (jnp.minimum(t, nt - 1), 0)),
                pl.BlockSpec((T, D), lambda t, off, cnt: (0, 0)),
            ],
            scratch_shapes=[
                pltpu.VMEM((n_slots, MOE_CHUNK, D), BF16),
                pltpu.VMEM((T, D), F32),
                pltpu.SemaphoreType.DMA((n_slots,)),
            ],
        ),
        out_shape=[
            jax.ShapeDtypeStruct((N - n_sample, D), F32),
            jax.ShapeDtypeStruct((n_sample, D), F32),
        ],
        compiler_params=_params("arbitrary"),
        name="moe_combine",
    )(off, cnt, x3, route, ltri, ys)


def _routing_tables(route, tile):
    n = route.shape[0]
    n_tok_tiles = n // tile
    experts = route[:, :TOP_K].astype(jnp.int32)
    onehot = (experts[:, :, None] == jnp.arange(N_EXPERTS)[None, None, :]).astype(jnp.int32).sum(axis=1)
    cnt = onehot.reshape(n_tok_tiles, tile, N_EXPERTS).sum(axis=1)
    span = (cnt + MOE_ROW_ALIGN - 1) // MOE_ROW_ALIGN * MOE_ROW_ALIGN
    totals = span.sum(axis=0)
    tiles_per = (totals + MOE_CHUNK + tile - 1) // tile
    tile_end = jnp.cumsum(tiles_per)
    group_off = (tile_end - tiles_per) * tile
    off = group_off[None, :] + jnp.cumsum(span, axis=0) - span
    max_rows = n * TOP_K + N_EXPERTS * (MOE_CHUNK + n_tok_tiles * (MOE_ROW_ALIGN - 1))
    n_row_tiles = max_rows // tile + N_EXPERTS
    n_active = tile_end[-1:].astype(jnp.int32)
    tile_ids = jnp.minimum(jnp.arange(n_row_tiles, dtype=jnp.int32), n_active[0] - 1)
    tile_expert = jnp.sum((tile_ids[:, None] >= tile_end[None, :]).astype(jnp.int32), axis=1)
    return (off.reshape(-1).astype(jnp.int32), cnt.reshape(-1).astype(jnp.int32),
            tile_expert, n_active, n_row_tiles * tile)


def _group_matrices(d_model):
    n_groups = d_model // HEAD_DIM
    gsum = np.zeros((d_model, LANE), np.float32)
    gsum[np.arange(d_model), np.arange(d_model) // HEAD_DIM] = 1.0
    assert n_groups <= LANE
    return jnp.asarray(gsum, BF16), jnp.asarray(gsum.T.copy(), BF16)


def kernel(x_prompt, x_sample, cache_k, cache_v, state_pool, g_pool_norm, w_pool, pool_scale, g_attn, w_q, g_qn, lambda_q1, lambda_k1, lambda_q2, lambda_k2, g_subln, w_o, g_kv, w_k, w_v, g_kn, rel_bias, g_ffn, w_gate_dense, w_up_dense, w_down_dense, w_router, w_gate_moe, w_up_moe, w_down_moe):
    Bp, Lp, D = x_prompt.shape
    Bs, Ls, _ = x_sample.shape
    past = cache_k.shape[1]
    n_heads = D // V_DIM
    n_sample = Bs * Ls
    assert Bp == 1 and n_sample == ROW_TILE and Lp % ROW_TILE == 0
    assert g_pool_norm.shape[0] == 1 and g_attn.shape[0] == 1
    bf = lambda a: a.astype(BF16)
    row = lambda a: a.reshape(1, -1)

    x1, pool_p, pool_s = _pool_layer(
        x_prompt.reshape(Lp, D), x_sample.reshape(n_sample, D), state_pool[0],
        row(g_pool_norm[0]), bf(w_pool[0]), row(pool_scale[0]), past)
    x2 = _dense_ffn(x1, row(g_ffn[0]), bf(w_gate_dense[0]), bf(w_up_dense[0]), bf(w_down_dense[0]))

    layer = 1
    lambda_init = 0.8 - 0.6 * math.exp(-0.3 * layer)
    lam = (jnp.exp(jnp.sum(lambda_q1[0] * lambda_k1[0])) - jnp.exp(jnp.sum(lambda_q2[0] * lambda_k2[0]))
           + lambda_init).reshape(1)
    gsum, gbc = _group_matrices(D)
    n_groups = D // HEAD_DIM
    k_p, v_p, k_s, v_s, qh, kh, vh, q_s = _qkv_proj(
        x2, row(g_kv), row(g_attn[0]), bf(w_k), bf(w_v), bf(w_q[0]),
        row(jnp.tile(g_kn, n_groups)), row(jnp.tile(g_qn[0], n_groups) * (HEAD_DIM ** -0.5 * LOG2E)),
        gsum, gbc, n_sample)
    out_scale = 1.0 - lambda_init
    g_sub = row(g_subln[0])
    score_bound = LOG2E * (math.sqrt(HEAD_DIM) * jnp.max(jnp.abs(g_qn[0])) * jnp.max(jnp.abs(g_kn))
                           + jnp.max(jnp.abs(rel_bias)))
    o_p = lax.cond(
        score_bound <= UNSHIFTED_SCORE_LIMIT,
        functools.partial(_attn_prompt, out_scale=out_scale, online=False),
        functools.partial(_attn_prompt, out_scale=out_scale, online=True),
        rel_bias, lam, qh, kh, vh, g_subln[0].reshape(V_DIM, 1))

    tab = jnp.repeat(rel_bias.T, 2 * Ls, axis=0) * LOG2E
    o_s = _attn_sample(lam, q_s, cache_k, cache_v, k_s, v_s, tab, g_sub, out_scale)

    wr = jnp.pad(w_router[0], ((0, 0), (0, LANE - N_EXPERTS)))
    wr_hi = bf(wr)
    wr_lo = bf(wr - wr_hi.astype(F32))
    x3, h_moe, route, route_t = _oproj_router(o_p, o_s, x2, bf(w_o[0]), row(g_ffn[1]), wr_hi, wr_lo)
    off, cnt, tile_expert, n_active, n_rows = _routing_tables(route, ROW_TILE)
    ltri = jnp.asarray(np.tril(np.ones((ROW_TILE, ROW_TILE), np.float32)), BF16)
    xs = _dispatch(off, cnt, h_moe, route_t, ltri.T, n_rows)
    ys = _experts(tile_expert, n_active, xs, bf(w_gate_moe[0]), bf(w_up_moe[0]), bf(w_down_moe[0]))
    y_p, y_s = _combine(off, cnt, x3, route, ltri, ys, n_sample)

    return (y_p.reshape(Bp, Lp, D), y_s.reshape(Bs, Ls, D),
            k_p.reshape(Bp, Lp, n_heads, 2, HEAD_DIM), v_p.reshape(Bp, Lp, n_heads, V_DIM),
            pool_p.reshape(1, Bp, POOL_STATE, D),
            k_s.reshape(Bs, Ls, n_heads, 2, HEAD_DIM), v_s.reshape(Bs, Ls, n_heads, V_DIM),
            pool_s.reshape(1, Bs, POOL_STATE, D))
```

```python
import functools
import math

import numpy as np
import jax
import jax.numpy as jnp
from jax import lax
from jax.experimental import pallas as pl
from jax.experimental.pallas import tpu as pltpu

EPS = 1e-6
CHUNK = 64
POOL_WINDOWS = (2, 4, 8, 16)
POOL_STATE = max(POOL_WINDOWS) - 1
HEAD_DIM = 64
V_DIM = 2 * HEAD_DIM
N_EXPERTS = 8
TOP_K = 2
MAX_EXACT = 8
BUCKET_UPPER = (1, 2, 3, 4, 5, 6, 7, 8, 12, 16, 23, 32, 46, 64, 91)
FAR_DISTANCE = 128
NEG = -1e30
LOG2E = math.log2(math.e)
UNSHIFTED_SCORE_LIMIT = 80.0

ROW_TILE = 512
FAR_TILES_PER_TRIP = 4
LANE = 128
VMEM_LIMIT = 56 * 1024 * 1024

F32 = jnp.float32
BF16 = jnp.bfloat16


def _dot(a, b):
    return jnp.dot(a, b, preferred_element_type=F32)


def _dot_nt(a, b):
    return lax.dot_general(a, b, (((1,), (1,)), ((), ())), preferred_element_type=F32)


def _rms_unit(x):
    return x * lax.rsqrt(jnp.mean(x * x, axis=-1, keepdims=True) + EPS)


def _rel_bias_tile(rel, tab):
    n = jnp.abs(rel)
    neg = tab(15)
    pos = tab(31)
    for b in range(14, -1, -1):
        lt = n < BUCKET_UPPER[b]
        neg = jnp.where(lt, tab(b), neg)
        pos = jnp.where(lt, tab(16 + b), pos)
    return jnp.where(rel > 0, pos, neg)


def _params(*sem):
    return pltpu.CompilerParams(dimension_semantics=sem, vmem_limit_bytes=VMEM_LIMIT)


def _pool_kernel(xp_ref, xs_ref, st_ref, g_ref, w_ref, sc_ref,
                 x1_ref, pp_ref, ps_ref, ext_ref, ext3_ref, *, past_len):
    i = pl.program_id(0)
    nt = pl.num_programs(0) - 1
    T, D = xp_ref.shape
    gw = D // len(POOL_WINDOWS)

    @pl.when(i < nt)
    def _prompt():
        x = xp_ref[...]
        h = _rms_unit(x) * g_ref[...]

        @pl.when(i == 0)
        def _():
            ext_ref[0:16, :] = jnp.zeros((16, D), F32)

        ext_ref[16:16 + T, :] = h
        row = i * T + lax.broadcasted_iota(jnp.int32, (T, 1), 0)
        parts = []
        for gi, w in enumerate(POOL_WINDOWS):
            c0 = gi * gw
            s = ext_ref[16:16 + T, c0:c0 + gw]
            for j in range(1, w):
                s = s + ext_ref[16 - j:16 - j + T, c0:c0 + gw]
            cnt = jnp.minimum(w, row + 1).astype(F32)
            pooled = s / cnt - h[:, c0:c0 + gw]
            parts.append(_dot(pooled.astype(BF16), w_ref[gi]))
        mix = jnp.concatenate(parts, axis=-1) * sc_ref[...]
        x1_ref[...] = x + mix
        tail = ext_ref[T:T + 16, :]
        ext_ref[0:16, :] = tail

        @pl.when(i == nt - 1)
        def _():
            pp_ref[...] = tail[1:16, :]

    @pl.when(i == nt)
    def _sample():
        B = st_ref.shape[0]
        L = T // B
        x = xs_ref[...]
        h = _rms_unit(x) * g_ref[...]
        ext3_ref[:, 1:16, :] = st_ref[...]
        ext3_ref[:, 16:16 + L, :] = h.reshape(B, L, D)
        t = lax.broadcasted_iota(jnp.int32, (1, L, 1), 1)
        parts = []
        for gi, w in enumerate(POOL_WINDOWS):
            c0 = gi * gw
            s = ext3_ref[:, 16:16 + L, c0:c0 + gw]
            for j in range(1, w):
                s = s + ext3_ref[:, 16 - j:16 - j + L, c0:c0 + gw]
            cnt = jnp.minimum(w, past_len + t + 1).astype(F32)
            pooled = (s / cnt).reshape(T, gw) - h[:, c0:c0 + gw]
            parts.append(_dot(pooled.astype(BF16), w_ref[gi]))
        mix = jnp.concatenate(parts, axis=-1) * sc_ref[...]
        x1_ref[...] = x + mix
        ps_ref[...] = ext3_ref[:, 16 + L - POOL_STATE:16 + L, :]


def _pool_layer(xp, xs, state, g, w, sc, past_len):
    Lp, D = xp.shape
    T = ROW_TILE
    nt = Lp // T
    B = state.shape[0]
    L = xs.shape[0] // B
    return pl.pallas_call(
        functools.partial(_pool_kernel, past_len=past_len),
        grid=(nt + 1,),
        in_specs=[
            pl.BlockSpec((T, D), lambda i: (jnp.minimum(i, nt - 1), 0)),
            pl.BlockSpec((T, D), lambda i: (0, 0)),
            pl.BlockSpec((B, POOL_STATE, D), lambda i: (0, 0, 0)),
            pl.BlockSpec((1, D), lambda i: (0, 0)),
            pl.BlockSpec(w.shape, lambda i: (0, 0, 0)),
            pl.BlockSpec((1, D), lambda i: (0, 0)),
        ],
        out_specs=[
            pl.BlockSpec((T, D), lambda i: (i, 0)),
            pl.BlockSpec((POOL_STATE, D), lambda i: (0, 0)),
            pl.BlockSpec((B, POOL_STATE, D), lambda i: (0, 0, 0)),
        ],
        out_shape=[
            jax.ShapeDtypeStruct((Lp + T, D), F32),
            jax.ShapeDtypeStruct((POOL_STATE, D), F32),
            jax.ShapeDtypeStruct((B, POOL_STATE, D), F32),
        ],
        scratch_shapes=[
            pltpu.VMEM((16 + T, D), F32),
            pltpu.VMEM((B, 16 + L, D), F32),
        ],
        compiler_params=_params("arbitrary"),
        name="pool_mixer",
    )(xp, xs, state, g, w, sc)


def _swiglu_tile(hb, wg, wu, wd):
    gt = _dot(hb, wg)
    ut = _dot(hb, wu)
    a = gt * jax.nn.sigmoid(gt) * ut
    return _dot(a.astype(BF16), wd)


def _ffn_kernel(x_ref, g_ref, wg_ref, wu_ref, wd_ref, out_ref, hb_ref, acc_ref):
    f = pl.program_id(1)

    @pl.when(f == 0)
    def _():
        x = x_ref[...]
        hb_ref[...] = (_rms_unit(x) * g_ref[...]).astype(BF16)
        acc_ref[...] = x

    acc_ref[...] += _swiglu_tile(hb_ref[...], wg_ref[...], wu_ref[...], wd_ref[...])

    @pl.when(f == pl.num_programs(1) - 1)
    def _():
        out_ref[...] = acc_ref[...]


def _ffn_splits(d_ff):
    nf = 2 if (d_ff // 2) % LANE == 0 else 1
    return nf, d_ff // nf


def _dense_ffn(x, g, wg, wu, wd):
    N, D = x.shape
    T = ROW_TILE
    nf, tf = _ffn_splits(wg.shape[1])
    return pl.pallas_call(
        _ffn_kernel,
        grid=(N // T, nf),
        in_specs=[
            pl.BlockSpec((T, D), lambda i, f: (i, 0)),
            pl.BlockSpec((1, D), lambda i, f: (0, 0)),
            pl.BlockSpec((D, tf), lambda i, f: (0, f)),
            pl.BlockSpec((D, tf), lambda i, f: (0, f)),
            pl.BlockSpec((tf, D), lambda i, f: (f, 0)),
        ],
        out_specs=pl.BlockSpec((T, D), lambda i, f: (i, 0)),
        out_shape=jax.ShapeDtypeStruct((N, D), F32),
        scratch_shapes=[pltpu.VMEM((T, D), BF16), pltpu.VMEM((T, D), F32)],
        compiler_params=_params("arbitrary", "arbitrary"),
        name="dense_swiglu",
    )(x, g, wg, wu, wd)


def _qkv_kernel(x_ref, gkv_ref, gq_ref, wk_ref, wv_ref, wq_ref, gkn_ref, gqn_ref,
                gsum_ref, gbc_ref,
                kp_ref, vp_ref, ks_ref, vs_ref, qh_ref, kh_ref, vh_ref, qs_ref):
    i = pl.program_id(0)
    nt = pl.num_programs(0) - 1
    n_heads = qh_ref.shape[0]

    xn = _rms_unit(x_ref[...])
    hkv = (xn * gkv_ref[...]).astype(BF16)
    hq = (xn * gq_ref[...]).astype(BF16)

    def head_norm(y, g):
        ssq = _dot((y * y).astype(BF16), gsum_ref[...])
        rs = lax.rsqrt(ssq * (1.0 / HEAD_DIM) + EPS)
        rs_hi = rs.astype(BF16)
        rs_lo = (rs - rs_hi.astype(F32)).astype(BF16)
        rsb = _dot(rs_hi, gbc_ref[...]) + _dot(rs_lo, gbc_ref[...])
        return y * rsb * g

    k = head_norm(_dot(hkv, wk_ref[...]), gkn_ref[...])
    v = _dot(hkv, wv_ref[...])
    q = head_norm(_dot(hq, wq_ref[...]), gqn_ref[...])

    @pl.when(i < nt)
    def _():
        kp_ref[...] = k
        vp_ref[...] = v
        for h in range(n_heads):
            sl = slice(h * V_DIM, (h + 1) * V_DIM)
            qh_ref[h] = q[:, sl].T.astype(BF16)
            kh_ref[h] = k[:, sl].astype(BF16)
            vh_ref[h] = v[:, sl].T.astype(BF16)

    @pl.when(i == nt)
    def _():
        ks_ref[...] = k
        vs_ref[...] = v
        qs_ref[...] = q.astype(BF16)


def _qkv_proj(x, gkv, gq, wk, wv, wq, gkn_t, gqn_t, gsum, gbc, n_sample):
    N, D = x.shape
    T = ROW_TILE
    nt = N // T - 1
    Lp = N - n_sample
    n_heads = D // V_DIM
    const2 = lambda i: (0, 0)
    prow = lambda i: (jnp.minimum(i, nt - 1), 0)
    phead = lambda i: (0, jnp.minimum(i, nt - 1), 0)
    pheadt = lambda i: (0, 0, jnp.minimum(i, nt - 1))
    return pl.pallas_call(
        _qkv_kernel,
        grid=(nt + 1,),
        in_specs=[
            pl.BlockSpec((T, D), lambda i: (i, 0)),
            pl.BlockSpec((1, D), const2),
            pl.BlockSpec((1, D), const2),
            pl.BlockSpec((D, D), const2),
            pl.BlockSpec((D, D), const2),
            pl.BlockSpec((D, D), const2),
            pl.BlockSpec((1, D), const2),
            pl.BlockSpec((1, D), const2),
            pl.BlockSpec(gsum.shape, const2),
            pl.BlockSpec(gbc.shape, const2),
        ],
        out_specs=[
            pl.BlockSpec((T, D), prow),
            pl.BlockSpec((T, D), prow),
            pl.BlockSpec((T, D), const2),
            pl.BlockSpec((T, D), const2),
            pl.BlockSpec((n_heads, V_DIM, T), pheadt),
            pl.BlockSpec((n_heads, T, V_DIM), phead),
            pl.BlockSpec((n_heads, V_DIM, T), pheadt),
            pl.BlockSpec((T, D), const2),
        ],
        out_shape=[
            jax.ShapeDtypeStruct((Lp, D), F32),
            jax.ShapeDtypeStruct((Lp, D), F32),
            jax.ShapeDtypeStruct((n_sample, D), F32),
            jax.ShapeDtypeStruct((n_sample, D), F32),
            jax.ShapeDtypeStruct((n_heads, V_DIM, Lp), BF16),
            jax.ShapeDtypeStruct((n_heads, Lp, V_DIM), BF16),
            jax.ShapeDtypeStruct((n_heads, V_DIM, Lp), BF16),
            jax.ShapeDtypeStruct((n_sample, D), BF16),
        ],
        compiler_params=_params("arbitrary"),
        name="qkv_proj",
    )(x, gkv, gq, wk, wv, wq, gkn_t, gqn_t, gsum, gbc)


def _diff_out(o0, o1, lam, g, out_scale):
    o = o0 - lam * o1
    return _rms_unit(o) * g * out_scale


def _sublane_partial_sum(p):
    return jnp.sum(p.reshape(p.shape[0] // 8, 8, p.shape[1]), axis=0)


def _attn_prompt_kernel(bias_ref, lam_ref, qt_ref, k_ref, vt_ref, g_ref, o_ref,
                        bn_ref, l_ref, acc_ref, *m_scratch, out_scale):
    h = pl.program_id(0)
    i = pl.program_id(1)
    T = qt_ref.shape[2]
    online = bool(m_scratch)

    @pl.when(i == 0)
    def _():
        key = lax.broadcasted_iota(jnp.int32, (T, T), 0)
        qry = lax.broadcasted_iota(jnp.int32, (T, T), 1)
        tab = lambda b: bias_ref[b, h] * LOG2E
        visible = (key // CHUNK) <= (qry // CHUNK)
        bn_ref[0:T, :] = _rel_bias_tile(key - qry - T, tab)
        bn_ref[T:2 * T, :] = jnp.where(visible, _rel_bias_tile(key - qry, tab), NEG)

    if online:
        m_ref, = m_scratch
        m_ref[...] = jnp.full(m_ref.shape, NEG, F32)
    l_ref[...] = jnp.zeros(l_ref.shape, F32)
    acc_ref[...] = jnp.zeros(acc_ref.shape, F32)
    qt = qt_ref[0]
    dim = lax.broadcasted_iota(jnp.int32, qt.shape, 0)
    zero = jnp.zeros_like(qt)
    qc = (jnp.where(dim < HEAD_DIM, qt, zero), jnp.where(dim >= HEAD_DIM, qt, zero))

    def update(j, bias, width=T):
        start = pl.multiple_of(j * T, T)
        kt = k_ref[0, pl.ds(start, width), :]
        vt = vt_ref[0, :, pl.ds(start, width)]
        for c in range(2):
            s = _dot(kt, qc[c]) + bias
            if online:
                m_old = m_ref[c]
                m_new = jnp.maximum(m_old, jnp.max(s, axis=0, keepdims=True))
                alpha = jnp.exp2(m_old - m_new)
                p = jnp.exp2(s - m_new)
                l_ref[c] = alpha * l_ref[c] + _sublane_partial_sum(p)
                acc_ref[c] = alpha * acc_ref[c] + _dot(vt, p.astype(BF16))
                m_ref[c] = m_new
            else:
                p = jnp.exp2(s)
                l_ref[c] += _sublane_partial_sum(p)
                acc_ref[c] += _dot(vt, p.astype(BF16))

    far_bias = bias_ref[15, h] * LOG2E

    n_far = jnp.maximum(i - 1, 0)

    def far_body(jj, carry):
        update(FAR_TILES_PER_TRIP * jj, far_bias, width=FAR_TILES_PER_TRIP * T)
        return carry

    n_trips = n_far // FAR_TILES_PER_TRIP
    lax.fori_loop(0, n_trips, far_body, 0)
    done = n_trips * FAR_TILES_PER_TRIP

    @pl.when(n_far - done >= 2)
    def _():
        update(done, far_bias, width=2 * T)

    @pl.when((n_far - done) % 2 == 1)
    def _():
        update(n_far - 1, far_bias)

    @pl.when(i >= 1)
    def _():
        update(i - 1, bn_ref[...], width=2 * T)

    @pl.when(i == 0)
    def _():
        update(0, bn_ref[T:2 * T, :])

    o0 = acc_ref[0] / jnp.sum(l_ref[0], axis=0, keepdims=True)
    o1 = acc_ref[1] / jnp.sum(l_ref[1], axis=0, keepdims=True)
    o = o0 - lam_ref[0] * o1
    y = o * lax.rsqrt(jnp.mean(o * o, axis=0, keepdims=True) + EPS) * g_ref[...] * out_scale
    o_ref[...] = y.T.astype(o_ref.dtype)


def _attn_prompt(rel_bias, lam, qth, kh, vth, g_subln, out_scale, online):
    n_heads, Lp, _ = kh.shape
    T = ROW_TILE
    assert T % CHUNK == 0 and T >= FAR_DISTANCE and FAR_TILES_PER_TRIP == 4
    m_scratch = [pltpu.VMEM((2, 1, T), F32)] if online else []
    return pl.pallas_call(
        functools.partial(_attn_prompt_kernel, out_scale=out_scale),
        grid=(n_heads, Lp // T),
        in_specs=[
            pl.BlockSpec(memory_space=pltpu.SMEM),
            pl.BlockSpec(memory_space=pltpu.SMEM),
            pl.BlockSpec((1, V_DIM, T), lambda h, i: (h, 0, i)),
            pl.BlockSpec((1, Lp, V_DIM), lambda h, i: (h, 0, 0)),
            pl.BlockSpec((1, V_DIM, Lp), lambda h, i: (h, 0, 0)),
            pl.BlockSpec((V_DIM, 1), lambda h, i: (0, 0)),
        ],
        out_specs=pl.BlockSpec((T, V_DIM), lambda h, i: (i, h)),
        out_shape=jax.ShapeDtypeStruct((Lp, n_heads * V_DIM), BF16),
        scratch_shapes=[
            pltpu.VMEM((2 * T, T), F32),
            pltpu.VMEM((2, 8, T), F32),
            pltpu.VMEM((2, V_DIM, T), F32),
        ] + m_scratch,
        compiler_params=_params("arbitrary", "arbitrary"),
        name="attn_prompt_online" if online else "attn_prompt",
    )(rel_bias, lam, qth, kh, vth, g_subln)


def _attn_sample_kernel(lam_ref, q_ref, ckt_ref, cv_ref, kn_ref, vn_ref, tab_ref, g_ref,
                        o_ref, nearb_ref, newb_ref, qbd_ref, m_ref, l_ref, acc_ref, *, out_scale):
    b = pl.program_id(0)
    kb = pl.program_id(1)
    nkb = pl.num_programs(1)
    R, Tk = nearb_ref.shape
    L = kn_ref.shape[0]
    D = q_ref.shape[1]
    n_heads = D // V_DIM

    @pl.when((b == 0) & (kb == 0))
    def _():
        tab = lambda bkt: tab_ref[:, bkt:bkt + 1]
        t_near = lax.broadcasted_iota(jnp.int32, (R, Tk), 0) % L
        col = lax.broadcasted_iota(jnp.int32, (R, Tk), 1)
        nearb_ref[...] = _rel_bias_tile(col - Tk - t_near, tab)
        t_new = lax.broadcasted_iota(jnp.int32, (R, L), 0) % L
        col_new = lax.broadcasted_iota(jnp.int32, (R, L), 1)
        newb_ref[...] = _rel_bias_tile(col_new - t_new, tab)

    @pl.when(kb == 0)
    def _():
        m_ref[...] = jnp.full(m_ref.shape, NEG, F32)
        l_ref[...] = jnp.zeros(l_ref.shape, F32)
        acc_ref[...] = jnp.zeros(acc_ref.shape, F32)
        q_rows = jnp.concatenate([q_ref[...]] * (R // L), axis=0)
        row_group = lax.broadcasted_iota(jnp.int32, (R, D), 0) // L
        col_group = lax.broadcasted_iota(jnp.int32, (R, D), 1) // HEAD_DIM
        qbd_ref[...] = jnp.where(row_group == col_group, q_rows, jnp.zeros_like(q_rows))

    rows_per_head = 2 * L

    def update(s, value_of_head):
        m_old = m_ref[...]
        m_new = jnp.maximum(m_old, jnp.max(s, axis=-1, keepdims=True))
        alpha = jnp.exp2(m_old - m_new)
        p = jnp.exp2(s - m_new)
        l_ref[...] = alpha * l_ref[...] + jnp.sum(p, axis=-1, keepdims=True)
        pb = p.astype(BF16)
        for h in range(n_heads):
            rs = slice(h * rows_per_head, (h + 1) * rows_per_head)
            acc_ref[rs, :] = alpha[rs, :] * acc_ref[rs, :] + _dot(pb[rs, :], value_of_head(h))
        m_ref[...] = m_new

    qbd = qbd_ref[...]
    s = _dot(qbd, ckt_ref[0].astype(BF16))
    cache_value = lambda h: cv_ref[0, pl.ds(h, Tk, stride=n_heads), :].astype(BF16)

    @pl.when(kb < nkb - 1)
    def _():
        update(s + tab_ref[:, 15:16], cache_value)

    @pl.when(kb == nkb - 1)
    def _():
        update(s + nearb_ref[...], cache_value)
        s_new = _dot_nt(qbd, kn_ref[...].astype(BF16)) + newb_ref[...]
        update(s_new, lambda h: vn_ref[:, h * V_DIM:(h + 1) * V_DIM].astype(BF16))
        lam = lam_ref[0]
        o = acc_ref[...] / l_ref[...]
        for h in range(n_heads):
            r0 = h * rows_per_head
            o_ref[:, h * V_DIM:(h + 1) * V_DIM] = _diff_out(
                o[r0:r0 + L, :], o[r0 + L:r0 + 2 * L, :], lam, g_ref[...], out_scale).astype(o_ref.dtype)


def _attn_sample(lam, q, cache_k, cache_v, k_new, v_new, tab, g_subln, out_scale):
    B, past, n_heads = cache_v.shape[:3]
    D = n_heads * V_DIM
    L = k_new.shape[0] // B
    R = (D // HEAD_DIM) * L
    Tk = min(2048, past)
    assert past % Tk == 0 and Tk >= FAR_DISTANCE + L
    cache_kt = jnp.transpose(cache_k, (0, 2, 3, 4, 1)).reshape(B, D, past)
    return pl.pallas_call(
        functools.partial(_attn_sample_kernel, out_scale=out_scale),
        grid=(B, past // Tk),
        in_specs=[
            pl.BlockSpec(memory_space=pltpu.SMEM),
            pl.BlockSpec((L, D), lambda b, k: (b, 0)),
            pl.BlockSpec((1, D, Tk), lambda b, k: (b, 0, k)),
            pl.BlockSpec((1, Tk * n_heads, V_DIM), lambda b, k: (b, k, 0)),
            pl.BlockSpec((L, D), lambda b, k: (b, 0)),
            pl.BlockSpec((L, D), lambda b, k: (b, 0)),
            pl.BlockSpec(tab.shape, lambda b, k: (0, 0)),
            pl.BlockSpec((1, V_DIM), lambda b, k: (0, 0)),
        ],
        out_specs=pl.BlockSpec((L, D), lambda b, k: (b, 0)),
        out_shape=jax.ShapeDtypeStruct((B * L, D), BF16),
        scratch_shapes=[
            pltpu.VMEM((R, Tk), F32),
            pltpu.VMEM((R, L), F32),
            pltpu.VMEM((R, D), BF16),
            pltpu.VMEM((R, 1), F32),
            pltpu.VMEM((R, 1), F32),
            pltpu.VMEM((R, V_DIM), F32),
        ],
        compiler_params=_params("arbitrary", "arbitrary"),
        name="attn_sample",
    )(lam, q, cache_kt, cache_v.reshape(B, past * n_heads, V_DIM), k_new, v_new, tab, g_subln)


def _oproj_router_kernel(op_ref, os_ref, x_ref, wo_ref, g_ref, wrh_ref, wrl_ref,
                         x3_ref, h_ref, route_ref, route_t_ref):
    i = pl.program_id(0)
    nt = pl.num_programs(0) - 1
    o = jnp.where(i == nt, os_ref[...], op_ref[...])
    x3 = x_ref[...] + _dot(o, wo_ref[...])
    x3_ref[...] = x3
    h = _rms_unit(x3) * g_ref[...]
    h_hi = h.astype(BF16)
    h_ref[...] = h_hi
    h_lo = (h - h_hi.astype(F32)).astype(BF16)
    logits = _dot(h_hi, wrh_ref[...]) + (_dot(h_lo, wrh_ref[...]) + _dot(h_hi, wrl_ref[...]))
    lane = lax.broadcasted_iota(jnp.int32, logits.shape, 1)
    lg = jnp.where(lane < N_EXPERTS, logits, -jnp.inf)
    m1 = jnp.max(lg, axis=-1, keepdims=True)
    i1 = jnp.min(jnp.where(lg == m1, lane, LANE), axis=-1, keepdims=True)
    lg2 = jnp.where(lane == i1, -jnp.inf, lg)
    m2 = jnp.max(lg2, axis=-1, keepdims=True)
    i2 = jnp.min(jnp.where(lg2 == m2, lane, LANE), axis=-1, keepdims=True)
    e2 = jnp.exp(m2 - m1)
    den = 1.0 + e2
    g1 = 1.0 / den
    g2 = e2 / den
    route = jnp.where(lane == 0, i1.astype(F32),
                      jnp.where(lane == 1, i2.astype(F32),
                                jnp.where(lane == 2, g1,
                                          jnp.where(lane == 3, g2, 0.0))))
    route_ref[...] = route
    route_t_ref[...] = route.T


def _oproj_router(o_p, o_s, x, wo, g, wr_hi, wr_lo):
    N, D = x.shape
    T = ROW_TILE
    nt = N // T - 1
    const2 = lambda i: (0, 0)
    row = lambda i: (i, 0)
    return pl.pallas_call(
        _oproj_router_kernel,
        grid=(nt + 1,),
        in_specs=[
            pl.BlockSpec((T, D), lambda i: (jnp.minimum(i, nt - 1), 0)),
            pl.BlockSpec((T, D), const2),
            pl.BlockSpec((T, D), row),
            pl.BlockSpec((D, D), const2),
            pl.BlockSpec((1, D), const2),
            pl.BlockSpec((D, LANE), const2),
            pl.BlockSpec((D, LANE), const2),
        ],
        out_specs=[
            pl.BlockSpec((T, D), row),
            pl.BlockSpec((T, D), row),
            pl.BlockSpec((T, LANE), row),
            pl.BlockSpec((LANE, T), lambda i: (0, i)),
        ],
        out_shape=[
            jax.ShapeDtypeStruct((N, D), F32),
            jax.ShapeDtypeStruct((N, D), BF16),
            jax.ShapeDtypeStruct((N, LANE), F32),
            jax.ShapeDtypeStruct((LANE, N), F32),
        ],
        compiler_params=_params("arbitrary"),
        name="oproj_router",
    )(o_p, o_s, x, wo, g, wr_hi, wr_lo)


MOE_CHUNK = 128
MOE_ROW_ALIGN = 16


def _for_each_chunk(cnt_ref, step, n_chunks, fn, experts=range(N_EXPERTS)):
    for e in experts:
        for k in range(n_chunks):
            @pl.when(cnt_ref[step * N_EXPERTS + e] > k * MOE_CHUNK)
            def _(e=e, k=k):
                fn(e, k)


def _dispatch_kernel(off_ref, cnt_ref, hb_ref, rt_ref, utri_ref, xs_in_ref, xs_ref, stage_ref, sem):
    del xs_in_ref
    t = pl.program_id(0)
    T = hb_ref.shape[0]
    S = MOE_CHUNK
    nk = T // S

    def chunk_copy(step, e, k):
        start = pl.multiple_of(off_ref[step * N_EXPERTS + e] + k * S, MOE_ROW_ALIGN)
        slot = e * nk + k
        return pltpu.make_async_copy(stage_ref.at[slot], xs_ref.at[pl.ds(start, S), :], sem.at[slot])

    rt = rt_ref[...]
    expert_id = lax.broadcasted_iota(jnp.int32, (N_EXPERTS, T), 0).astype(F32)
    member = (rt[0:1, :] == expert_id) | (rt[1:2, :] == expert_id)
    rank = _dot(member.astype(BF16), utri_ref[...])
    rank = jnp.where(member, rank, 0.0)
    hb = hb_ref[...]
    row = lax.broadcasted_iota(jnp.int32, (S, 1), 0)

    def emit(e, k):
        sel = rank[e:e + 1, :] == (row + (k * S + 1)).astype(F32)
        stage_ref[e * nk + k] = _dot(sel.astype(BF16), hb).astype(BF16)
        chunk_copy(t, e, k).start()

    for e in range(N_EXPERTS):
        @pl.when(t > 0)
        def _(e=e):
            _for_each_chunk(cnt_ref, t - 1, nk, lambda e_, k: chunk_copy(t - 1, e_, k).wait(), experts=(e,))

        _for_each_chunk(cnt_ref, t, nk, emit, experts=(e,))

    @pl.when(t == pl.num_programs(0) - 1)
    def _():
        _for_each_chunk(cnt_ref, t, nk, lambda e, k: chunk_copy(t, e, k).wait())


def _dispatch(off, cnt, hb, route_t, utri, n_rows):
    N, D = hb.shape
    T = ROW_TILE
    nk = T // MOE_CHUNK
    n_slots = N_EXPERTS * nk
    xs_init = jnp.zeros((n_rows, D), BF16)
    return pl.pallas_call(
        _dispatch_kernel,
        grid_spec=pltpu.PrefetchScalarGridSpec(
            num_scalar_prefetch=2,
            grid=(N // T,),
            in_specs=[
                pl.BlockSpec((T, D), lambda t, off, cnt: (t, 0)),
                pl.BlockSpec((LANE, T), lambda t, off, cnt: (0, t)),
                pl.BlockSpec((T, T), lambda t, off, cnt: (0, 0)),
                pl.BlockSpec(memory_space=pl.ANY),
            ],
            out_specs=pl.BlockSpec(memory_space=pl.ANY),
            scratch_shapes=[
                pltpu.VMEM((n_slots, MOE_CHUNK, D), BF16),
                pltpu.SemaphoreType.DMA((n_slots,)),
            ],
        ),
        out_shape=jax.ShapeDtypeStruct((n_rows, D), BF16),
        input_output_aliases={5: 0},
        compiler_params=_params("arbitrary"),
        name="moe_dispatch",
    )(off, cnt, hb, route_t, utri, xs_init)


def _experts_kernel(te_ref, na_ref, xs_ref, wg_ref, wu_ref, wd_ref, out_ref, acc_ref):
    i = pl.program_id(0)
    f = pl.program_id(1)
    last = pl.num_programs(1) - 1
    active = i < na_ref[0]

    @pl.when(active)
    def _():
        @pl.when(f == 0)
        def _():
            acc_ref[...] = jnp.zeros(acc_ref.shape, F32)

        acc_ref[...] += _swiglu_tile(xs_ref[...], wg_ref[0], wu_ref[0], wd_ref[0])

        @pl.when(f == last)
        def _():
            out_ref[...] = acc_ref[...].astype(out_ref.dtype)

    @pl.when(jnp.logical_not(active) & (f == last))
    def _():
        out_ref[...] = jnp.zeros(out_ref.shape, out_ref.dtype)


def _experts(tile_expert, n_active, xs, wg, wu, wd):
    P, D = xs.shape
    T = ROW_TILE
    nf, tf = _ffn_splits(wg.shape[2])
    fidx = lambda i, f, na: jnp.where(i < na[0], f, nf - 1)
    return pl.pallas_call(
        _experts_kernel,
        grid_spec=pltpu.PrefetchScalarGridSpec(
            num_scalar_prefetch=2,
            grid=(P // T, nf),
            in_specs=[
                pl.BlockSpec((T, D), lambda i, f, te, na: (i, 0)),
                pl.BlockSpec((1, D, tf), lambda i, f, te, na: (te[i], 0, fidx(i, f, na))),
                pl.BlockSpec((1, D, tf), lambda i, f, te, na: (te[i], 0, fidx(i, f, na))),
                pl.BlockSpec((1, tf, D), lambda i, f, te, na: (te[i], fidx(i, f, na), 0)),
            ],
            out_specs=pl.BlockSpec((T, D), lambda i, f, te, na: (i, 0)),
            scratch_shapes=[pltpu.VMEM((T, D), F32)],
        ),
        out_shape=jax.ShapeDtypeStruct((P, D), BF16),
        compiler_params=_params("arbitrary", "arbitrary"),
        name="moe_experts",
    )(tile_expert, n_active, xs, wg, wu, wd)


def _combine_kernel(off_ref, cnt_ref, x_ref, route_ref, ltri_ref, ys_hbm, yp_ref, ysm_ref,
                    buf_ref, acc_ref, sem):
    t = pl.program_id(0)
    nt = pl.num_programs(0) - 1
    T = x_ref.shape[0]
    S = MOE_CHUNK
    nk = T // S

    def chunk_copy(step, e, k):
        start = pl.multiple_of(off_ref[step * N_EXPERTS + e] + k * S, MOE_ROW_ALIGN)
        bank, slot = step % 2, e * nk + k
        return pltpu.make_async_copy(ys_hbm.at[pl.ds(start, S), :], buf_ref.at[bank, slot], sem.at[bank, slot])

    @pl.when(t == 0)
    def _():
        _for_each_chunk(cnt_ref, t, nk, lambda e, k: chunk_copy(t, e, k).start())

    @pl.when(t < nt)
    def _():
        _for_each_chunk(cnt_ref, t + 1, nk, lambda e, k: chunk_copy(t + 1, e, k).start())

    route = route_ref[...]
    e1, e2, g1, g2 = (route[:, j:j + 1] for j in range(2 * TOP_K))
    lane = lax.broadcasted_iota(jnp.int32, route.shape, 1).astype(F32)
    member = (lane == e1) | (lane == e2)
    rank = _dot(ltri_ref[...], member.astype(BF16))
    rank = jnp.where(member, rank, 0.0)
    col = lax.broadcasted_iota(jnp.int32, (1, S), 1)
    acc_ref[...] = x_ref[...]

    def absorb(e, k):
        chunk_copy(t, e, k).wait()
        sel = rank[:, e:e + 1] == (col + (k * S + 1)).astype(F32)
        gate = jnp.where(e1 == e, g1, 0.0) + jnp.where(e2 == e, g2, 0.0)
        acc_ref[...] += gate * _dot(sel.astype(BF16), buf_ref[t % 2, e * nk + k])

    _for_each_chunk(cnt_ref, t, nk, absorb)

    @pl.when(t < nt)
    def _():
        yp_ref[...] = acc_ref[...]

    @pl.when(t == nt)
    def _():
        ysm_ref[...] = acc_ref[...]


def _combine(off, cnt, x3, route, ltri, ys, n_sample):
    N, D = x3.shape
    T = ROW_TILE
    nt = N // T - 1
    n_slots = N_EXPERTS * (T // MOE_CHUNK)
    return pl.pallas_call(
        _combine_kernel,
        grid_spec=pltpu.PrefetchScalarGridSpec(
            num_scalar_prefetch=2,
            grid=(nt + 1,),
            in_specs=[
                pl.BlockSpec((T, D), lambda t, off, cnt: (t, 0)),
                pl.BlockSpec((T, LANE), lambda t, off, cnt: (t, 0)),
                pl.BlockSpec((T, T), lambda t, off, cnt: (0, 0)),
                pl.BlockSpec(memory_space=pl.ANY),
            ],
            out_specs=[
                pl.BlockSpec((T, D), lambda t, off, cnt: (jnp.minimum(t, nt - 1), 0)),
                pl.BlockSpec((T, D), lambda t, off, cnt: (0, 0)),
            ],
            scratch_shapes=[
                pltpu.VMEM((2, n_slots, MOE_CHUNK, D), BF16),
                pltpu.VMEM((T, D), F32),
                pltpu.SemaphoreType.DMA((2, n_slots)),
            ],
        ),
        out_shape=[
            jax.ShapeDtypeStruct((N - n_sample, D), F32),
            jax.ShapeDtypeStruct((n_sample, D), F32),
        ],
        compiler_params=_params("arbitrary"),
        name="moe_combine",
    )(off, cnt, x3, route, ltri, ys)


def _routing_tables(route, tile):
    n = route.shape[0]
    n_tok_tiles = n // tile
    experts = route[:, :TOP_K].astype(jnp.int32)
    onehot = (experts[:, :, None] == jnp.arange(N_EXPERTS)[None, None, :]).astype(jnp.int32).sum(axis=1)
    cnt = onehot.reshape(n_tok_tiles, tile, N_EXPERTS).sum(axis=1)
    span = (cnt + MOE_ROW_ALIGN - 1) // MOE_ROW_ALIGN * MOE_ROW_ALIGN
    totals = span.sum(axis=0)
    tiles_per = (totals + MOE_CHUNK + tile - 1) // tile
    tile_end = jnp.cumsum(tiles_per)
    group_off = (tile_end - tiles_per) * tile
    off = group_off[None, :] + jnp.cumsum(span, axis=0) - span
    max_rows = n * TOP_K + N_EXPERTS * (MOE_CHUNK + n_tok_tiles * (MOE_ROW_ALIGN - 1))
    n_row_tiles = max_rows // tile + N_EXPERTS
    n_active = tile_end[-1:].astype(jnp.int32)
    tile_ids = jnp.minimum(jnp.arange(n_row_tiles, dtype=jnp.int32), n_active[0] - 1)
    tile_expert = jnp.sum((tile_ids[:, None] >= tile_end[None, :]).astype(jnp.int32), axis=1)
    return (off.reshape(-1).astype(jnp.int32), cnt.reshape(-1).astype(jnp.int32),
            tile_expert, n_active, n_row_tiles * tile)


def _group_matrices(d_model):
    n_groups = d_model // HEAD_DIM
    gsum = np.zeros((d_model, LANE), np.float32)
    gsum[np.arange(d_model), np.arange(d_model) // HEAD_DIM] = 1.0
    assert n_groups <= LANE
    return jnp.asarray(gsum, BF16), jnp.asarray(gsum.T.copy(), BF16)


def kernel(x_prompt, x_sample, cache_k, cache_v, state_pool, g_pool_norm, w_pool, pool_scale, g_attn, w_q, g_qn, lambda_q1, lambda_k1, lambda_q2, lambda_k2, g_subln, w_o, g_kv, w_k, w_v, g_kn, rel_bias, g_ffn, w_gate_dense, w_up_dense, w_down_dense, w_router, w_gate_moe, w_up_moe, w_down_moe):
    Bp, Lp, D = x_prompt.shape
    Bs, Ls, _ = x_sample.shape
    past = cache_k.shape[1]
    n_heads = D // V_DIM
    n_sample = Bs * Ls
    assert Bp == 1 and n_sample == ROW_TILE and Lp % ROW_TILE == 0
    assert g_pool_norm.shape[0] == 1 and g_attn.shape[0] == 1
    bf = lambda a: a.astype(BF16)
    row = lambda a: a.reshape(1, -1)

    x1, pool_p, pool_s = _pool_layer(
        x_prompt.reshape(Lp, D), x_sample.reshape(n_sample, D), state_pool[0],
        row(g_pool_norm[0]), bf(w_pool[0]), row(pool_scale[0]), past)
    x2 = _dense_ffn(x1, row(g_ffn[0]), bf(w_gate_dense[0]), bf(w_up_dense[0]), bf(w_down_dense[0]))

    layer = 1
    lambda_init = 0.8 - 0.6 * math.exp(-0.3 * layer)
    lam = (jnp.exp(jnp.sum(lambda_q1[0] * lambda_k1[0])) - jnp.exp(jnp.sum(lambda_q2[0] * lambda_k2[0]))
           + lambda_init).reshape(1)
    gsum, gbc = _group_matrices(D)
    n_groups = D // HEAD_DIM
    k_p, v_p, k_s, v_s, qh, kh, vh, q_s = _qkv_proj(
        x2, row(g_kv), row(g_attn[0]), bf(w_k), bf(w_v), bf(w_q[0]),
        row(jnp.tile(g_kn, n_groups)), row(jnp.tile(g_qn[0], n_groups) * (HEAD_DIM ** -0.5 * LOG2E)),
        gsum, gbc, n_sample)
    out_scale = 1.0 - lambda_init
    g_sub = row(g_subln[0])
    score_bound = LOG2E * (math.sqrt(HEAD_DIM) * jnp.max(jnp.abs(g_qn[0])) * jnp.max(jnp.abs(g_kn))
                           + jnp.max(jnp.abs(rel_bias)))
    o_p = lax.cond(
        score_bound <= UNSHIFTED_SCORE_LIMIT,
        functools.partial(_attn_prompt, out_scale=out_scale, online=False),
        functools.partial(_attn_prompt, out_scale=out_scale, online=True),
        rel_bias, lam, qh, kh, vh, g_subln[0].reshape(V_DIM, 1))

    tab = jnp.repeat(rel_bias.T, 2 * Ls, axis=0) * LOG2E
    o_s = _attn_sample(lam, q_s, cache_k, cache_v, k_s, v_s, tab, g_sub, out_scale)

    wr = jnp.pad(w_router[0], ((0, 0), (0, LANE - N_EXPERTS)))
    wr_hi = bf(wr)
    wr_lo = bf(wr - wr_hi.astype(F32))
    x3, h_moe, route, route_t = _oproj_router(o_p, o_s, x2, bf(w_o[0]), row(g_ffn[1]), wr_hi, wr_lo)
    off, cnt, tile_expert, n_active, n_rows = _routing_tables(route, ROW_TILE)
    ltri = jnp.asarray(np.tril(np.ones((ROW_TILE, ROW_TILE), np.float32)), BF16)
    xs = _dispatch(off, cnt, h_moe, route_t, ltri.T, n_rows)
    ys = _experts(tile_expert, n_active, xs, bf(w_gate_moe[0]), bf(w_up_moe[0]), bf(w_down_moe[0]))
    y_p, y_s = _combine(off, cnt, x3, route, ltri, ys, n_sample)

    return (y_p.reshape(Bp, Lp, D), y_s.reshape(Bs, Ls, D),
            k_p.reshape(Bp, Lp, n_heads, 2, HEAD_DIM), v_p.reshape(Bp, Lp, n_heads, V_DIM),
            pool_p.reshape(1, Bp, POOL_STATE, D),
            k_s.reshape(Bs, Ls, n_heads, 2, HEAD_DIM), v_s.reshape(Bs, Ls, n_heads, V_DIM),
            pool_s.reshape(1, Bs, POOL_STATE, D))
```

```python
import functools
import math

import numpy as np
import jax
import jax.numpy as jnp
from jax import lax
from jax.experimental import pallas as pl
from jax.experimental.pallas import tpu as pltpu

EPS = 1e-6
CHUNK = 64
POOL_WINDOWS = (2, 4, 8, 16)
POOL_STATE = max(POOL_WINDOWS) - 1
HEAD_DIM = 64
V_DIM = 2 * HEAD_DIM
N_EXPERTS = 8
TOP_K = 2
MAX_EXACT = 8
BUCKET_UPPER = (1, 2, 3, 4, 5, 6, 7, 8, 12, 16, 23, 32, 46, 64, 91)
FAR_DISTANCE = 128
NEG = -1e30
LOG2E = math.log2(math.e)
UNSHIFTED_SCORE_LIMIT = 80.0

ROW_TILE = 512
FAR_TILES_PER_TRIP = 4
LANE = 128
MXU_DIM = 256
VMEM_LIMIT = 56 * 1024 * 1024

F32 = jnp.float32
BF16 = jnp.bfloat16


def _dot(a, b):
    return jnp.dot(a, b, preferred_element_type=F32)


def _dot_nt(a, b):
    return lax.dot_general(a, b, (((1,), (1,)), ((), ())), preferred_element_type=F32)


def _rms_unit(x):
    return x * lax.rsqrt(jnp.mean(x * x, axis=-1, keepdims=True) + EPS)


def _rel_bias_tile(rel, tab):
    n = jnp.abs(rel)
    neg = tab(15)
    pos = tab(31)
    for b in range(14, -1, -1):
        lt = n < BUCKET_UPPER[b]
        neg = jnp.where(lt, tab(b), neg)
        pos = jnp.where(lt, tab(16 + b), pos)
    return jnp.where(rel > 0, pos, neg)


def _params(*sem):
    return pltpu.CompilerParams(dimension_semantics=sem, vmem_limit_bytes=VMEM_LIMIT)


def _pool_kernel(xp_ref, xs_ref, st_ref, g_ref, w_ref, sc_ref,
                 x1_ref, pp_ref, ps_ref, ext_ref, ext3_ref, *, past_len):
    i = pl.program_id(0)
    nt = pl.num_programs(0) - 1
    T, D = xp_ref.shape
    gw = D // len(POOL_WINDOWS)

    @pl.when(i < nt)
    def _prompt():
        x = xp_ref[...]
        h = _rms_unit(x) * g_ref[...]

        @pl.when(i == 0)
        def _():
            ext_ref[0:16, :] = jnp.zeros((16, D), F32)

        ext_ref[16:16 + T, :] = h
        row = i * T + lax.broadcasted_iota(jnp.int32, (T, 1), 0)
        parts = []
        for gi, w in enumerate(POOL_WINDOWS):
            c0 = gi * gw
            s = ext_ref[16:16 + T, c0:c0 + gw]
            for j in range(1, w):
                s = s + ext_ref[16 - j:16 - j + T, c0:c0 + gw]
            cnt = jnp.minimum(w, row + 1).astype(F32)
            pooled = s / cnt - h[:, c0:c0 + gw]
            parts.append(_dot(pooled.astype(BF16), w_ref[gi]))
        mix = jnp.concatenate(parts, axis=-1) * sc_ref[...]
        x1_ref[...] = x + mix
        tail = ext_ref[T:T + 16, :]
        ext_ref[0:16, :] = tail

        @pl.when(i == nt - 1)
        def _():
            pp_ref[...] = tail[1:16, :]

    @pl.when(i == nt)
    def _sample():
        B = st_ref.shape[0]
        L = T // B
        x = xs_ref[...]
        h = _rms_unit(x) * g_ref[...]
        ext3_ref[:, 1:16, :] = st_ref[...]
        ext3_ref[:, 16:16 + L, :] = h.reshape(B, L, D)
        t = lax.broadcasted_iota(jnp.int32, (1, L, 1), 1)
        parts = []
        for gi, w in enumerate(POOL_WINDOWS):
            c0 = gi * gw
            s = ext3_ref[:, 16:16 + L, c0:c0 + gw]
            for j in range(1, w):
                s = s + ext3_ref[:, 16 - j:16 - j + L, c0:c0 + gw]
            cnt = jnp.minimum(w, past_len + t + 1).astype(F32)
            pooled = (s / cnt).reshape(T, gw) - h[:, c0:c0 + gw]
            parts.append(_dot(pooled.astype(BF16), w_ref[gi]))
        mix = jnp.concatenate(parts, axis=-1) * sc_ref[...]
        x1_ref[...] = x + mix
        ps_ref[...] = ext3_ref[:, 16 + L - POOL_STATE:16 + L, :]


def _pool_layer(xp, xs, state, g, w, sc, past_len):
    Lp, D = xp.shape
    T = ROW_TILE
    nt = Lp // T
    B = state.shape[0]
    L = xs.shape[0] // B
    return pl.pallas_call(
        functools.partial(_pool_kernel, past_len=past_len),
        grid=(nt + 1,),
        in_specs=[
            pl.BlockSpec((T, D), lambda i: (jnp.minimum(i, nt - 1), 0)),
            pl.BlockSpec((T, D), lambda i: (0, 0)),
            pl.BlockSpec((B, POOL_STATE, D), lambda i: (0, 0, 0)),
            pl.BlockSpec((1, D), lambda i: (0, 0)),
            pl.BlockSpec(w.shape, lambda i: (0, 0, 0)),
            pl.BlockSpec((1, D), lambda i: (0, 0)),
        ],
        out_specs=[
            pl.BlockSpec((T, D), lambda i: (i, 0)),
            pl.BlockSpec((POOL_STATE, D), lambda i: (0, 0)),
            pl.BlockSpec((B, POOL_STATE, D), lambda i: (0, 0, 0)),
        ],
        out_shape=[
            jax.ShapeDtypeStruct((Lp + T, D), F32),
            jax.ShapeDtypeStruct((POOL_STATE, D), F32),
            jax.ShapeDtypeStruct((B, POOL_STATE, D), F32),
        ],
        scratch_shapes=[
            pltpu.VMEM((16 + T, D), F32),
            pltpu.VMEM((B, 16 + L, D), F32),
        ],
        compiler_params=_params("arbitrary"),
        name="pool_mixer",
    )(xp, xs, state, g, w, sc)


def _swiglu_tile(hb, wg_ref, wu_ref, wd_ref):
    tf = wg_ref.shape[1]
    step = 3 * MXU_DIM
    out = None
    for c0 in range(0, tf, step):
        c1 = min(c0 + step, tf)
        gt = _dot(hb, wg_ref[:, c0:c1])
        ut = _dot(hb, wu_ref[:, c0:c1])
        a = gt * jax.nn.sigmoid(gt) * ut
        part = _dot(a.astype(BF16), wd_ref[c0:c1, :])
        out = part if out is None else out + part
    return out


def _ffn_kernel(x_ref, g_ref, wg_ref, wu_ref, wd_ref, out_ref, hb_ref, acc_ref):
    f = pl.program_id(1)

    @pl.when(f == 0)
    def _():
        x = x_ref[...]
        hb_ref[...] = (_rms_unit(x) * g_ref[...]).astype(BF16)
        acc_ref[...] = x

    acc_ref[...] += _swiglu_tile(hb_ref[...], wg_ref, wu_ref, wd_ref)

    @pl.when(f == pl.num_programs(1) - 1)
    def _():
        out_ref[...] = acc_ref[...]


def _ffn_splits(d_ff):
    nf = 1
    return nf, d_ff // nf


def _dense_ffn(x, g, wg, wu, wd):
    N, D = x.shape
    T = ROW_TILE
    nf, tf = _ffn_splits(wg.shape[1])
    return pl.pallas_call(
        _ffn_kernel,
        grid=(N // T, nf),
        in_specs=[
            pl.BlockSpec((T, D), lambda i, f: (i, 0)),
            pl.BlockSpec((1, D), lambda i, f: (0, 0)),
            pl.BlockSpec((D, tf), lambda i, f: (0, f)),
            pl.BlockSpec((D, tf), lambda i, f: (0, f)),
            pl.BlockSpec((tf, D), lambda i, f: (f, 0)),
        ],
        out_specs=pl.BlockSpec((T, D), lambda i, f: (i, 0)),
        out_shape=jax.ShapeDtypeStruct((N, D), F32),
        scratch_shapes=[pltpu.VMEM((T, D), BF16), pltpu.VMEM((T, D), F32)],
        compiler_params=_params("arbitrary", "arbitrary"),
        name="dense_swiglu",
    )(x, g, wg, wu, wd)


def _qkv_kernel(x_ref, gkv_ref, gq_ref, wk_ref, wv_ref, wq_ref, gkn_ref, gqn_ref,
                gsum_ref, gbc_ref,
                kp_ref, vp_ref, ks_ref, vs_ref, qh_ref, kh_ref, vh_ref, qs_ref):
    i = pl.program_id(0)
    nt = pl.num_programs(0) - 1
    n_heads = qh_ref.shape[0]

    xn = _rms_unit(x_ref[...])
    hkv = (xn * gkv_ref[...]).astype(BF16)
    hq = (xn * gq_ref[...]).astype(BF16)

    def head_norm(y, g):
        ssq = _dot((y * y).astype(BF16), gsum_ref[...])
        rs = lax.rsqrt(ssq * (1.0 / HEAD_DIM) + EPS)
        rs_hi = rs.astype(BF16)
        rs_lo = (rs - rs_hi.astype(F32)).astype(BF16)
        rsb = _dot(rs_hi, gbc_ref[...]) + _dot(rs_lo, gbc_ref[...])
        return y * rsb * g

    k = head_norm(_dot(hkv, wk_ref[...]), gkn_ref[...])
    v = _dot(hkv, wv_ref[...])
    q = head_norm(_dot(hq, wq_ref[...]), gqn_ref[...])

    @pl.when(i < nt)
    def _():
        kp_ref[...] = k
        vp_ref[...] = v
        for h in range(n_heads):
            sl = slice(h * V_DIM, (h + 1) * V_DIM)
            qh_ref[h] = q[:, sl].T.astype(BF16)
            kh_ref[h] = k[:, sl].astype(BF16)
            vh_ref[h] = v[:, sl].T.astype(BF16)

    @pl.when(i == nt)
    def _():
        ks_ref[...] = k
        vs_ref[...] = v
        qs_ref[...] = q.astype(BF16)


def _qkv_proj(x, gkv, gq, wk, wv, wq, gkn_t, gqn_t, gsum, gbc, n_sample):
    N, D = x.shape
    T = ROW_TILE
    nt = N // T - 1
    Lp = N - n_sample
    n_heads = D // V_DIM
    const2 = lambda i: (0, 0)
    prow = lambda i: (jnp.minimum(i, nt - 1), 0)
    phead = lambda i: (0, jnp.minimum(i, nt - 1), 0)
    pheadt = lambda i: (0, 0, jnp.minimum(i, nt - 1))
    return pl.pallas_call(
        _qkv_kernel,
        grid=(nt + 1,),
        in_specs=[
            pl.BlockSpec((T, D), lambda i: (i, 0)),
            pl.BlockSpec((1, D), const2),
            pl.BlockSpec((1, D), const2),
            pl.BlockSpec((D, D), const2),
            pl.BlockSpec((D, D), const2),
            pl.BlockSpec((D, D), const2),
            pl.BlockSpec((1, D), const2),
            pl.BlockSpec((1, D), const2),
            pl.BlockSpec(gsum.shape, const2),
            pl.BlockSpec(gbc.shape, const2),
        ],
        out_specs=[
            pl.BlockSpec((T, D), prow),
            pl.BlockSpec((T, D), prow),
            pl.BlockSpec((T, D), const2),
            pl.BlockSpec((T, D), const2),
            pl.BlockSpec((n_heads, V_DIM, T), pheadt),
            pl.BlockSpec((n_heads, T, V_DIM), phead),
            pl.BlockSpec((n_heads, V_DIM, T), pheadt),
            pl.BlockSpec((T, D), const2),
        ],
        out_shape=[
            jax.ShapeDtypeStruct((Lp, D), F32),
            jax.ShapeDtypeStruct((Lp, D), F32),
            jax.ShapeDtypeStruct((n_sample, D), F32),
            jax.ShapeDtypeStruct((n_sample, D), F32),
            jax.ShapeDtypeStruct((n_heads, V_DIM, Lp), BF16),
            jax.ShapeDtypeStruct((n_heads, Lp, V_DIM), BF16),
            jax.ShapeDtypeStruct((n_heads, V_DIM, Lp), BF16),
            jax.ShapeDtypeStruct((n_sample, D), BF16),
        ],
        compiler_params=_params("arbitrary"),
        name="qkv_proj",
    )(x, gkv, gq, wk, wv, wq, gkn_t, gqn_t, gsum, gbc)


def _diff_out(o0, o1, lam, g, out_scale):
    o = o0 - lam * o1
    return _rms_unit(o) * g * out_scale


def _sublane_partial_sum(p):
    return jnp.sum(p.reshape(p.shape[0] // 8, 8, p.shape[1]), axis=0)


def _attn_prompt_kernel(bias_ref, lam_ref, qt_ref, k_ref, vt_ref, g_ref, o_ref,
                        bn_ref, l_ref, acc_ref, *m_scratch, out_scale):
    h = pl.program_id(0)
    i = pl.program_id(1)
    T = qt_ref.shape[2]
    online = bool(m_scratch)

    @pl.when(i == 0)
    def _():
        key = lax.broadcasted_iota(jnp.int32, (T, T), 0)
        qry = lax.broadcasted_iota(jnp.int32, (T, T), 1)
        tab = lambda b: bias_ref[b, h] * LOG2E
        visible = (key // CHUNK) <= (qry // CHUNK)
        bn_ref[0:T, :] = _rel_bias_tile(key - qry - T, tab)
        bn_ref[T:2 * T, :] = jnp.where(visible, _rel_bias_tile(key - qry, tab), NEG)

    if online:
        m_ref, = m_scratch
        m_ref[...] = jnp.full(m_ref.shape, NEG, F32)
    l_ref[...] = jnp.zeros(l_ref.shape, F32)
    acc_ref[...] = jnp.zeros(acc_ref.shape, F32)
    qt = qt_ref[0]
    dim = lax.broadcasted_iota(jnp.int32, qt.shape, 0)
    zero = jnp.zeros_like(qt)
    qc = (jnp.where(dim < HEAD_DIM, qt, zero), jnp.where(dim >= HEAD_DIM, qt, zero))

    def update(j, bias, width=T):
        start = pl.multiple_of(j * T, T)
        kt = k_ref[0, pl.ds(start, width), :]
        vt = vt_ref[0, :, pl.ds(start, width)]
        for c in range(2):
            s = _dot(kt, qc[c]) + bias
            if online:
                m_old = m_ref[c]
                m_new = jnp.maximum(m_old, jnp.max(s, axis=0, keepdims=True))
                alpha = jnp.exp2(m_old - m_new)
                p = jnp.exp2(s - m_new)
                l_ref[c] = alpha * l_ref[c] + _sublane_partial_sum(p)
                acc_ref[c] = alpha * acc_ref[c] + _dot(vt, p.astype(BF16))
                m_ref[c] = m_new
            else:
                p = jnp.exp2(s)
                l_ref[c] += _sublane_partial_sum(p)
                acc_ref[c] += _dot(vt, p.astype(BF16))

    far_bias = bias_ref[15, h] * LOG2E

    n_far = jnp.maximum(i - 1, 0)

    def far_body(jj, carry):
        update(FAR_TILES_PER_TRIP * jj, far_bias, width=FAR_TILES_PER_TRIP * T)
        return carry

    n_trips = n_far // FAR_TILES_PER_TRIP
    lax.fori_loop(0, n_trips, far_body, 0)
    done = n_trips * FAR_TILES_PER_TRIP

    @pl.when(n_far - done >= 2)
    def _():
        update(done, far_bias, width=2 * T)

    @pl.when((n_far - done) % 2 == 1)
    def _():
        update(n_far - 1, far_bias)

    @pl.when(i >= 1)
    def _():
        update(i - 1, bn_ref[...], width=2 * T)

    @pl.when(i == 0)
    def _():
        update(0, bn_ref[T:2 * T, :])

    o0 = acc_ref[0] / jnp.sum(l_ref[0], axis=0, keepdims=True)
    o1 = acc_ref[1] / jnp.sum(l_ref[1], axis=0, keepdims=True)
    o = o0 - lam_ref[0] * o1
    y = o * lax.rsqrt(jnp.mean(o * o, axis=0, keepdims=True) + EPS) * g_ref[...] * out_scale
    o_ref[...] = y.T.astype(o_ref.dtype)


def _attn_prompt(rel_bias, lam, qth, kh, vth, g_subln, out_scale, online):
    n_heads, Lp, _ = kh.shape
    T = ROW_TILE
    assert T % CHUNK == 0 and T >= FAR_DISTANCE and FAR_TILES_PER_TRIP == 4
    m_scratch = [pltpu.VMEM((2, 1, T), F32)] if online else []
    return pl.pallas_call(
        functools.partial(_attn_prompt_kernel, out_scale=out_scale),
        grid=(n_heads, Lp // T),
        in_specs=[
            pl.BlockSpec(memory_space=pltpu.SMEM),
            pl.BlockSpec(memory_space=pltpu.SMEM),
            pl.BlockSpec((1, V_DIM, T), lambda h, i: (h, 0, i)),
            pl.BlockSpec((1, Lp, V_DIM), lambda h, i: (h, 0, 0)),
            pl.BlockSpec((1, V_DIM, Lp), lambda h, i: (h, 0, 0)),
            pl.BlockSpec((V_DIM, 1), lambda h, i: (0, 0)),
        ],
        out_specs=pl.BlockSpec((T, V_DIM), lambda h, i: (i, h)),
        out_shape=jax.ShapeDtypeStruct((Lp, n_heads * V_DIM), BF16),
        scratch_shapes=[
            pltpu.VMEM((2 * T, T), F32),
            pltpu.VMEM((2, 8, T), F32),
            pltpu.VMEM((2, V_DIM, T), F32),
        ] + m_scratch,
        compiler_params=_params("arbitrary", "arbitrary"),
        name="attn_prompt_online" if online else "attn_prompt",
    )(rel_bias, lam, qth, kh, vth, g_subln)


def _attn_sample_kernel(lam_ref, q_ref, ckt_ref, cv_ref, kn_ref, vn_ref, tab_ref, g_ref,
                        o_ref, nearb_ref, newb_ref, qbd_ref, m_ref, l_ref, acc_ref, *, out_scale):
    b = pl.program_id(0)
    kb = pl.program_id(1)
    nkb = pl.num_programs(1)
    R, Tk = nearb_ref.shape
    L = kn_ref.shape[0]
    D = q_ref.shape[1]
    n_heads = D // V_DIM

    @pl.when((b == 0) & (kb == 0))
    def _():
        tab = lambda bkt: tab_ref[:, bkt:bkt + 1]
        t_near = lax.broadcasted_iota(jnp.int32, (R, Tk), 0) % L
        col = lax.broadcasted_iota(jnp.int32, (R, Tk), 1)
        nearb_ref[...] = _rel_bias_tile(col - Tk - t_near, tab)
        t_new = lax.broadcasted_iota(jnp.int32, (R, L), 0) % L
        col_new = lax.broadcasted_iota(jnp.int32, (R, L), 1)
        newb_ref[...] = _rel_bias_tile(col_new - t_new, tab)

    @pl.when(kb == 0)
    def _():
        m_ref[...] = jnp.full(m_ref.shape, NEG, F32)
        l_ref[...] = jnp.zeros(l_ref.shape, F32)
        acc_ref[...] = jnp.zeros(acc_ref.shape, F32)
        q_rows = jnp.concatenate([q_ref[...]] * (R // L), axis=0)
        row_group = lax.broadcasted_iota(jnp.int32, (R, D), 0) // L
        col_group = lax.broadcasted_iota(jnp.int32, (R, D), 1) // HEAD_DIM
        qbd_ref[...] = jnp.where(row_group == col_group, q_rows, jnp.zeros_like(q_rows))

    rows_per_head = 2 * L

    def update(s, value_of_head):
        m_old = m_ref[...]
        m_new = jnp.maximum(m_old, jnp.max(s, axis=-1, keepdims=True))
        alpha = jnp.exp2(m_old - m_new)
        p = jnp.exp2(s - m_new)
        l_ref[...] = alpha * l_ref[...] + jnp.sum(p, axis=-1, keepdims=True)
        pb = p.astype(BF16)
        for h in range(n_heads):
            rs = slice(h * rows_per_head, (h + 1) * rows_per_head)
            acc_ref[rs, :] = alpha[rs, :] * acc_ref[rs, :] + _dot(pb[rs, :], value_of_head(h))
        m_ref[...] = m_new

    qbd = qbd_ref[...]
    s = _dot(qbd, ckt_ref[0].astype(BF16))
    cache_value = lambda h: cv_ref[0, pl.ds(h, Tk, stride=n_heads), :].astype(BF16)

    @pl.when(kb < nkb - 1)
    def _():
        update(s + tab_ref[:, 15:16], cache_value)

    @pl.when(kb == nkb - 1)
    def _():
        update(s + nearb_ref[...], cache_value)
        s_new = _dot_nt(qbd, kn_ref[...].astype(BF16)) + newb_ref[...]
        update(s_new, lambda h: vn_ref[:, h * V_DIM:(h + 1) * V_DIM].astype(BF16))
        lam = lam_ref[0]
        o = acc_ref[...] / l_ref[...]
        for h in range(n_heads):
            r0 = h * rows_per_head
            o_ref[:, h * V_DIM:(h + 1) * V_DIM] = _diff_out(
                o[r0:r0 + L, :], o[r0 + L:r0 + 2 * L, :], lam, g_ref[...], out_scale).astype(o_ref.dtype)


def _attn_sample(lam, q, cache_k, cache_v, k_new, v_new, tab, g_subln, out_scale):
    B, past, n_heads = cache_v.shape[:3]
    D = n_heads * V_DIM
    L = k_new.shape[0] // B
    R = (D // HEAD_DIM) * L
    Tk = min(2048, past)
    assert past % Tk == 0 and Tk >= FAR_DISTANCE + L
    cache_kt = jnp.transpose(cache_k, (0, 2, 3, 4, 1)).reshape(B, D, past)
    return pl.pallas_call(
        functools.partial(_attn_sample_kernel, out_scale=out_scale),
        grid=(B, past // Tk),
        in_specs=[
            pl.BlockSpec(memory_space=pltpu.SMEM),
            pl.BlockSpec((L, D), lambda b, k: (b, 0)),
            pl.BlockSpec((1, D, Tk), lambda b, k: (b, 0, k)),
            pl.BlockSpec((1, Tk * n_heads, V_DIM), lambda b, k: (b, k, 0)),
            pl.BlockSpec((L, D), lambda b, k: (b, 0)),
            pl.BlockSpec((L, D), lambda b, k: (b, 0)),
            pl.BlockSpec(tab.shape, lambda b, k: (0, 0)),
            pl.BlockSpec((1, V_DIM), lambda b, k: (0, 0)),
        ],
        out_specs=pl.BlockSpec((L, D), lambda b, k: (b, 0)),
        out_shape=jax.ShapeDtypeStruct((B * L, D), BF16),
        scratch_shapes=[
            pltpu.VMEM((R, Tk), F32),
            pltpu.VMEM((R, L), F32),
            pltpu.VMEM((R, D), BF16),
            pltpu.VMEM((R, 1), F32),
            pltpu.VMEM((R, 1), F32),
            pltpu.VMEM((R, V_DIM), F32),
        ],
        compiler_params=_params("arbitrary", "arbitrary"),
        name="attn_sample",
    )(lam, q, cache_kt, cache_v.reshape(B, past * n_heads, V_DIM), k_new, v_new, tab, g_subln)


def _oproj_router_kernel(op_ref, os_ref, x_ref, wo_ref, g_ref, wrh_ref, wrl_ref,
                         x3_ref, h_ref, route_ref, route_t_ref):
    i = pl.program_id(0)
    nt = pl.num_programs(0) - 1
    o = jnp.where(i == nt, os_ref[...], op_ref[...])
    x3 = x_ref[...] + _dot(o, wo_ref[...])
    x3_ref[...] = x3
    h = _rms_unit(x3) * g_ref[...]
    h_hi = h.astype(BF16)
    h_ref[...] = h_hi
    h_lo = (h - h_hi.astype(F32)).astype(BF16)
    logits = _dot(h_hi, wrh_ref[...]) + (_dot(h_lo, wrh_ref[...]) + _dot(h_hi, wrl_ref[...]))
    lane = lax.broadcasted_iota(jnp.int32, logits.shape, 1)
    lg = jnp.where(lane < N_EXPERTS, logits, -jnp.inf)
    m1 = jnp.max(lg, axis=-1, keepdims=True)
    i1 = jnp.min(jnp.where(lg == m1, lane, LANE), axis=-1, keepdims=True)
    lg2 = jnp.where(lane == i1, -jnp.inf, lg)
    m2 = jnp.max(lg2, axis=-1, keepdims=True)
    i2 = jnp.min(jnp.where(lg2 == m2, lane, LANE), axis=-1, keepdims=True)
    e2 = jnp.exp(m2 - m1)
    den = 1.0 + e2
    g1 = 1.0 / den
    g2 = e2 / den
    route = jnp.where(lane == 0, i1.astype(F32),
                      jnp.where(lane == 1, i2.astype(F32),
                                jnp.where(lane == 2, g1,
                                          jnp.where(lane == 3, g2, 0.0))))
    route_ref[...] = route
    route_t_ref[...] = route.T


def _oproj_router(o_p, o_s, x, wo, g, wr_hi, wr_lo):
    N, D = x.shape
    T = ROW_TILE
    nt = N // T - 1
    const2 = lambda i: (0, 0)
    row = lambda i: (i, 0)
    return pl.pallas_call(
        _oproj_router_kernel,
        grid=(nt + 1,),
        in_specs=[
            pl.BlockSpec((T, D), lambda i: (jnp.minimum(i, nt - 1), 0)),
            pl.BlockSpec((T, D), const2),
            pl.BlockSpec((T, D), row),
            pl.BlockSpec((D, D), const2),
            pl.BlockSpec((1, D), const2),
            pl.BlockSpec((D, LANE), const2),
            pl.BlockSpec((D, LANE), const2),
        ],
        out_specs=[
            pl.BlockSpec((T, D), row),
            pl.BlockSpec((T, D), row),
            pl.BlockSpec((T, LANE), row),
            pl.BlockSpec((LANE, T), lambda i: (0, i)),
        ],
        out_shape=[
            jax.ShapeDtypeStruct((N, D), F32),
            jax.ShapeDtypeStruct((N, D), BF16),
            jax.ShapeDtypeStruct((N, LANE), F32),
            jax.ShapeDtypeStruct((LANE, N), F32),
        ],
        compiler_params=_params("arbitrary"),
        name="oproj_router",
    )(o_p, o_s, x, wo, g, wr_hi, wr_lo)


MOE_CHUNK = 128
MOE_ROW_ALIGN = 16


def _for_each_chunk(cnt_ref, step, n_chunks, fn, experts=range(N_EXPERTS)):
    for e in experts:
        for k in range(n_chunks):
            @pl.when(cnt_ref[step * N_EXPERTS + e] > k * MOE_CHUNK)
            def _(e=e, k=k):
                fn(e, k)


def _dispatch_kernel(off_ref, cnt_ref, hb_ref, rt_ref, utri_ref, xs_in_ref, xs_ref, stage_ref, sem):
    del xs_in_ref
    t = pl.program_id(0)
    T = hb_ref.shape[0]
    S = MOE_CHUNK
    nk = T // S

    def chunk_copy(step, e, k):
        start = pl.multiple_of(off_ref[step * N_EXPERTS + e] + k * S, MOE_ROW_ALIGN)
        slot = e * nk + k
        return pltpu.make_async_copy(stage_ref.at[slot], xs_ref.at[pl.ds(start, S), :], sem.at[slot])

    rt = rt_ref[...]
    expert_id = lax.broadcasted_iota(jnp.int32, (N_EXPERTS, T), 0).astype(F32)
    member = (rt[0:1, :] == expert_id) | (rt[1:2, :] == expert_id)
    rank = _dot(member.astype(BF16), utri_ref[...])
    rank = jnp.where(member, rank, 0.0)
    hb = hb_ref[...]
    row = lax.broadcasted_iota(jnp.int32, (S, 1), 0)

    def emit(e, k):
        sel = rank[e:e + 1, :] == (row + (k * S + 1)).astype(F32)
        stage_ref[e * nk + k] = _dot(sel.astype(BF16), hb).astype(BF16)
        chunk_copy(t, e, k).start()

    for e in range(N_EXPERTS):
        @pl.when(t > 0)
        def _(e=e):
            _for_each_chunk(cnt_ref, t - 1, nk, lambda e_, k: chunk_copy(t - 1, e_, k).wait(), experts=(e,))

        _for_each_chunk(cnt_ref, t, nk, emit, experts=(e,))

    @pl.when(t == pl.num_programs(0) - 1)
    def _():
        _for_each_chunk(cnt_ref, t, nk, lambda e, k: chunk_copy(t, e, k).wait())


def _dispatch(off, cnt, hb, route_t, utri, n_rows):
    N, D = hb.shape
    T = ROW_TILE
    nk = T // MOE_CHUNK
    n_slots = N_EXPERTS * nk
    xs_init = jnp.zeros((n_rows, D), BF16)
    return pl.pallas_call(
        _dispatch_kernel,
        grid_spec=pltpu.PrefetchScalarGridSpec(
            num_scalar_prefetch=2,
            grid=(N // T,),
            in_specs=[
                pl.BlockSpec((T, D), lambda t, off, cnt: (t, 0)),
                pl.BlockSpec((LANE, T), lambda t, off, cnt: (0, t)),
                pl.BlockSpec((T, T), lambda t, off, cnt: (0, 0)),
                pl.BlockSpec(memory_space=pl.ANY),
            ],
            out_specs=pl.BlockSpec(memory_space=pl.ANY),
            scratch_shapes=[
                pltpu.VMEM((n_slots, MOE_CHUNK, D), BF16),
                pltpu.SemaphoreType.DMA((n_slots,)),
            ],
        ),
        out_shape=jax.ShapeDtypeStruct((n_rows, D), BF16),
        input_output_aliases={5: 0},
        compiler_params=_params("arbitrary"),
        name="moe_dispatch",
    )(off, cnt, hb, route_t, utri, xs_init)


def _experts_kernel(te_ref, na_ref, xs_ref, wg_ref, wu_ref, wd_ref, out_ref, acc_ref):
    i = pl.program_id(0)
    f = pl.program_id(1)
    last = pl.num_programs(1) - 1
    active = i < na_ref[0]

    @pl.when(active)
    def _():
        @pl.when(f == 0)
        def _():
            acc_ref[...] = jnp.zeros(acc_ref.shape, F32)

        acc_ref[...] += _swiglu_tile(xs_ref[...], wg_ref.at[0], wu_ref.at[0], wd_ref.at[0])

        @pl.when(f == last)
        def _():
            out_ref[...] = acc_ref[...].astype(out_ref.dtype)

    @pl.when(jnp.logical_not(active) & (f == last))
    def _():
        out_ref[...] = jnp.zeros(out_ref.shape, out_ref.dtype)


def _experts(tile_expert, n_active, xs, wg, wu, wd):
    P, D = xs.shape
    T = ROW_TILE
    nf, tf = _ffn_splits(wg.shape[2])
    fidx = lambda i, f, na: jnp.where(i < na[0], f, nf - 1)
    return pl.pallas_call(
        _experts_kernel,
        grid_spec=pltpu.PrefetchScalarGridSpec(
            num_scalar_prefetch=2,
            grid=(P // T, nf),
            in_specs=[
                pl.BlockSpec((T, D), lambda i, f, te, na: (i, 0)),
                pl.BlockSpec((1, D, tf), lambda i, f, te, na: (te[i], 0, fidx(i, f, na))),
                pl.BlockSpec((1, D, tf), lambda i, f, te, na: (te[i], 0, fidx(i, f, na))),
                pl.BlockSpec((1, tf, D), lambda i, f, te, na: (te[i], fidx(i, f, na), 0)),
            ],
            out_specs=pl.BlockSpec((T, D), lambda i, f, te, na: (i, 0)),
            scratch_shapes=[pltpu.VMEM((T, D), F32)],
        ),
        out_shape=jax.ShapeDtypeStruct((P, D), BF16),
        compiler_params=_params("arbitrary", "arbitrary"),
        name="moe_experts",
    )(tile_expert, n_active, xs, wg, wu, wd)


def _combine_kernel(off_ref, cnt_ref, x_ref, route_ref, ltri_ref, ys_hbm, yp_ref, ysm_ref,
                    buf_ref, acc_ref, sem):
    t = pl.program_id(0)
    nt = pl.num_programs(0) - 1
    T = x_ref.shape[0]
    S = MOE_CHUNK
    nk = T // S

    def chunk_copy(step, e, k):
        start = pl.multiple_of(off_ref[step * N_EXPERTS + e] + k * S, MOE_ROW_ALIGN)
        bank, slot = step % 2, e * nk + k
        return pltpu.make_async_copy(ys_hbm.at[pl.ds(start, S), :], buf_ref.at[bank, slot], sem.at[bank, slot])

    @pl.when(t == 0)
    def _():
        _for_each_chunk(cnt_ref, t, nk, lambda e, k: chunk_copy(t, e, k).start())

    @pl.when(t < nt)
    def _():
        _for_each_chunk(cnt_ref, t + 1, nk, lambda e, k: chunk_copy(t + 1, e, k).start())

    route = route_ref[...]
    e1, e2, g1, g2 = (route[:, j:j + 1] for j in range(2 * TOP_K))
    lane = lax.broadcasted_iota(jnp.int32, route.shape, 1).astype(F32)
    member = (lane == e1) | (lane == e2)
    rank = _dot(ltri_ref[...], member.astype(BF16))
    rank = jnp.where(member, rank, 0.0)
    col = lax.broadcasted_iota(jnp.int32, (1, S), 1)
    acc_ref[...] = x_ref[...]

    def absorb(e, k):
        chunk_copy(t, e, k).wait()
        sel = rank[:, e:e + 1] == (col + (k * S + 1)).astype(F32)
        gate = jnp.where(e1 == e, g1, 0.0) + jnp.where(e2 == e, g2, 0.0)
        acc_ref[...] += gate * _dot(sel.astype(BF16), buf_ref[t % 2, e * nk + k])

    _for_each_chunk(cnt_ref, t, nk, absorb)

    @pl.when(t < nt)
    def _():
        yp_ref[...] = acc_ref[...]

    @pl.when(t == nt)
    def _():
        ysm_ref[...] = acc_ref[...]


def _combine(off, cnt, x3, route, ltri, ys, n_sample):
    N, D = x3.shape
    T = ROW_TILE
    nt = N // T - 1
    n_slots = N_EXPERTS * (T // MOE_CHUNK)
    return pl.pallas_call(
        _combine_kernel,
        grid_spec=pltpu.PrefetchScalarGridSpec(
            num_scalar_prefetch=2,
            grid=(nt + 1,),
            in_specs=[
                pl.BlockSpec((T, D), lambda t, off, cnt: (t, 0)),
                pl.BlockSpec((T, LANE), lambda t, off, cnt: (t, 0)),
                pl.BlockSpec((T, T), lambda t, off, cnt: (0, 0)),
                pl.BlockSpec(memory_space=pl.ANY),
            ],
            out_specs=[
                pl.BlockSpec((T, D), lambda t, off, cnt: (jnp.minimum(t, nt - 1), 0)),
                pl.BlockSpec((T, D), lambda t, off, cnt: (0, 0)),
            ],
            scratch_shapes=[
                pltpu.VMEM((2, n_slots, MOE_CHUNK, D), BF16),
                pltpu.VMEM((T, D), F32),
                pltpu.SemaphoreType.DMA((2, n_slots)),
            ],
        ),
        out_shape=[
            jax.ShapeDtypeStruct((N - n_sample, D), F32),
            jax.ShapeDtypeStruct((n_sample, D), F32),
        ],
        compiler_params=_params("arbitrary"),
        name="moe_combine",
    )(off, cnt, x3, route, ltri, ys)


def _routing_tables(route, tile):
    n = route.shape[0]
    n_tok_tiles = n // tile
    experts = route[:, :TOP_K].astype(jnp.int32)
    onehot = (experts[:, :, None] == jnp.arange(N_EXPERTS)[None, None, :]).astype(jnp.int32).sum(axis=1)
    cnt = onehot.reshape(n_tok_tiles, tile, N_EXPERTS).sum(axis=1)
    span = (cnt + MOE_ROW_ALIGN - 1) // MOE_ROW_ALIGN * MOE_ROW_ALIGN
    totals = span.sum(axis=0)
    tiles_per = (totals + MOE_CHUNK + tile - 1) // tile
    tile_end = jnp.cumsum(tiles_per)
    group_off = (tile_end - tiles_per) * tile
    off = group_off[None, :] + jnp.cumsum(span, axis=0) - span
    max_rows = n * TOP_K + N_EXPERTS * (MOE_CHUNK + n_tok_tiles * (MOE_ROW_ALIGN - 1))
    n_row_tiles = max_rows // tile + N_EXPERTS
    n_active = tile_end[-1:].astype(jnp.int32)
    tile_ids = jnp.minimum(jnp.arange(n_row_tiles, dtype=jnp.int32), n_active[0] - 1)
    tile_expert = jnp.sum((tile_ids[:, None] >= tile_end[None, :]).astype(jnp.int32), axis=1)
    return (off.reshape(-1).astype(jnp.int32), cnt.reshape(-1).astype(jnp.int32),
            tile_expert, n_active, n_row_tiles * tile)


def _group_matrices(d_model):
    n_groups = d_model // HEAD_DIM
    gsum = np.zeros((d_model, LANE), np.float32)
    gsum[np.arange(d_model), np.arange(d_model) // HEAD_DIM] = 1.0
    assert n_groups <= LANE
    return jnp.asarray(gsum, BF16), jnp.asarray(gsum.T.copy(), BF16)


def kernel(x_prompt, x_sample, cache_k, cache_v, state_pool, g_pool_norm, w_pool, pool_scale, g_attn, w_q, g_qn, lambda_q1, lambda_k1, lambda_q2, lambda_k2, g_subln, w_o, g_kv, w_k, w_v, g_kn, rel_bias, g_ffn, w_gate_dense, w_up_dense, w_down_dense, w_router, w_gate_moe, w_up_moe, w_down_moe):
    Bp, Lp, D = x_prompt.shape
    Bs, Ls, _ = x_sample.shape
    past = cache_k.shape[1]
    n_heads = D // V_DIM
    n_sample = Bs * Ls
    assert Bp == 1 and n_sample == ROW_TILE and Lp % ROW_TILE == 0
    assert g_pool_norm.shape[0] == 1 and g_attn.shape[0] == 1
    bf = lambda a: a.astype(BF16)
    row = lambda a: a.reshape(1, -1)

    x1, pool_p, pool_s = _pool_layer(
        x_prompt.reshape(Lp, D), x_sample.reshape(n_sample, D), state_pool[0],
        row(g_pool_norm[0]), bf(w_pool[0]), row(pool_scale[0]), past)
    x2 = _dense_ffn(x1, row(g_ffn[0]), bf(w_gate_dense[0]), bf(w_up_dense[0]), bf(w_down_dense[0]))

    layer = 1
    lambda_init = 0.8 - 0.6 * math.exp(-0.3 * layer)
    lam = (jnp.exp(jnp.sum(lambda_q1[0] * lambda_k1[0])) - jnp.exp(jnp.sum(lambda_q2[0] * lambda_k2[0]))
           + lambda_init).reshape(1)
    gsum, gbc = _group_matrices(D)
    n_groups = D // HEAD_DIM
    k_p, v_p, k_s, v_s, qh, kh, vh, q_s = _qkv_proj(
        x2, row(g_kv), row(g_attn[0]), bf(w_k), bf(w_v), bf(w_q[0]),
        row(jnp.tile(g_kn, n_groups)), row(jnp.tile(g_qn[0], n_groups) * (HEAD_DIM ** -0.5 * LOG2E)),
        gsum, gbc, n_sample)
    out_scale = 1.0 - lambda_init
    g_sub = row(g_subln[0])
    score_bound = LOG2E * (math.sqrt(HEAD_DIM) * jnp.max(jnp.abs(g_qn[0])) * jnp.max(jnp.abs(g_kn))
                           + jnp.max(jnp.abs(rel_bias)))
    o_p = lax.cond(
        score_bound <= UNSHIFTED_SCORE_LIMIT,
        functools.partial(_attn_prompt, out_scale=out_scale, online=False),
        functools.partial(_attn_prompt, out_scale=out_scale, online=True),
        rel_bias, lam, qh, kh, vh, g_subln[0].reshape(V_DIM, 1))

    tab = jnp.repeat(rel_bias.T, 2 * Ls, axis=0) * LOG2E
    o_s = _attn_sample(lam, q_s, cache_k, cache_v, k_s, v_s, tab, g_sub, out_scale)

    wr = jnp.pad(w_router[0], ((0, 0), (0, LANE - N_EXPERTS)))
    wr_hi = bf(wr)
    wr_lo = bf(wr - wr_hi.astype(F32))
    x3, h_moe, route, route_t = _oproj_router(o_p, o_s, x2, bf(w_o[0]), row(g_ffn[1]), wr_hi, wr_lo)
    off, cnt, tile_expert, n_active, n_rows = _routing_tables(route, ROW_TILE)
    ltri = jnp.asarray(np.tril(np.ones((ROW_TILE, ROW_TILE), np.float32)), BF16)
    xs = _dispatch(off, cnt, h_moe, route_t, ltri.T, n_rows)
    ys = _experts(tile_expert, n_active, xs, bf(w_gate_moe[0]), bf(w_up_moe[0]), bf(w_down_moe[0]))
    y_p, y_s = _combine(off, cnt, x3, route, ltri, ys, n_sample)

    return (y_p.reshape(Bp, Lp, D), y_s.reshape(Bs, Ls, D),
            k_p.reshape(Bp, Lp, n_heads, 2, HEAD_DIM), v_p.reshape(Bp, Lp, n_heads, V_DIM),
            pool_p.reshape(1, Bp, POOL_STATE, D),
            k_s.reshape(Bs, Ls, n_heads, 2, HEAD_DIM), v_s.reshape(Bs, Ls, n_heads, V_DIM),
            pool_s.reshape(1, Bs, POOL_STATE, D))
```

```python
import functools
import math

import numpy as np
import jax
import jax.numpy as jnp
from jax import lax
from jax.experimental import pallas as pl
from jax.experimental.pallas import tpu as pltpu

EPS = 1e-6
CHUNK = 64
POOL_WINDOWS = (2, 4, 8, 16)
POOL_STATE = max(POOL_WINDOWS) - 1
POOL_LEAD = 24
HEAD_DIM = 64
V_DIM = 2 * HEAD_DIM
N_EXPERTS = 8
TOP_K = 2
MAX_EXACT = 8
BUCKET_UPPER = (1, 2, 3, 4, 5, 6, 7, 8, 12, 16, 23, 32, 46, 64, 91)
FAR_DISTANCE = 128
NEG = -1e30
LOG2E = math.log2(math.e)
UNSHIFTED_SCORE_LIMIT = 80.0

ROW_TILE = 512
FAR_TILES_PER_TRIP = 4
LANE = 128
MXU_DIM = 256
VMEM_LIMIT = 56 * 1024 * 1024

F32 = jnp.float32
BF16 = jnp.bfloat16


def _dot(a, b):
    return jnp.dot(a, b, preferred_element_type=F32)


def _dot_nt(a, b):
    return lax.dot_general(a, b, (((1,), (1,)), ((), ())), preferred_element_type=F32)


def _rms_unit(x):
    return x * lax.rsqrt(jnp.mean(x * x, axis=-1, keepdims=True) + EPS)


def _rel_bias_tile(rel, tab):
    n = jnp.abs(rel)
    neg = tab(15)
    pos = tab(31)
    for b in range(14, -1, -1):
        lt = n < BUCKET_UPPER[b]
        neg = jnp.where(lt, tab(b), neg)
        pos = jnp.where(lt, tab(16 + b), pos)
    return jnp.where(rel > 0, pos, neg)


def _params(*sem):
    return pltpu.CompilerParams(dimension_semantics=sem, vmem_limit_bytes=VMEM_LIMIT)


def _pool_kernel(xp_ref, xs_ref, st_ref, g_ref, w_ref, sc_ref,
                 x1_ref, pp_ref, ps_ref, ext_ref, lva_ref, lvb_ref, ext3_ref, *, past_len):
    i = pl.program_id(0)
    nt = pl.num_programs(0) - 1
    T, D = xp_ref.shape
    gw = D // len(POOL_WINDOWS)

    @pl.when(i < nt)
    def _prompt():
        x = xp_ref[...]
        h = _rms_unit(x) * g_ref[...]

        @pl.when(i == 0)
        def _():
            ext_ref[0:POOL_LEAD, :] = jnp.zeros((POOL_LEAD, D), F32)
            for buf in (lva_ref, lvb_ref):
                buf[0:8, :] = jnp.zeros((8, gw), F32)

        ext_ref[POOL_LEAD:POOL_LEAD + T, :] = h
        row = i * T + lax.broadcasted_iota(jnp.int32, (T, 1), 0)
        parts = []
        for gi, w in enumerate(POOL_WINDOWS):
            c0 = gi * gw
            read = lambda lo, n: ext_ref[lo:lo + n, c0:c0 + gw]
            n_levels = w.bit_length() - 1
            for k in range(n_levels):
                d = 1 << k
                if k == n_levels - 1:
                    s = read(POOL_LEAD, T) + read(POOL_LEAD - d, T)
                else:
                    buf = (lva_ref, lvb_ref)[k % 2]
                    buf[8:POOL_LEAD + T, :] = read(8, 16 + T) + read(8 - d, 16 + T)
                    read = lambda lo, n, buf=buf: buf[lo:lo + n, :]
            cnt = jnp.minimum(w, row + 1).astype(F32)
            pooled = s / cnt - h[:, c0:c0 + gw]
            parts.append(_dot(pooled.astype(BF16), w_ref[gi]))
        mix = jnp.concatenate(parts, axis=-1) * sc_ref[...]
        x1_ref[...] = x + mix
        tail = ext_ref[T + 8:T + POOL_LEAD, :]
        ext_ref[8:POOL_LEAD, :] = tail

        @pl.when(i == nt - 1)
        def _():
            pp_ref[...] = tail[1:16, :]

    @pl.when(i == nt)
    def _sample():
        B = st_ref.shape[0]
        L = T // B
        x = xs_ref[...]
        h = _rms_unit(x) * g_ref[...]
        ext3_ref[:, 1:16, :] = st_ref[...]
        ext3_ref[:, 16:16 + L, :] = h.reshape(B, L, D)
        t = lax.broadcasted_iota(jnp.int32, (1, L, 1), 1)
        parts = []
        for gi, w in enumerate(POOL_WINDOWS):
            c0 = gi * gw
            s = ext3_ref[:, 16:16 + L, c0:c0 + gw]
            for j in range(1, w):
                s = s + ext3_ref[:, 16 - j:16 - j + L, c0:c0 + gw]
            cnt = jnp.minimum(w, past_len + t + 1).astype(F32)
            pooled = (s / cnt).reshape(T, gw) - h[:, c0:c0 + gw]
            parts.append(_dot(pooled.astype(BF16), w_ref[gi]))
        mix = jnp.concatenate(parts, axis=-1) * sc_ref[...]
        x1_ref[...] = x + mix
        ps_ref[...] = ext3_ref[:, 16 + L - POOL_STATE:16 + L, :]


def _pool_layer(xp, xs, state, g, w, sc, past_len):
    Lp, D = xp.shape
    T = ROW_TILE
    nt = Lp // T
    B = state.shape[0]
    L = xs.shape[0] // B
    return pl.pallas_call(
        functools.partial(_pool_kernel, past_len=past_len),
        grid=(nt + 1,),
        in_specs=[
            pl.BlockSpec((T, D), lambda i: (jnp.minimum(i, nt - 1), 0)),
            pl.BlockSpec((T, D), lambda i: (0, 0)),
            pl.BlockSpec((B, POOL_STATE, D), lambda i: (0, 0, 0)),
            pl.BlockSpec((1, D), lambda i: (0, 0)),
            pl.BlockSpec(w.shape, lambda i: (0, 0, 0)),
            pl.BlockSpec((1, D), lambda i: (0, 0)),
        ],
        out_specs=[
            pl.BlockSpec((T, D), lambda i: (i, 0)),
            pl.BlockSpec((POOL_STATE, D), lambda i: (0, 0)),
            pl.BlockSpec((B, POOL_STATE, D), lambda i: (0, 0, 0)),
        ],
        out_shape=[
            jax.ShapeDtypeStruct((Lp + T, D), F32),
            jax.ShapeDtypeStruct((POOL_STATE, D), F32),
            jax.ShapeDtypeStruct((B, POOL_STATE, D), F32),
        ],
        scratch_shapes=[
            pltpu.VMEM((POOL_LEAD + T, D), F32),
            pltpu.VMEM((POOL_LEAD + T, D // len(POOL_WINDOWS)), F32),
            pltpu.VMEM((POOL_LEAD + T, D // len(POOL_WINDOWS)), F32),
            pltpu.VMEM((B, 16 + L, D), F32),
        ],
        compiler_params=_params("arbitrary"),
        name="pool_mixer",
    )(xp, xs, state, g, w, sc)


def _swiglu_tile(hb, wg_ref, wu_ref, wd_ref):
    tf = wg_ref.shape[1]
    step = 3 * MXU_DIM
    out = None
    for c0 in range(0, tf, step):
        c1 = min(c0 + step, tf)
        gt = _dot(hb, wg_ref[:, c0:c1])
        ut = _dot(hb, wu_ref[:, c0:c1])
        a = gt * jax.nn.sigmoid(gt) * ut
        part = _dot(a.astype(BF16), wd_ref[c0:c1, :])
        out = part if out is None else out + part
    return out


def _ffn_kernel(x_ref, g_ref, wg_ref, wu_ref, wd_ref, out_ref, hb_ref, acc_ref):
    f = pl.program_id(1)

    @pl.when(f == 0)
    def _():
        x = x_ref[...]
        hb_ref[...] = (_rms_unit(x) * g_ref[...]).astype(BF16)
        acc_ref[...] = x

    acc_ref[...] += _swiglu_tile(hb_ref[...], wg_ref, wu_ref, wd_ref)

    @pl.when(f == pl.num_programs(1) - 1)
    def _():
        out_ref[...] = acc_ref[...]


def _ffn_splits(d_ff):
    nf = 1
    return nf, d_ff // nf


def _dense_ffn(x, g, wg, wu, wd):
    N, D = x.shape
    T = ROW_TILE
    nf, tf = _ffn_splits(wg.shape[1])
    return pl.pallas_call(
        _ffn_kernel,
        grid=(N // T, nf),
        in_specs=[
            pl.BlockSpec((T, D), lambda i, f: (i, 0)),
            pl.BlockSpec((1, D), lambda i, f: (0, 0)),
            pl.BlockSpec((D, tf), lambda i, f: (0, f)),
            pl.BlockSpec((D, tf), lambda i, f: (0, f)),
            pl.BlockSpec((tf, D), lambda i, f: (f, 0)),
        ],
        out_specs=pl.BlockSpec((T, D), lambda i, f: (i, 0)),
        out_shape=jax.ShapeDtypeStruct((N, D), F32),
        scratch_shapes=[pltpu.VMEM((T, D), BF16), pltpu.VMEM((T, D), F32)],
        compiler_params=_params("arbitrary", "arbitrary"),
        name="dense_swiglu",
    )(x, g, wg, wu, wd)


def _qkv_kernel(x_ref, gkv_ref, gq_ref, wk_ref, wv_ref, wq_ref, gkn_ref, gqn_ref,
                gsum_ref, gbc_ref,
                kp_ref, vp_ref, ks_ref, vs_ref, qh_ref, kh_ref, vh_ref, qs_ref):
    i = pl.program_id(0)
    nt = pl.num_programs(0) - 1
    n_heads = qh_ref.shape[0]

    xn = _rms_unit(x_ref[...])
    hkv = (xn * gkv_ref[...]).astype(BF16)
    hq = (xn * gq_ref[...]).astype(BF16)

    def head_norm(y, g):
        ssq = _dot((y * y).astype(BF16), gsum_ref[...])
        rs = lax.rsqrt(ssq * (1.0 / HEAD_DIM) + EPS)
        rs_hi = rs.astype(BF16)
        rs_lo = (rs - rs_hi.astype(F32)).astype(BF16)
        rsb = _dot(rs_hi, gbc_ref[...]) + _dot(rs_lo, gbc_ref[...])
        return y * rsb * g

    k = head_norm(_dot(hkv, wk_ref[...]), gkn_ref[...])
    v = _dot(hkv, wv_ref[...])
    q = head_norm(_dot(hq, wq_ref[...]), gqn_ref[...])

    @pl.when(i < nt)
    def _():
        kp_ref[...] = k
        vp_ref[...] = v
        for h in range(n_heads):
            sl = slice(h * V_DIM, (h + 1) * V_DIM)
            qh_ref[h] = q[:, sl].T.astype(BF16)
            kh_ref[h] = k[:, sl].astype(BF16)
            vh_ref[h] = v[:, sl].T.astype(BF16)

    @pl.when(i == nt)
    def _():
        ks_ref[...] = k
        vs_ref[...] = v
        qs_ref[...] = q.astype(BF16)


def _qkv_proj(x, gkv, gq, wk, wv, wq, gkn_t, gqn_t, gsum, gbc, n_sample):
    N, D = x.shape
    T = ROW_TILE
    nt = N // T - 1
    Lp = N - n_sample
    n_heads = D // V_DIM
    const2 = lambda i: (0, 0)
    prow = lambda i: (jnp.minimum(i, nt - 1), 0)
    phead = lambda i: (0, jnp.minimum(i, nt - 1), 0)
    pheadt = lambda i: (0, 0, jnp.minimum(i, nt - 1))
    return pl.pallas_call(
        _qkv_kernel,
        grid=(nt + 1,),
        in_specs=[
            pl.BlockSpec((T, D), lambda i: (i, 0)),
            pl.BlockSpec((1, D), const2),
            pl.BlockSpec((1, D), const2),
            pl.BlockSpec((D, D), const2),
            pl.BlockSpec((D, D), const2),
            pl.BlockSpec((D, D), const2),
            pl.BlockSpec((1, D), const2),
            pl.BlockSpec((1, D), const2),
            pl.BlockSpec(gsum.shape, const2),
            pl.BlockSpec(gbc.shape, const2),
        ],
        out_specs=[
            pl.BlockSpec((T, D), prow),
            pl.BlockSpec((T, D), prow),
            pl.BlockSpec((T, D), const2),
            pl.BlockSpec((T, D), const2),
            pl.BlockSpec((n_heads, V_DIM, T), pheadt),
            pl.BlockSpec((n_heads, T, V_DIM), phead),
            pl.BlockSpec((n_heads, V_DIM, T), pheadt),
            pl.BlockSpec((T, D), const2),
        ],
        out_shape=[
            jax.ShapeDtypeStruct((Lp, D), F32),
            jax.ShapeDtypeStruct((Lp, D), F32),
            jax.ShapeDtypeStruct((n_sample, D), F32),
            jax.ShapeDtypeStruct((n_sample, D), F32),
            jax.ShapeDtypeStruct((n_heads, V_DIM, Lp), BF16),
            jax.ShapeDtypeStruct((n_heads, Lp, V_DIM), BF16),
            jax.ShapeDtypeStruct((n_heads, V_DIM, Lp), BF16),
            jax.ShapeDtypeStruct((n_sample, D), BF16),
        ],
        compiler_params=_params("arbitrary"),
        name="qkv_proj",
    )(x, gkv, gq, wk, wv, wq, gkn_t, gqn_t, gsum, gbc)


def _diff_out(o0, o1, lam, g, out_scale):
    o = o0 - lam * o1
    return _rms_unit(o) * g * out_scale


def _sublane_partial_sum(p):
    return jnp.sum(p.reshape(p.shape[0] // 8, 8, p.shape[1]), axis=0)


def _attn_prompt_kernel(bias_ref, lam_ref, qt_ref, k_ref, vt_ref, g_ref, o_ref,
                        bn_ref, l_ref, acc_ref, *m_scratch, out_scale):
    h = pl.program_id(0)
    i = pl.program_id(1)
    T = qt_ref.shape[2]
    online = bool(m_scratch)

    @pl.when(i == 0)
    def _():
        key = lax.broadcasted_iota(jnp.int32, (T, T), 0)
        qry = lax.broadcasted_iota(jnp.int32, (T, T), 1)
        tab = lambda b: bias_ref[b, h] * LOG2E
        visible = (key // CHUNK) <= (qry // CHUNK)
        bn_ref[0:T, :] = _rel_bias_tile(key - qry - T, tab)
        bn_ref[T:2 * T, :] = jnp.where(visible, _rel_bias_tile(key - qry, tab), NEG)

    if online:
        m_ref, = m_scratch
        m_ref[...] = jnp.full(m_ref.shape, NEG, F32)
    l_ref[...] = jnp.zeros(l_ref.shape, F32)
    acc_ref[...] = jnp.zeros(acc_ref.shape, F32)
    qt = qt_ref[0]
    dim = lax.broadcasted_iota(jnp.int32, qt.shape, 0)
    zero = jnp.zeros_like(qt)
    qc = (jnp.where(dim < HEAD_DIM, qt, zero), jnp.where(dim >= HEAD_DIM, qt, zero))

    def update(j, bias, width=T):
        start = pl.multiple_of(j * T, T)
        kt = k_ref[0, pl.ds(start, width), :]
        vt = vt_ref[0, :, pl.ds(start, width)]
        for c in range(2):
            s = _dot(kt, qc[c]) + bias
            if online:
                m_old = m_ref[c]
                m_new = jnp.maximum(m_old, jnp.max(s, axis=0, keepdims=True))
                alpha = jnp.exp2(m_old - m_new)
                p = jnp.exp2(s - m_new)
                l_ref[c] = alpha * l_ref[c] + _sublane_partial_sum(p)
                acc_ref[c] = alpha * acc_ref[c] + _dot(vt, p.astype(BF16))
                m_ref[c] = m_new
            else:
                p = jnp.exp2(s)
                l_ref[c] += _sublane_partial_sum(p)
                acc_ref[c] += _dot(vt, p.astype(BF16))

    far_bias = bias_ref[15, h] * LOG2E

    n_far = jnp.maximum(i - 1, 0)

    def far_body(jj, carry):
        update(FAR_TILES_PER_TRIP * jj, far_bias, width=FAR_TILES_PER_TRIP * T)
        return carry

    n_trips = n_far // FAR_TILES_PER_TRIP
    lax.fori_loop(0, n_trips, far_body, 0)
    done = n_trips * FAR_TILES_PER_TRIP

    @pl.when(n_far - done >= 2)
    def _():
        update(done, far_bias, width=2 * T)

    @pl.when((n_far - done) % 2 == 1)
    def _():
        update(n_far - 1, far_bias)

    @pl.when(i >= 1)
    def _():
        update(i - 1, bn_ref[...], width=2 * T)

    @pl.when(i == 0)
    def _():
        update(0, bn_ref[T:2 * T, :])

    o0 = acc_ref[0] / jnp.sum(l_ref[0], axis=0, keepdims=True)
    o1 = acc_ref[1] / jnp.sum(l_ref[1], axis=0, keepdims=True)
    o = o0 - lam_ref[0] * o1
    y = o * lax.rsqrt(jnp.mean(o * o, axis=0, keepdims=True) + EPS) * g_ref[...] * out_scale
    o_ref[...] = y.T.astype(o_ref.dtype)


def _attn_prompt(rel_bias, lam, qth, kh, vth, g_subln, out_scale, online):
    n_heads, Lp, _ = kh.shape
    T = ROW_TILE
    assert T % CHUNK == 0 and T >= FAR_DISTANCE and FAR_TILES_PER_TRIP == 4
    m_scratch = [pltpu.VMEM((2, 1, T), F32)] if online else []
    return pl.pallas_call(
        functools.partial(_attn_prompt_kernel, out_scale=out_scale),
        grid=(n_heads, Lp // T),
        in_specs=[
            pl.BlockSpec(memory_space=pltpu.SMEM),
            pl.BlockSpec(memory_space=pltpu.SMEM),
            pl.BlockSpec((1, V_DIM, T), lambda h, i: (h, 0, i)),
            pl.BlockSpec((1, Lp, V_DIM), lambda h, i: (h, 0, 0)),
            pl.BlockSpec((1, V_DIM, Lp), lambda h, i: (h, 0, 0)),
            pl.BlockSpec((V_DIM, 1), lambda h, i: (0, 0)),
        ],
        out_specs=pl.BlockSpec((T, V_DIM), lambda h, i: (i, h)),
        out_shape=jax.ShapeDtypeStruct((Lp, n_heads * V_DIM), BF16),
        scratch_shapes=[
            pltpu.VMEM((2 * T, T), F32),
            pltpu.VMEM((2, 8, T), F32),
            pltpu.VMEM((2, V_DIM, T), F32),
        ] + m_scratch,
        compiler_params=_params("arbitrary", "arbitrary"),
        name="attn_prompt_online" if online else "attn_prompt",
    )(rel_bias, lam, qth, kh, vth, g_subln)


def _attn_sample_kernel(lam_ref, q_ref, ckt_ref, cv_ref, kn_ref, vn_ref, tab_ref, g_ref,
                        o_ref, nearb_ref, newb_ref, qbd_ref, m_ref, l_ref, acc_ref, *, out_scale):
    b = pl.program_id(0)
    kb = pl.program_id(1)
    nkb = pl.num_programs(1)
    R, Tk = nearb_ref.shape
    L = kn_ref.shape[0]
    D = q_ref.shape[1]
    n_heads = D // V_DIM

    @pl.when((b == 0) & (kb == 0))
    def _():
        tab = lambda bkt: tab_ref[:, bkt:bkt + 1]
        t_near = lax.broadcasted_iota(jnp.int32, (R, Tk), 0) % L
        col = lax.broadcasted_iota(jnp.int32, (R, Tk), 1)
        nearb_ref[...] = _rel_bias_tile(col - Tk - t_near, tab)
        t_new = lax.broadcasted_iota(jnp.int32, (R, L), 0) % L
        col_new = lax.broadcasted_iota(jnp.int32, (R, L), 1)
        newb_ref[...] = _rel_bias_tile(col_new - t_new, tab)

    @pl.when(kb == 0)
    def _():
        m_ref[...] = jnp.full(m_ref.shape, NEG, F32)
        l_ref[...] = jnp.zeros(l_ref.shape, F32)
        acc_ref[...] = jnp.zeros(acc_ref.shape, F32)
        q_rows = jnp.concatenate([q_ref[...]] * (R // L), axis=0)
        row_group = lax.broadcasted_iota(jnp.int32, (R, D), 0) // L
        col_group = lax.broadcasted_iota(jnp.int32, (R, D), 1) // HEAD_DIM
        qbd_ref[...] = jnp.where(row_group == col_group, q_rows, jnp.zeros_like(q_rows))

    rows_per_head = 2 * L

    def update(s, value_of_head):
        m_old = m_ref[...]
        m_new = jnp.maximum(m_old, jnp.max(s, axis=-1, keepdims=True))
        alpha = jnp.exp2(m_old - m_new)
        p = jnp.exp2(s - m_new)
        l_ref[...] = alpha * l_ref[...] + jnp.sum(p, axis=-1, keepdims=True)
        pb = p.astype(BF16)
        for h in range(n_heads):
            rs = slice(h * rows_per_head, (h + 1) * rows_per_head)
            acc_ref[rs, :] = alpha[rs, :] * acc_ref[rs, :] + _dot(pb[rs, :], value_of_head(h))
        m_ref[...] = m_new

    qbd = qbd_ref[...]
    s = _dot(qbd, ckt_ref[0].astype(BF16))
    cache_value = lambda h: cv_ref[0, pl.ds(h, Tk, stride=n_heads), :].astype(BF16)

    @pl.when(kb < nkb - 1)
    def _():
        update(s + tab_ref[:, 15:16], cache_value)

    @pl.when(kb == nkb - 1)
    def _():
        update(s + nearb_ref[...], cache_value)
        s_new = _dot_nt(qbd, kn_ref[...].astype(BF16)) + newb_ref[...]
        update(s_new, lambda h: vn_ref[:, h * V_DIM:(h + 1) * V_DIM].astype(BF16))
        lam = lam_ref[0]
        o = acc_ref[...] / l_ref[...]
        for h in range(n_heads):
            r0 = h * rows_per_head
            o_ref[:, h * V_DIM:(h + 1) * V_DIM] = _diff_out(
                o[r0:r0 + L, :], o[r0 + L:r0 + 2 * L, :], lam, g_ref[...], out_scale).astype(o_ref.dtype)


def _attn_sample(lam, q, cache_k, cache_v, k_new, v_new, tab, g_subln, out_scale):
    B, past, n_heads = cache_v.shape[:3]
    D = n_heads * V_DIM
    L = k_new.shape[0] // B
    R = (D // HEAD_DIM) * L
    Tk = min(2048, past)
    assert past % Tk == 0 and Tk >= FAR_DISTANCE + L
    cache_kt = jnp.transpose(cache_k, (0, 2, 3, 4, 1)).reshape(B, D, past)
    return pl.pallas_call(
        functools.partial(_attn_sample_kernel, out_scale=out_scale),
        grid=(B, past // Tk),
        in_specs=[
            pl.BlockSpec(memory_space=pltpu.SMEM),
            pl.BlockSpec((L, D), lambda b, k: (b, 0)),
            pl.BlockSpec((1, D, Tk), lambda b, k: (b, 0, k)),
            pl.BlockSpec((1, Tk * n_heads, V_DIM), lambda b, k: (b, k, 0)),
            pl.BlockSpec((L, D), lambda b, k: (b, 0)),
            pl.BlockSpec((L, D), lambda b, k: (b, 0)),
            pl.BlockSpec(tab.shape, lambda b, k: (0, 0)),
            pl.BlockSpec((1, V_DIM), lambda b, k: (0, 0)),
        ],
        out_specs=pl.BlockSpec((L, D), lambda b, k: (b, 0)),
        out_shape=jax.ShapeDtypeStruct((B * L, D), BF16),
        scratch_shapes=[
            pltpu.VMEM((R, Tk), F32),
            pltpu.VMEM((R, L), F32),
            pltpu.VMEM((R, D), BF16),
            pltpu.VMEM((R, 1), F32),
            pltpu.VMEM((R, 1), F32),
            pltpu.VMEM((R, V_DIM), F32),
        ],
        compiler_params=_params("arbitrary", "arbitrary"),
        name="attn_sample",
    )(lam, q, cache_kt, cache_v.reshape(B, past * n_heads, V_DIM), k_new, v_new, tab, g_subln)


def _oproj_router_kernel(op_ref, os_ref, x_ref, wo_ref, g_ref, wrh_ref, wrl_ref,
                         x3_ref, h_ref, route_ref, route_t_ref):
    i = pl.program_id(0)
    nt = pl.num_programs(0) - 1
    o = jnp.where(i == nt, os_ref[...], op_ref[...])
    x3 = x_ref[...] + _dot(o, wo_ref[...])
    x3_ref[...] = x3
    h = _rms_unit(x3) * g_ref[...]
    h_hi = h.astype(BF16)
    h_ref[...] = h_hi
    h_lo = (h - h_hi.astype(F32)).astype(BF16)
    logits = _dot(h_hi, wrh_ref[...]) + (_dot(h_lo, wrh_ref[...]) + _dot(h_hi, wrl_ref[...]))
    lane = lax.broadcasted_iota(jnp.int32, logits.shape, 1)
    lg = jnp.where(lane < N_EXPERTS, logits, -jnp.inf)
    m1 = jnp.max(lg, axis=-1, keepdims=True)
    i1 = jnp.min(jnp.where(lg == m1, lane, LANE), axis=-1, keepdims=True)
    lg2 = jnp.where(lane == i1, -jnp.inf, lg)
    m2 = jnp.max(lg2, axis=-1, keepdims=True)
    i2 = jnp.min(jnp.where(lg2 == m2, lane, LANE), axis=-1, keepdims=True)
    e2 = jnp.exp(m2 - m1)
    den = 1.0 + e2
    g1 = 1.0 / den
    g2 = e2 / den
    route = jnp.where(lane == 0, i1.astype(F32),
                      jnp.where(lane == 1, i2.astype(F32),
                                jnp.where(lane == 2, g1,
                                          jnp.where(lane == 3, g2, 0.0))))
    route_ref[...] = route
    route_t_ref[...] = route.T


def _oproj_router(o_p, o_s, x, wo, g, wr_hi, wr_lo):
    N, D = x.shape
    T = ROW_TILE
    nt = N // T - 1
    const2 = lambda i: (0, 0)
    row = lambda i: (i, 0)
    return pl.pallas_call(
        _oproj_router_kernel,
        grid=(nt + 1,),
        in_specs=[
            pl.BlockSpec((T, D), lambda i: (jnp.minimum(i, nt - 1), 0)),
            pl.BlockSpec((T, D), const2),
            pl.BlockSpec((T, D), row),
            pl.BlockSpec((D, D), const2),
            pl.BlockSpec((1, D), const2),
            pl.BlockSpec((D, LANE), const2),
            pl.BlockSpec((D, LANE), const2),
        ],
        out_specs=[
            pl.BlockSpec((T, D), row),
            pl.BlockSpec((T, D), row),
            pl.BlockSpec((T, LANE), row),
            pl.BlockSpec((LANE, T), lambda i: (0, i)),
        ],
        out_shape=[
            jax.ShapeDtypeStruct((N, D), F32),
            jax.ShapeDtypeStruct((N, D), BF16),
            jax.ShapeDtypeStruct((N, LANE), F32),
            jax.ShapeDtypeStruct((LANE, N), F32),
        ],
        compiler_params=_params("arbitrary"),
        name="oproj_router",
    )(o_p, o_s, x, wo, g, wr_hi, wr_lo)


MOE_CHUNK = 128
MOE_ROW_ALIGN = 16


def _for_each_chunk(cnt_ref, step, n_chunks, fn, first=0):
    for e in range(N_EXPERTS):
        for k in range(first, n_chunks):
            if k == 0:
                fn(e, k)
            else:
                @pl.when(cnt_ref[step * N_EXPERTS + e] > k * MOE_CHUNK)
                def _(e=e, k=k):
                    fn(e, k)


def _dispatch_kernel(off_ref, cnt_ref, hb_ref, rt_ref, utri_ref, xs_in_ref, xs_ref, stage_ref, sem):
    del xs_in_ref
    t = pl.program_id(0)
    T = hb_ref.shape[0]
    S = MOE_CHUNK
    nk = T // S

    def chunk_copy(step, e, k):
        start = pl.multiple_of(off_ref[step * N_EXPERTS + e] + k * S, MOE_ROW_ALIGN)
        slot = k * N_EXPERTS + e
        return pltpu.make_async_copy(stage_ref.at[slot], xs_ref.at[pl.ds(start, S), :], sem.at[slot])

    rt = rt_ref[...]
    expert_id = lax.broadcasted_iota(jnp.int32, (N_EXPERTS, T), 0).astype(F32)
    member = (rt[0:1, :] == expert_id) | (rt[1:2, :] == expert_id)
    rank = _dot(member.astype(BF16), utri_ref[...])
    rank = jnp.where(member, rank, 0.0)
    hb = hb_ref[...]

    row3 = lax.broadcasted_iota(jnp.int32, (1, S, 1), 1)
    sel_first = (rank[:, None, :] == (row3 + 1).astype(F32)).astype(BF16).reshape(N_EXPERTS * S, T)
    first = _dot(sel_first, hb).astype(BF16).reshape(N_EXPERTS, S, hb.shape[1])

    @pl.when(t > 0)
    def _():
        _for_each_chunk(cnt_ref, t - 1, nk, lambda e, k: chunk_copy(t - 1, e, k).wait())

    stage_ref[0:N_EXPERTS] = first
    for e in range(N_EXPERTS):
        chunk_copy(t, e, 0).start()

    row = lax.broadcasted_iota(jnp.int32, (S, 1), 0)

    def emit(e, k):
        sel = rank[e:e + 1, :] == (row + (k * S + 1)).astype(F32)
        stage_ref[k * N_EXPERTS + e] = _dot(sel.astype(BF16), hb).astype(BF16)
        chunk_copy(t, e, k).start()

    _for_each_chunk(cnt_ref, t, nk, emit, first=1)

    @pl.when(t == pl.num_programs(0) - 1)
    def _():
        _for_each_chunk(cnt_ref, t, nk, lambda e, k: chunk_copy(t, e, k).wait())


def _dispatch(off, cnt, hb, route_t, utri, n_rows):
    N, D = hb.shape
    T = ROW_TILE
    nk = T // MOE_CHUNK
    n_slots = N_EXPERTS * nk
    xs_init = jnp.zeros((n_rows, D), BF16)
    return pl.pallas_call(
        _dispatch_kernel,
        grid_spec=pltpu.PrefetchScalarGridSpec(
            num_scalar_prefetch=2,
            grid=(N // T,),
            in_specs=[
                pl.BlockSpec((T, D), lambda t, off, cnt: (t, 0)),
                pl.BlockSpec((LANE, T), lambda t, off, cnt: (0, t)),
                pl.BlockSpec((T, T), lambda t, off, cnt: (0, 0)),
                pl.BlockSpec(memory_space=pl.ANY),
            ],
            out_specs=pl.BlockSpec(memory_space=pl.ANY),
            scratch_shapes=[
                pltpu.VMEM((n_slots, MOE_CHUNK, D), BF16),
                pltpu.SemaphoreType.DMA((n_slots,)),
            ],
        ),
        out_shape=jax.ShapeDtypeStruct((n_rows, D), BF16),
        input_output_aliases={5: 0},
        compiler_params=_params("arbitrary"),
        name="moe_dispatch",
    )(off, cnt, hb, route_t, utri, xs_init)


def _experts_kernel(te_ref, na_ref, xs_ref, wg_ref, wu_ref, wd_ref, out_ref, acc_ref):
    i = pl.program_id(0)
    f = pl.program_id(1)
    last = pl.num_programs(1) - 1
    active = i < na_ref[0]

    @pl.when(active)
    def _():
        @pl.when(f == 0)
        def _():
            acc_ref[...] = jnp.zeros(acc_ref.shape, F32)

        acc_ref[...] += _swiglu_tile(xs_ref[...], wg_ref.at[0], wu_ref.at[0], wd_ref.at[0])

        @pl.when(f == last)
        def _():
            out_ref[...] = acc_ref[...].astype(out_ref.dtype)

    @pl.when(jnp.logical_not(active) & (f == last))
    def _():
        out_ref[...] = jnp.zeros(out_ref.shape, out_ref.dtype)


def _experts(tile_expert, n_active, xs, wg, wu, wd):
    P, D = xs.shape
    T = ROW_TILE
    nf, tf = _ffn_splits(wg.shape[2])
    fidx = lambda i, f, na: jnp.where(i < na[0], f, nf - 1)
    return pl.pallas_call(
        _experts_kernel,
        grid_spec=pltpu.PrefetchScalarGridSpec(
            num_scalar_prefetch=2,
            grid=(P // T, nf),
            in_specs=[
                pl.BlockSpec((T, D), lambda i, f, te, na: (i, 0)),
                pl.BlockSpec((1, D, tf), lambda i, f, te, na: (te[i], 0, fidx(i, f, na))),
                pl.BlockSpec((1, D, tf), lambda i, f, te, na: (te[i], 0, fidx(i, f, na))),
                pl.BlockSpec((1, tf, D), lambda i, f, te, na: (te[i], fidx(i, f, na), 0)),
            ],
            out_specs=pl.BlockSpec((T, D), lambda i, f, te, na: (i, 0)),
            scratch_shapes=[pltpu.VMEM((T, D), F32)],
        ),
        out_shape=jax.ShapeDtypeStruct((P, D), BF16),
        compiler_params=_params("arbitrary", "arbitrary"),
        name="moe_experts",
    )(tile_expert, n_active, xs, wg, wu, wd)


def _combine_kernel(off_ref, cnt_ref, x_ref, route_ref, ltri_ref, ys_hbm, yp_ref, ysm_ref,
                    buf_ref, acc_ref, sem):
    t = pl.program_id(0)
    nt = pl.num_programs(0) - 1
    T = x_ref.shape[0]
    S = MOE_CHUNK
    nk = T // S

    def chunk_copy(step, e, k):
        start = pl.multiple_of(off_ref[step * N_EXPERTS + e] + k * S, MOE_ROW_ALIGN)
        bank, slot = step % 2, k * N_EXPERTS + e
        return pltpu.make_async_copy(ys_hbm.at[pl.ds(start, S), :], buf_ref.at[bank, slot], sem.at[bank, slot])

    @pl.when(t == 0)
    def _():
        _for_each_chunk(cnt_ref, t, nk, lambda e, k: chunk_copy(t, e, k).start())

    @pl.when(t < nt)
    def _():
        _for_each_chunk(cnt_ref, t + 1, nk, lambda e, k: chunk_copy(t + 1, e, k).start())

    route = route_ref[...]
    e1, e2, g1, g2 = (route[:, j:j + 1] for j in range(2 * TOP_K))
    lane = lax.broadcasted_iota(jnp.int32, route.shape, 1).astype(F32)
    member = (lane == e1) | (lane == e2)
    rank = _dot(ltri_ref[...], member.astype(BF16))
    rank = jnp.where(member, rank, 0.0)
    col = lax.broadcasted_iota(jnp.int32, (1, S), 1)
    gate_of = lambda e: jnp.where(e1 == e, g1, 0.0) + jnp.where(e2 == e, g2, 0.0)
    select = lambda e, k: rank[:, e:e + 1] == (col + (k * S + 1)).astype(F32)

    for e in range(N_EXPERTS):
        chunk_copy(t, e, 0).wait()
    weight = jnp.concatenate([jnp.where(select(e, 0), gate_of(e), 0.0) for e in range(N_EXPERTS)], axis=-1)
    w_hi = weight.astype(BF16)
    w_lo = (weight - w_hi.astype(F32)).astype(BF16)
    rows = buf_ref[t % 2, 0:N_EXPERTS].reshape(N_EXPERTS * S, x_ref.shape[1])
    acc_ref[...] = x_ref[...] + (_dot(w_hi, rows) + _dot(w_lo, rows))

    def absorb(e, k):
        chunk_copy(t, e, k).wait()
        acc_ref[...] += gate_of(e) * _dot(select(e, k).astype(BF16), buf_ref[t % 2, k * N_EXPERTS + e])

    _for_each_chunk(cnt_ref, t, nk, absorb, first=1)

    @pl.when(t < nt)
    def _():
        yp_ref[...] = acc_ref[...]

    @pl.when(t == nt)
    def _():
        ysm_ref[...] = acc_ref[...]


def _combine(off, cnt, x3, route, ltri, ys, n_sample):
    N, D = x3.shape
    T = ROW_TILE
    nt = N // T - 1
    n_slots = N_EXPERTS * (T // MOE_CHUNK)
    return pl.pallas_call(
        _combine_kernel,
        grid_spec=pltpu.PrefetchScalarGridSpec(
            num_scalar_prefetch=2,
            grid=(nt + 1,),
            in_specs=[
                pl.BlockSpec((T, D), lambda t, off, cnt: (t, 0)),
                pl.BlockSpec((T, LANE), lambda t, off, cnt: (t, 0)),
                pl.BlockSpec((T, T), lambda t, off, cnt: (0, 0)),
                pl.BlockSpec(memory_space=pl.ANY),
            ],
            out_specs=[
                pl.BlockSpec((T, D), lambda t, off, cnt: (jnp.minimum(t, nt - 1), 0)),
                pl.BlockSpec((T, D), lambda t, off, cnt: (0, 0)),
            ],
            scratch_shapes=[
                pltpu.VMEM((2, n_slots, MOE_CHUNK, D), BF16),
                pltpu.VMEM((T, D), F32),
                pltpu.SemaphoreType.DMA((2, n_slots)),
            ],
        ),
        out_shape=[
            jax.ShapeDtypeStruct((N - n_sample, D), F32),
            jax.ShapeDtypeStruct((n_sample, D), F32),
        ],
        compiler_params=_params("arbitrary"),
        name="moe_combine",
    )(off, cnt, x3, route, ltri, ys)


def _routing_tables(route, tile):
    n = route.shape[0]
    n_tok_tiles = n // tile
    experts = route[:, :TOP_K].astype(jnp.int32)
    onehot = (experts[:, :, None] == jnp.arange(N_EXPERTS)[None, None, :]).astype(jnp.int32).sum(axis=1)
    cnt = onehot.reshape(n_tok_tiles, tile, N_EXPERTS).sum(axis=1)
    span = (cnt + MOE_ROW_ALIGN - 1) // MOE_ROW_ALIGN * MOE_ROW_ALIGN
    totals = span.sum(axis=0)
    tiles_per = (totals + MOE_CHUNK + tile - 1) // tile
    tile_end = jnp.cumsum(tiles_per)
    group_off = (tile_end - tiles_per) * tile
    off = group_off[None, :] + jnp.cumsum(span, axis=0) - span
    max_rows = n * TOP_K + N_EXPERTS * (MOE_CHUNK + n_tok_tiles * (MOE_ROW_ALIGN - 1))
    n_row_tiles = max_rows // tile + N_EXPERTS
    n_active = tile_end[-1:].astype(jnp.int32)
    tile_ids = jnp.minimum(jnp.arange(n_row_tiles, dtype=jnp.int32), n_active[0] - 1)
    tile_expert = jnp.sum((tile_ids[:, None] >= tile_end[None, :]).astype(jnp.int32), axis=1)
    return (off.reshape(-1).astype(jnp.int32), cnt.reshape(-1).astype(jnp.int32),
            tile_expert, n_active, n_row_tiles * tile)


def _group_matrices(d_model):
    n_groups = d_model // HEAD_DIM
    gsum = np.zeros((d_model, LANE), np.float32)
    gsum[np.arange(d_model), np.arange(d_model) // HEAD_DIM] = 1.0
    assert n_groups <= LANE
    return jnp.asarray(gsum, BF16), jnp.asarray(gsum.T.copy(), BF16)


def kernel(x_prompt, x_sample, cache_k, cache_v, state_pool, g_pool_norm, w_pool, pool_scale, g_attn, w_q, g_qn, lambda_q1, lambda_k1, lambda_q2, lambda_k2, g_subln, w_o, g_kv, w_k, w_v, g_kn, rel_bias, g_ffn, w_gate_dense, w_up_dense, w_down_dense, w_router, w_gate_moe, w_up_moe, w_down_moe):
    Bp, Lp, D = x_prompt.shape
    Bs, Ls, _ = x_sample.shape
    past = cache_k.shape[1]
    n_heads = D // V_DIM
    n_sample = Bs * Ls
    assert Bp == 1 and n_sample == ROW_TILE and Lp % ROW_TILE == 0
    assert g_pool_norm.shape[0] == 1 and g_attn.shape[0] == 1
    bf = lambda a: a.astype(BF16)
    row = lambda a: a.reshape(1, -1)

    x1, pool_p, pool_s = _pool_layer(
        x_prompt.reshape(Lp, D), x_sample.reshape(n_sample, D), state_pool[0],
        row(g_pool_norm[0]), bf(w_pool[0]), row(pool_scale[0]), past)
    x2 = _dense_ffn(x1, row(g_ffn[0]), bf(w_gate_dense[0]), bf(w_up_dense[0]), bf(w_down_dense[0]))

    layer = 1
    lambda_init = 0.8 - 0.6 * math.exp(-0.3 * layer)
    lam = (jnp.exp(jnp.sum(lambda_q1[0] * lambda_k1[0])) - jnp.exp(jnp.sum(lambda_q2[0] * lambda_k2[0]))
           + lambda_init).reshape(1)
    gsum, gbc = _group_matrices(D)
    n_groups = D // HEAD_DIM
    k_p, v_p, k_s, v_s, qh, kh, vh, q_s = _qkv_proj(
        x2, row(g_kv), row(g_attn[0]), bf(w_k), bf(w_v), bf(w_q[0]),
        row(jnp.tile(g_kn, n_groups)), row(jnp.tile(g_qn[0], n_groups) * (HEAD_DIM ** -0.5 * LOG2E)),
        gsum, gbc, n_sample)
    out_scale = 1.0 - lambda_init
    g_sub = row(g_subln[0])
    score_bound = LOG2E * (math.sqrt(HEAD_DIM) * jnp.max(jnp.abs(g_qn[0])) * jnp.max(jnp.abs(g_kn))
                           + jnp.max(jnp.abs(rel_bias)))
    o_p = lax.cond(
        score_bound <= UNSHIFTED_SCORE_LIMIT,
        functools.partial(_attn_prompt, out_scale=out_scale, online=False),
        functools.partial(_attn_prompt, out_scale=out_scale, online=True),
        rel_bias, lam, qh, kh, vh, g_subln[0].reshape(V_DIM, 1))

    tab = jnp.repeat(rel_bias.T, 2 * Ls, axis=0) * LOG2E
    o_s = _attn_sample(lam, q_s, cache_k, cache_v, k_s, v_s, tab, g_sub, out_scale)

    wr = jnp.pad(w_router[0], ((0, 0), (0, LANE - N_EXPERTS)))
    wr_hi = bf(wr)
    wr_lo = bf(wr - wr_hi.astype(F32))
    x3, h_moe, route, route_t = _oproj_router(o_p, o_s, x2, bf(w_o[0]), row(g_ffn[1]), wr_hi, wr_lo)
    off, cnt, tile_expert, n_active, n_rows = _routing_tables(route, ROW_TILE)
    ltri = jnp.asarray(np.tril(np.ones((ROW_TILE, ROW_TILE), np.float32)), BF16)
    xs = _dispatch(off, cnt, h_moe, route_t, ltri.T, n_rows)
    ys = _experts(tile_expert, n_active, xs, bf(w_gate_moe[0]), bf(w_up_moe[0]), bf(w_down_moe[0]))
    y_p, y_s = _combine(off, cnt, x3, route, ltri, ys, n_sample)

    return (y_p.reshape(Bp, Lp, D), y_s.reshape(Bs, Ls, D),
            k_p.reshape(Bp, Lp, n_heads, 2, HEAD_DIM), v_p.reshape(Bp, Lp, n_heads, V_DIM),
            pool_p.reshape(1, Bp, POOL_STATE, D),
            k_s.reshape(Bs, Ls, n_heads, 2, HEAD_DIM), v_s.reshape(Bs, Ls, n_heads, V_DIM),
            pool_s.reshape(1, Bs, POOL_STATE, D))
```

```python
import functools
import math

import numpy as np
import jax
import jax.numpy as jnp
from jax import lax
from jax.experimental import pallas as pl
from jax.experimental.pallas import tpu as pltpu

EPS = 1e-6
CHUNK = 64
POOL_WINDOWS = (2, 4, 8, 16)
POOL_STATE = max(POOL_WINDOWS) - 1
POOL_LEAD = 24
HEAD_DIM = 64
V_DIM = 2 * HEAD_DIM
N_EXPERTS = 8
TOP_K = 2
MAX_EXACT = 8
BUCKET_UPPER = (1, 2, 3, 4, 5, 6, 7, 8, 12, 16, 23, 32, 46, 64, 91)
FAR_DISTANCE = 128
NEG = -1e30
LOG2E = math.log2(math.e)
UNSHIFTED_SCORE_LIMIT = 80.0

ROW_TILE = 512
FAR_TILES_PER_TRIP = 8
LANE = 128
MXU_DIM = 256
VMEM_LIMIT = 56 * 1024 * 1024

F32 = jnp.float32
BF16 = jnp.bfloat16


def _dot(a, b):
    return jnp.dot(a, b, preferred_element_type=F32)


def _dot_nt(a, b):
    return lax.dot_general(a, b, (((1,), (1,)), ((), ())), preferred_element_type=F32)


def _rms_unit(x):
    return x * lax.rsqrt(jnp.mean(x * x, axis=-1, keepdims=True) + EPS)


def _rel_bias_tile(rel, tab):
    n = jnp.abs(rel)
    neg = tab(15)
    pos = tab(31)
    for b in range(14, -1, -1):
        lt = n < BUCKET_UPPER[b]
        neg = jnp.where(lt, tab(b), neg)
        pos = jnp.where(lt, tab(16 + b), pos)
    return jnp.where(rel > 0, pos, neg)


def _params(*sem):
    return pltpu.CompilerParams(dimension_semantics=sem, vmem_limit_bytes=VMEM_LIMIT)


def _pool_kernel(xp_ref, xs_ref, st_ref, g_ref, w_ref, sc_ref,
                 x1_ref, pp_ref, ps_ref, ext_ref, lva_ref, lvb_ref, ext3_ref, *, past_len):
    i = pl.program_id(0)
    nt = pl.num_programs(0) - 1
    T, D = xp_ref.shape
    gw = D // len(POOL_WINDOWS)

    @pl.when(i < nt)
    def _prompt():
        x = xp_ref[...]
        h = _rms_unit(x) * g_ref[...]

        @pl.when(i == 0)
        def _():
            ext_ref[0:POOL_LEAD, :] = jnp.zeros((POOL_LEAD, D), F32)
            for buf in (lva_ref, lvb_ref):
                buf[0:8, :] = jnp.zeros((8, gw), F32)

        ext_ref[POOL_LEAD:POOL_LEAD + T, :] = h
        row = i * T + lax.broadcasted_iota(jnp.int32, (T, 1), 0)
        parts = []
        for gi, w in enumerate(POOL_WINDOWS):
            c0 = gi * gw
            read = lambda lo, n: ext_ref[lo:lo + n, c0:c0 + gw]
            n_levels = w.bit_length() - 1
            for k in range(n_levels):
                d = 1 << k
                if k == n_levels - 1:
                    s = read(POOL_LEAD, T) + read(POOL_LEAD - d, T)
                else:
                    buf = (lva_ref, lvb_ref)[k % 2]
                    buf[8:POOL_LEAD + T, :] = read(8, 16 + T) + read(8 - d, 16 + T)
                    read = lambda lo, n, buf=buf: buf[lo:lo + n, :]
            cnt = jnp.minimum(w, row + 1).astype(F32)
            pooled = s / cnt - h[:, c0:c0 + gw]
            parts.append(_dot(pooled.astype(BF16), w_ref[gi]))
        mix = jnp.concatenate(parts, axis=-1) * sc_ref[...]
        x1_ref[...] = x + mix
        tail = ext_ref[T + 8:T + POOL_LEAD, :]
        ext_ref[8:POOL_LEAD, :] = tail

        @pl.when(i == nt - 1)
        def _():
            pp_ref[...] = tail[1:16, :]

    @pl.when(i == nt)
    def _sample():
        B = st_ref.shape[0]
        L = T // B
        x = xs_ref[...]
        h = _rms_unit(x) * g_ref[...]
        ext3_ref[:, 1:16, :] = st_ref[...]
        ext3_ref[:, 16:16 + L, :] = h.reshape(B, L, D)
        t = lax.broadcasted_iota(jnp.int32, (1, L, 1), 1)
        parts = []
        for gi, w in enumerate(POOL_WINDOWS):
            c0 = gi * gw
            s = ext3_ref[:, 16:16 + L, c0:c0 + gw]
            for j in range(1, w):
                s = s + ext3_ref[:, 16 - j:16 - j + L, c0:c0 + gw]
            cnt = jnp.minimum(w, past_len + t + 1).astype(F32)
            pooled = (s / cnt).reshape(T, gw) - h[:, c0:c0 + gw]
            parts.append(_dot(pooled.astype(BF16), w_ref[gi]))
        mix = jnp.concatenate(parts, axis=-1) * sc_ref[...]
        x1_ref[...] = x + mix
        ps_ref[...] = ext3_ref[:, 16 + L - POOL_STATE:16 + L, :]


def _pool_layer(xp, xs, state, g, w, sc, past_len):
    Lp, D = xp.shape
    T = ROW_TILE
    nt = Lp // T
    B = state.shape[0]
    L = xs.shape[0] // B
    return pl.pallas_call(
        functools.partial(_pool_kernel, past_len=past_len),
        grid=(nt + 1,),
        in_specs=[
            pl.BlockSpec((T, D), lambda i: (jnp.minimum(i, nt - 1), 0)),
            pl.BlockSpec((T, D), lambda i: (0, 0)),
            pl.BlockSpec((B, POOL_STATE, D), lambda i: (0, 0, 0)),
            pl.BlockSpec((1, D), lambda i: (0, 0)),
            pl.BlockSpec(w.shape, lambda i: (0, 0, 0)),
            pl.BlockSpec((1, D), lambda i: (0, 0)),
        ],
        out_specs=[
            pl.BlockSpec((T, D), lambda i: (i, 0)),
            pl.BlockSpec((POOL_STATE, D), lambda i: (0, 0)),
            pl.BlockSpec((B, POOL_STATE, D), lambda i: (0, 0, 0)),
        ],
        out_shape=[
            jax.ShapeDtypeStruct((Lp + T, D), F32),
            jax.ShapeDtypeStruct((POOL_STATE, D), F32),
            jax.ShapeDtypeStruct((B, POOL_STATE, D), F32),
        ],
        scratch_shapes=[
            pltpu.VMEM((POOL_LEAD + T, D), F32),
            pltpu.VMEM((POOL_LEAD + T, D // len(POOL_WINDOWS)), F32),
            pltpu.VMEM((POOL_LEAD + T, D // len(POOL_WINDOWS)), F32),
            pltpu.VMEM((B, 16 + L, D), F32),
        ],
        compiler_params=_params("arbitrary"),
        name="pool_mixer",
    )(xp, xs, state, g, w, sc)


def _swiglu_tile(hb, wg_ref, wu_ref, wd_ref):
    tf = wg_ref.shape[1]
    step = 3 * MXU_DIM
    out = None
    for c0 in range(0, tf, step):
        c1 = min(c0 + step, tf)
        gt = _dot(hb, wg_ref[:, c0:c1])
        ut = _dot(hb, wu_ref[:, c0:c1])
        a = gt * jax.nn.sigmoid(gt) * ut
        part = _dot(a.astype(BF16), wd_ref[c0:c1, :])
        out = part if out is None else out + part
    return out


def _ffn_kernel(x_ref, g_ref, wg_ref, wu_ref, wd_ref, out_ref, hb_ref, acc_ref):
    f = pl.program_id(1)

    @pl.when(f == 0)
    def _():
        x = x_ref[...]
        hb_ref[...] = (_rms_unit(x) * g_ref[...]).astype(BF16)
        acc_ref[...] = x

    acc_ref[...] += _swiglu_tile(hb_ref[...], wg_ref, wu_ref, wd_ref)

    @pl.when(f == pl.num_programs(1) - 1)
    def _():
        out_ref[...] = acc_ref[...]


def _ffn_splits(d_ff):
    nf = 1
    return nf, d_ff // nf


def _dense_ffn(x, g, wg, wu, wd):
    N, D = x.shape
    T = ROW_TILE
    nf, tf = _ffn_splits(wg.shape[1])
    return pl.pallas_call(
        _ffn_kernel,
        grid=(N // T, nf),
        in_specs=[
            pl.BlockSpec((T, D), lambda i, f: (i, 0)),
            pl.BlockSpec((1, D), lambda i, f: (0, 0)),
            pl.BlockSpec((D, tf), lambda i, f: (0, f)),
            pl.BlockSpec((D, tf), lambda i, f: (0, f)),
            pl.BlockSpec((tf, D), lambda i, f: (f, 0)),
        ],
        out_specs=pl.BlockSpec((T, D), lambda i, f: (i, 0)),
        out_shape=jax.ShapeDtypeStruct((N, D), F32),
        scratch_shapes=[pltpu.VMEM((T, D), BF16), pltpu.VMEM((T, D), F32)],
        compiler_params=_params("arbitrary", "arbitrary"),
        name="dense_swiglu",
    )(x, g, wg, wu, wd)


def _qkv_kernel(x_ref, gkv_ref, gq_ref, wk_ref, wv_ref, wq_ref, gkn_ref, gqn_ref,
                gsum_ref, gbc_ref,
                kp_ref, vp_ref, ks_ref, vs_ref, qh_ref, kh_ref, vh_ref, qs_ref):
    i = pl.program_id(0)
    nt = pl.num_programs(0) - 1
    n_heads = qh_ref.shape[0]

    xn = _rms_unit(x_ref[...])
    hkv = (xn * gkv_ref[...]).astype(BF16)
    hq = (xn * gq_ref[...]).astype(BF16)

    def head_norm(y, g):
        ssq = _dot((y * y).astype(BF16), gsum_ref[...])
        rs = lax.rsqrt(ssq * (1.0 / HEAD_DIM) + EPS)
        rs_hi = rs.astype(BF16)
        rs_lo = (rs - rs_hi.astype(F32)).astype(BF16)
        rsb = _dot(rs_hi, gbc_ref[...]) + _dot(rs_lo, gbc_ref[...])
        return y * rsb * g

    k = head_norm(_dot(hkv, wk_ref[...]), gkn_ref[...])
    v = _dot(hkv, wv_ref[...])
    q = head_norm(_dot(hq, wq_ref[...]), gqn_ref[...])

    @pl.when(i < nt)
    def _():
        kp_ref[...] = k
        vp_ref[...] = v
        for h in range(n_heads):
            sl = slice(h * V_DIM, (h + 1) * V_DIM)
            qh_ref[h] = q[:, sl].T.astype(BF16)
            kh_ref[h] = k[:, sl].astype(BF16)
            vh_ref[h] = v[:, sl].T.astype(BF16)

    @pl.when(i == nt)
    def _():
        ks_ref[...] = k
        vs_ref[...] = v
        qs_ref[...] = q.astype(BF16)


def _qkv_proj(x, gkv, gq, wk, wv, wq, gkn_t, gqn_t, gsum, gbc, n_sample):
    N, D = x.shape
    T = ROW_TILE
    nt = N // T - 1
    Lp = N - n_sample
    n_heads = D // V_DIM
    const2 = lambda i: (0, 0)
    prow = lambda i: (jnp.minimum(i, nt - 1), 0)
    phead = lambda i: (0, jnp.minimum(i, nt - 1), 0)
    pheadt = lambda i: (0, 0, jnp.minimum(i, nt - 1))
    return pl.pallas_call(
        _qkv_kernel,
        grid=(nt + 1,),
        in_specs=[
            pl.BlockSpec((T, D), lambda i: (i, 0)),
            pl.BlockSpec((1, D), const2),
            pl.BlockSpec((1, D), const2),
            pl.BlockSpec((D, D), const2),
            pl.BlockSpec((D, D), const2),
            pl.BlockSpec((D, D), const2),
            pl.BlockSpec((1, D), const2),
            pl.BlockSpec((1, D), const2),
            pl.BlockSpec(gsum.shape, const2),
            pl.BlockSpec(gbc.shape, const2),
        ],
        out_specs=[
            pl.BlockSpec((T, D), prow),
            pl.BlockSpec((T, D), prow),
            pl.BlockSpec((T, D), const2),
            pl.BlockSpec((T, D), const2),
            pl.BlockSpec((n_heads, V_DIM, T), pheadt),
            pl.BlockSpec((n_heads, T, V_DIM), phead),
            pl.BlockSpec((n_heads, V_DIM, T), pheadt),
            pl.BlockSpec((T, D), const2),
        ],
        out_shape=[
            jax.ShapeDtypeStruct((Lp, D), F32),
            jax.ShapeDtypeStruct((Lp, D), F32),
            jax.ShapeDtypeStruct((n_sample, D), F32),
            jax.ShapeDtypeStruct((n_sample, D), F32),
            jax.ShapeDtypeStruct((n_heads, V_DIM, Lp), BF16),
            jax.ShapeDtypeStruct((n_heads, Lp, V_DIM), BF16),
            jax.ShapeDtypeStruct((n_heads, V_DIM, Lp), BF16),
            jax.ShapeDtypeStruct((n_sample, D), BF16),
        ],
        compiler_params=_params("arbitrary"),
        name="qkv_proj",
    )(x, gkv, gq, wk, wv, wq, gkn_t, gqn_t, gsum, gbc)


def _diff_out(o0, o1, lam, g, out_scale):
    o = o0 - lam * o1
    return _rms_unit(o) * g * out_scale


def _sublane_partial_sum(p):
    return jnp.sum(p.reshape(p.shape[0] // 8, 8, p.shape[1]), axis=0)


def _attn_prompt_kernel(bias_ref, lam_ref, qt_ref, k_ref, vt_ref, g_ref, o_ref,
                        bn_ref, l_ref, acc_ref, *m_scratch, out_scale):
    h = pl.program_id(0)
    i = pl.program_id(1)
    T = qt_ref.shape[2]
    online = bool(m_scratch)

    @pl.when(i == 0)
    def _():
        key = lax.broadcasted_iota(jnp.int32, (T, T), 0)
        qry = lax.broadcasted_iota(jnp.int32, (T, T), 1)
        tab = lambda b: bias_ref[b, h] * LOG2E
        visible = (key // CHUNK) <= (qry // CHUNK)
        bn_ref[0:T, :] = _rel_bias_tile(key - qry - T, tab)
        bn_ref[T:2 * T, :] = jnp.where(visible, _rel_bias_tile(key - qry, tab), NEG)

    if online:
        m_ref, = m_scratch
        m_ref[...] = jnp.full(m_ref.shape, NEG, F32)
    l_ref[...] = jnp.zeros(l_ref.shape, F32)
    acc_ref[...] = jnp.zeros(acc_ref.shape, F32)
    qt = qt_ref[0]
    dim = lax.broadcasted_iota(jnp.int32, qt.shape, 0)
    zero = jnp.zeros_like(qt)
    qc = (jnp.where(dim < HEAD_DIM, qt, zero), jnp.where(dim >= HEAD_DIM, qt, zero))

    def update(j, bias, width=T):
        start = pl.multiple_of(j * T, T)
        kt = k_ref[0, pl.ds(start, width), :]
        vt = vt_ref[0, :, pl.ds(start, width)]
        for c in range(2):
            s = _dot(kt, qc[c]) + bias
            if online:
                m_old = m_ref[c]
                m_new = jnp.maximum(m_old, jnp.max(s, axis=0, keepdims=True))
                alpha = jnp.exp2(m_old - m_new)
                p = jnp.exp2(s - m_new)
                l_ref[c] = alpha * l_ref[c] + _sublane_partial_sum(p)
                acc_ref[c] = alpha * acc_ref[c] + _dot(vt, p.astype(BF16))
                m_ref[c] = m_new
            else:
                p = jnp.exp2(s)
                l_ref[c] += _sublane_partial_sum(p)
                acc_ref[c] += _dot(vt, p.astype(BF16))

    far_bias = bias_ref[15, h] * LOG2E

    n_far = jnp.maximum(i - 1, 0)

    def far_body(jj, carry):
        update(FAR_TILES_PER_TRIP * jj, far_bias, width=FAR_TILES_PER_TRIP * T)
        return carry

    n_trips = n_far // FAR_TILES_PER_TRIP
    lax.fori_loop(0, n_trips, far_body, 0)
    done = n_trips * FAR_TILES_PER_TRIP
    step = FAR_TILES_PER_TRIP // 2
    while step >= 1:
        @pl.when(((n_far - done) & step) != 0)
        def _(done=done, step=step):
            update(done, far_bias, width=step * T)

        done = done + ((n_far - done) & step)
        step //= 2

    @pl.when(i >= 1)
    def _():
        update(i - 1, bn_ref[...], width=2 * T)

    @pl.when(i == 0)
    def _():
        update(0, bn_ref[T:2 * T, :])

    o0 = acc_ref[0] / jnp.sum(l_ref[0], axis=0, keepdims=True)
    o1 = acc_ref[1] / jnp.sum(l_ref[1], axis=0, keepdims=True)
    o = o0 - lam_ref[0] * o1
    y = o * lax.rsqrt(jnp.mean(o * o, axis=0, keepdims=True) + EPS) * g_ref[...] * out_scale
    o_ref[...] = y.T.astype(o_ref.dtype)


def _attn_prompt(rel_bias, lam, qth, kh, vth, g_subln, out_scale, online):
    n_heads, Lp, _ = kh.shape
    T = ROW_TILE
    assert T % CHUNK == 0 and T >= FAR_DISTANCE and FAR_TILES_PER_TRIP & (FAR_TILES_PER_TRIP - 1) == 0
    m_scratch = [pltpu.VMEM((2, 1, T), F32)] if online else []
    return pl.pallas_call(
        functools.partial(_attn_prompt_kernel, out_scale=out_scale),
        grid=(n_heads, Lp // T),
        in_specs=[
            pl.BlockSpec(memory_space=pltpu.SMEM),
            pl.BlockSpec(memory_space=pltpu.SMEM),
            pl.BlockSpec((1, V_DIM, T), lambda h, i: (h, 0, i)),
            pl.BlockSpec((1, Lp, V_DIM), lambda h, i: (h, 0, 0)),
            pl.BlockSpec((1, V_DIM, Lp), lambda h, i: (h, 0, 0)),
            pl.BlockSpec((V_DIM, 1), lambda h, i: (0, 0)),
        ],
        out_specs=pl.BlockSpec((T, V_DIM), lambda h, i: (i, h)),
        out_shape=jax.ShapeDtypeStruct((Lp, n_heads * V_DIM), BF16),
        scratch_shapes=[
            pltpu.VMEM((2 * T, T), F32),
            pltpu.VMEM((2, 8, T), F32),
            pltpu.VMEM((2, V_DIM, T), F32),
        ] + m_scratch,
        compiler_params=_params("arbitrary", "arbitrary"),
        name="attn_prompt_online" if online else "attn_prompt",
    )(rel_bias, lam, qth, kh, vth, g_subln)


def _attn_sample_kernel(lam_ref, q_ref, ckt_ref, cv_ref, kn_ref, vn_ref, tab_ref, g_ref,
                        o_ref, nearb_ref, newb_ref, qbd_ref, m_ref, l_ref, acc_ref, *, out_scale):
    b = pl.program_id(0)
    kb = pl.program_id(1)
    nkb = pl.num_programs(1)
    R, Tk = nearb_ref.shape
    L = kn_ref.shape[0]
    D = q_ref.shape[1]
    n_heads = D // V_DIM

    @pl.when((b == 0) & (kb == 0))
    def _():
        tab = lambda bkt: tab_ref[:, bkt:bkt + 1]
        t_near = lax.broadcasted_iota(jnp.int32, (R, Tk), 0) % L
        col = lax.broadcasted_iota(jnp.int32, (R, Tk), 1)
        nearb_ref[...] = _rel_bias_tile(col - Tk - t_near, tab)
        t_new = lax.broadcasted_iota(jnp.int32, (R, L), 0) % L
        col_new = lax.broadcasted_iota(jnp.int32, (R, L), 1)
        newb_ref[...] = _rel_bias_tile(col_new - t_new, tab)

    @pl.when(kb == 0)
    def _():
        m_ref[...] = jnp.full(m_ref.shape, NEG, F32)
        l_ref[...] = jnp.zeros(l_ref.shape, F32)
        acc_ref[...] = jnp.zeros(acc_ref.shape, F32)
        q_rows = jnp.concatenate([q_ref[...]] * (R // L), axis=0)
        row_group = lax.broadcasted_iota(jnp.int32, (R, D), 0) // L
        col_group = lax.broadcasted_iota(jnp.int32, (R, D), 1) // HEAD_DIM
        qbd_ref[...] = jnp.where(row_group == col_group, q_rows, jnp.zeros_like(q_rows))

    rows_per_head = 2 * L

    def update(s, value_of_head):
        m_old = m_ref[...]
        m_new = jnp.maximum(m_old, jnp.max(s, axis=-1, keepdims=True))
        alpha = jnp.exp2(m_old - m_new)
        p = jnp.exp2(s - m_new)
        l_ref[...] = alpha * l_ref[...] + jnp.sum(p, axis=-1, keepdims=True)
        pb = p.astype(BF16)
        for h in range(n_heads):
            rs = slice(h * rows_per_head, (h + 1) * rows_per_head)
            acc_ref[rs, :] = alpha[rs, :] * acc_ref[rs, :] + _dot(pb[rs, :], value_of_head(h))
        m_ref[...] = m_new

    qbd = qbd_ref[...]
    s = _dot(qbd, ckt_ref[0].astype(BF16))
    cache_value = lambda h: cv_ref[0, pl.ds(h, Tk, stride=n_heads), :].astype(BF16)

    @pl.when(kb < nkb - 1)
    def _():
        update(s + tab_ref[:, 15:16], cache_value)

    @pl.when(kb == nkb - 1)
    def _():
        update(s + nearb_ref[...], cache_value)
        s_new = _dot_nt(qbd, kn_ref[...].astype(BF16)) + newb_ref[...]
        update(s_new, lambda h: vn_ref[:, h * V_DIM:(h + 1) * V_DIM].astype(BF16))
        lam = lam_ref[0]
        o = acc_ref[...] / l_ref[...]
        for h in range(n_heads):
            r0 = h * rows_per_head
            o_ref[:, h * V_DIM:(h + 1) * V_DIM] = _diff_out(
                o[r0:r0 + L, :], o[r0 + L:r0 + 2 * L, :], lam, g_ref[...], out_scale).astype(o_ref.dtype)


def _attn_sample(lam, q, cache_k, cache_v, k_new, v_new, tab, g_subln, out_scale):
    B, past, n_heads = cache_v.shape[:3]
    D = n_heads * V_DIM
    L = k_new.shape[0] // B
    R = (D // HEAD_DIM) * L
    Tk = min(2048, past)
    assert past % Tk == 0 and Tk >= FAR_DISTANCE + L
    cache_kt = jnp.transpose(cache_k, (0, 2, 3, 4, 1)).reshape(B, D, past)
    return pl.pallas_call(
        functools.partial(_attn_sample_kernel, out_scale=out_scale),
        grid=(B, past // Tk),
        in_specs=[
            pl.BlockSpec(memory_space=pltpu.SMEM),
            pl.BlockSpec((L, D), lambda b, k: (b, 0)),
            pl.BlockSpec((1, D, Tk), lambda b, k: (b, 0, k)),
            pl.BlockSpec((1, Tk * n_heads, V_DIM), lambda b, k: (b, k, 0)),
            pl.BlockSpec((L, D), lambda b, k: (b, 0)),
            pl.BlockSpec((L, D), lambda b, k: (b, 0)),
            pl.BlockSpec(tab.shape, lambda b, k: (0, 0)),
            pl.BlockSpec((1, V_DIM), lambda b, k: (0, 0)),
        ],
        out_specs=pl.BlockSpec((L, D), lambda b, k: (b, 0)),
        out_shape=jax.ShapeDtypeStruct((B * L, D), BF16),
        scratch_shapes=[
            pltpu.VMEM((R, Tk), F32),
            pltpu.VMEM((R, L), F32),
            pltpu.VMEM((R, D), BF16),
            pltpu.VMEM((R, 1), F32),
            pltpu.VMEM((R, 1), F32),
            pltpu.VMEM((R, V_DIM), F32),
        ],
        compiler_params=_params("arbitrary", "arbitrary"),
        name="attn_sample",
    )(lam, q, cache_kt, cache_v.reshape(B, past * n_heads, V_DIM), k_new, v_new, tab, g_subln)


def _oproj_router_kernel(op_ref, os_ref, x_ref, wo_ref, g_ref, wrh_ref, wrl_ref,
                         x3_ref, h_ref, route_ref, route_t_ref):
    i = pl.program_id(0)
    nt = pl.num_programs(0) - 1
    o = jnp.where(i == nt, os_ref[...], op_ref[...])
    x3 = x_ref[...] + _dot(o, wo_ref[...])
    x3_ref[...] = x3
    h = _rms_unit(x3) * g_ref[...]
    h_hi = h.astype(BF16)
    h_ref[...] = h_hi
    h_lo = (h - h_hi.astype(F32)).astype(BF16)
    logits = _dot(h_hi, wrh_ref[...]) + (_dot(h_lo, wrh_ref[...]) + _dot(h_hi, wrl_ref[...]))
    lane = lax.broadcasted_iota(jnp.int32, logits.shape, 1)
    lg = jnp.where(lane < N_EXPERTS, logits, -jnp.inf)
    m1 = jnp.max(lg, axis=-1, keepdims=True)
    i1 = jnp.min(jnp.where(lg == m1, lane, LANE), axis=-1, keepdims=True)
    lg2 = jnp.where(lane == i1, -jnp.inf, lg)
    m2 = jnp.max(lg2, axis=-1, keepdims=True)
    i2 = jnp.min(jnp.where(lg2 == m2, lane, LANE), axis=-1, keepdims=True)
    e2 = jnp.exp(m2 - m1)
    den = 1.0 + e2
    g1 = 1.0 / den
    g2 = e2 / den
    route = jnp.where(lane == 0, i1.astype(F32),
                      jnp.where(lane == 1, i2.astype(F32),
                                jnp.where(lane == 2, g1,
                                          jnp.where(lane == 3, g2, 0.0))))
    route_ref[...] = route
    route_t_ref[...] = route.T


def _oproj_router(o_p, o_s, x, wo, g, wr_hi, wr_lo):
    N, D = x.shape
    T = ROW_TILE
    nt = N // T - 1
    const2 = lambda i: (0, 0)
    row = lambda i: (i, 0)
    return pl.pallas_call(
        _oproj_router_kernel,
        grid=(nt + 1,),
        in_specs=[
            pl.BlockSpec((T, D), lambda i: (jnp.minimum(i, nt - 1), 0)),
            pl.BlockSpec((T, D), const2),
            pl.BlockSpec((T, D), row),
            pl.BlockSpec((D, D), const2),
            pl.BlockSpec((1, D), const2),
            pl.BlockSpec((D, LANE), const2),
            pl.BlockSpec((D, LANE), const2),
        ],
        out_specs=[
            pl.BlockSpec((T, D), row),
            pl.BlockSpec((T, D), row),
            pl.BlockSpec((T, LANE), row),
            pl.BlockSpec((LANE, T), lambda i: (0, i)),
        ],
        out_shape=[
            jax.ShapeDtypeStruct((N, D), F32),
            jax.ShapeDtypeStruct((N, D), BF16),
            jax.ShapeDtypeStruct((N, LANE), F32),
            jax.ShapeDtypeStruct((LANE, N), F32),
        ],
        compiler_params=_params("arbitrary"),
        name="oproj_router",
    )(o_p, o_s, x, wo, g, wr_hi, wr_lo)


MOE_CHUNK = 128
MOE_ROW_ALIGN = 16


def _for_each_chunk(cnt_ref, step, n_chunks, fn, first=0):
    for e in range(N_EXPERTS):
        for k in range(first, n_chunks):
            if k == 0:
                fn(e, k)
            else:
                @pl.when(cnt_ref[step * N_EXPERTS + e] > k * MOE_CHUNK)
                def _(e=e, k=k):
                    fn(e, k)


def _dispatch_kernel(off_ref, cnt_ref, hb_ref, rt_ref, utri_ref, xs_in_ref, xs_ref, stage_ref, sem):
    del xs_in_ref
    t = pl.program_id(0)
    T = hb_ref.shape[0]
    S = MOE_CHUNK
    nk = T // S

    def chunk_copy(step, e, k):
        start = pl.multiple_of(off_ref[step * N_EXPERTS + e] + k * S, MOE_ROW_ALIGN)
        slot = k * N_EXPERTS + e
        return pltpu.make_async_copy(stage_ref.at[slot], xs_ref.at[pl.ds(start, S), :], sem.at[slot])

    rt = rt_ref[...]
    expert_id = lax.broadcasted_iota(jnp.int32, (N_EXPERTS, T), 0).astype(F32)
    member = (rt[0:1, :] == expert_id) | (rt[1:2, :] == expert_id)
    rank = _dot(member.astype(BF16), utri_ref[...])
    rank = jnp.where(member, rank, 0.0)
    hb = hb_ref[...]

    row3 = lax.broadcasted_iota(jnp.int32, (1, S, 1), 1)
    sel_first = (rank[:, None, :] == (row3 + 1).astype(F32)).astype(BF16).reshape(N_EXPERTS * S, T)
    first = _dot(sel_first, hb).astype(BF16).reshape(N_EXPERTS, S, hb.shape[1])

    @pl.when(t > 0)
    def _():
        _for_each_chunk(cnt_ref, t - 1, nk, lambda e, k: chunk_copy(t - 1, e, k).wait())

    stage_ref[0:N_EXPERTS] = first
    for e in range(N_EXPERTS):
        chunk_copy(t, e, 0).start()

    row = lax.broadcasted_iota(jnp.int32, (S, 1), 0)

    def emit(e, k):
        sel = rank[e:e + 1, :] == (row + (k * S + 1)).astype(F32)
        stage_ref[k * N_EXPERTS + e] = _dot(sel.astype(BF16), hb).astype(BF16)
        chunk_copy(t, e, k).start()

    _for_each_chunk(cnt_ref, t, nk, emit, first=1)

    @pl.when(t == pl.num_programs(0) - 1)
    def _():
        _for_each_chunk(cnt_ref, t, nk, lambda e, k: chunk_copy(t, e, k).wait())


def _dispatch(off, cnt, hb, route_t, utri, n_rows):
    N, D = hb.shape
    T = ROW_TILE
    nk = T // MOE_CHUNK
    n_slots = N_EXPERTS * nk
    xs_init = jnp.zeros((n_rows, D), BF16)
    return pl.pallas_call(
        _dispatch_kernel,
        grid_spec=pltpu.PrefetchScalarGridSpec(
            num_scalar_prefetch=2,
            grid=(N // T,),
            in_specs=[
                pl.BlockSpec((T, D), lambda t, off, cnt: (t, 0)),
                pl.BlockSpec((LANE, T), lambda t, off, cnt: (0, t)),
                pl.BlockSpec((T, T), lambda t, off, cnt: (0, 0)),
                pl.BlockSpec(memory_space=pl.ANY),
            ],
            out_specs=pl.BlockSpec(memory_space=pl.ANY),
            scratch_shapes=[
                pltpu.VMEM((n_slots, MOE_CHUNK, D), BF16),
                pltpu.SemaphoreType.DMA((n_slots,)),
            ],
        ),
        out_shape=jax.ShapeDtypeStruct((n_rows, D), BF16),
        input_output_aliases={5: 0},
        compiler_params=_params("arbitrary"),
        name="moe_dispatch",
    )(off, cnt, hb, route_t, utri, xs_init)


def _experts_kernel(te_ref, na_ref, xs_ref, wg_ref, wu_ref, wd_ref, out_ref, acc_ref):
    i = pl.program_id(0)
    f = pl.program_id(1)
    last = pl.num_programs(1) - 1
    active = i < na_ref[0]

    @pl.when(active)
    def _():
        @pl.when(f == 0)
        def _():
            acc_ref[...] = jnp.zeros(acc_ref.shape, F32)

        acc_ref[...] += _swiglu_tile(xs_ref[...], wg_ref.at[0], wu_ref.at[0], wd_ref.at[0])

        @pl.when(f == last)
        def _():
            out_ref[...] = acc_ref[...].astype(out_ref.dtype)

    @pl.when(jnp.logical_not(active) & (f == last))
    def _():
        out_ref[...] = jnp.zeros(out_ref.shape, out_ref.dtype)


def _experts(tile_expert, n_active, xs, wg, wu, wd):
    P, D = xs.shape
    T = ROW_TILE
    nf, tf = _ffn_splits(wg.shape[2])
    fidx = lambda i, f, na: jnp.where(i < na[0], f, nf - 1)
    return pl.pallas_call(
        _experts_kernel,
        grid_spec=pltpu.PrefetchScalarGridSpec(
            num_scalar_prefetch=2,
            grid=(P // T, nf),
            in_specs=[
                pl.BlockSpec((T, D), lambda i, f, te, na: (i, 0)),
                pl.BlockSpec((1, D, tf), lambda i, f, te, na: (te[i], 0, fidx(i, f, na))),
                pl.BlockSpec((1, D, tf), lambda i, f, te, na: (te[i], 0, fidx(i, f, na))),
                pl.BlockSpec((1, tf, D), lambda i, f, te, na: (te[i], fidx(i, f, na), 0)),
            ],
            out_specs=pl.BlockSpec((T, D), lambda i, f, te, na: (i, 0)),
            scratch_shapes=[pltpu.VMEM((T, D), F32)],
        ),
        out_shape=jax.ShapeDtypeStruct((P, D), BF16),
        compiler_params=_params("arbitrary", "arbitrary"),
        name="moe_experts",
    )(tile_expert, n_active, xs, wg, wu, wd)


def _combine_kernel(off_ref, cnt_ref, x_ref, route_ref, ltri_ref, ys_hbm, yp_ref, ysm_ref,
                    buf_ref, acc_ref, sem):
    t = pl.program_id(0)
    nt = pl.num_programs(0) - 1
    T = x_ref.shape[0]
    S = MOE_CHUNK
    nk = T // S

    def chunk_copy(step, e, k):
        start = pl.multiple_of(off_ref[step * N_EXPERTS + e] + k * S, MOE_ROW_ALIGN)
        bank, slot = step % 2, k * N_EXPERTS + e
        return pltpu.make_async_copy(ys_hbm.at[pl.ds(start, S), :], buf_ref.at[bank, slot], sem.at[bank, slot])

    @pl.when(t == 0)
    def _():
        _for_each_chunk(cnt_ref, t, nk, lambda e, k: chunk_copy(t, e, k).start())

    @pl.when(t < nt)
    def _():
        _for_each_chunk(cnt_ref, t + 1, nk, lambda e, k: chunk_copy(t + 1, e, k).start())

    route = route_ref[...]
    e1, e2, g1, g2 = (route[:, j:j + 1] for j in range(2 * TOP_K))
    lane = lax.broadcasted_iota(jnp.int32, route.shape, 1).astype(F32)
    member = (lane == e1) | (lane == e2)
    rank = _dot(ltri_ref[...], member.astype(BF16))
    rank = jnp.where(member, rank, 0.0)
    col = lax.broadcasted_iota(jnp.int32, (1, S), 1)
    gate_of = lambda e: jnp.where(e1 == e, g1, 0.0) + jnp.where(e2 == e, g2, 0.0)
    select = lambda e, k: rank[:, e:e + 1] == (col + (k * S + 1)).astype(F32)

    for e in range(N_EXPERTS):
        chunk_copy(t, e, 0).wait()
    weight = jnp.concatenate([jnp.where(select(e, 0), gate_of(e), 0.0) for e in range(N_EXPERTS)], axis=-1)
    w_hi = weight.astype(BF16)
    w_lo = (weight - w_hi.astype(F32)).astype(BF16)
    rows = buf_ref[t % 2, 0:N_EXPERTS].reshape(N_EXPERTS * S, x_ref.shape[1])
    acc_ref[...] = x_ref[...] + (_dot(w_hi, rows) + _dot(w_lo, rows))

    def absorb(e, k):
        chunk_copy(t, e, k).wait()
        acc_ref[...] += gate_of(e) * _dot(select(e, k).astype(BF16), buf_ref[t % 2, k * N_EXPERTS + e])

    _for_each_chunk(cnt_ref, t, nk, absorb, first=1)

    @pl.when(t < nt)
    def _():
        yp_ref[...] = acc_ref[...]

    @pl.when(t == nt)
    def _():
        ysm_ref[...] = acc_ref[...]


def _combine(off, cnt, x3, route, ltri, ys, n_sample):
    N, D = x3.shape
    T = ROW_TILE
    nt = N // T - 1
    n_slots = N_EXPERTS * (T // MOE_CHUNK)
    return pl.pallas_call(
        _combine_kernel,
        grid_spec=pltpu.PrefetchScalarGridSpec(
            num_scalar_prefetch=2,
            grid=(nt + 1,),
            in_specs=[
                pl.BlockSpec((T, D), lambda t, off, cnt: (t, 0)),
                pl.BlockSpec((T, LANE), lambda t, off, cnt: (t, 0)),
                pl.BlockSpec((T, T), lambda t, off, cnt: (0, 0)),
                pl.BlockSpec(memory_space=pl.ANY),
            ],
            out_specs=[
                pl.BlockSpec((T, D), lambda t, off, cnt: (jnp.minimum(t, nt - 1), 0)),
                pl.BlockSpec((T, D), lambda t, off, cnt: (0, 0)),
            ],
            scratch_shapes=[
                pltpu.VMEM((2, n_slots, MOE_CHUNK, D), BF16),
                pltpu.VMEM((T, D), F32),
                pltpu.SemaphoreType.DMA((2, n_slots)),
            ],
        ),
        out_shape=[
            jax.ShapeDtypeStruct((N - n_sample, D), F32),
            jax.ShapeDtypeStruct((n_sample, D), F32),
        ],
        compiler_params=_params("arbitrary"),
        name="moe_combine",
    )(off, cnt, x3, route, ltri, ys)


def _routing_tables(route, tile):
    n = route.shape[0]
    n_tok_tiles = n // tile
    experts = route[:, :TOP_K].astype(jnp.int32)
    onehot = (experts[:, :, None] == jnp.arange(N_EXPERTS)[None, None, :]).astype(jnp.int32).sum(axis=1)
    cnt = onehot.reshape(n_tok_tiles, tile, N_EXPERTS).sum(axis=1)
    span = (cnt + MOE_ROW_ALIGN - 1) // MOE_ROW_ALIGN * MOE_ROW_ALIGN
    totals = span.sum(axis=0)
    tiles_per = (totals + MOE_CHUNK + tile - 1) // tile
    tile_end = jnp.cumsum(tiles_per)
    group_off = (tile_end - tiles_per) * tile
    off = group_off[None, :] + jnp.cumsum(span, axis=0) - span
    max_rows = n * TOP_K + N_EXPERTS * (MOE_CHUNK + n_tok_tiles * (MOE_ROW_ALIGN - 1))
    n_row_tiles = max_rows // tile + N_EXPERTS
    n_active = tile_end[-1:].astype(jnp.int32)
    tile_ids = jnp.minimum(jnp.arange(n_row_tiles, dtype=jnp.int32), n_active[0] - 1)
    tile_expert = jnp.sum((tile_ids[:, None] >= tile_end[None, :]).astype(jnp.int32), axis=1)
    return (off.reshape(-1).astype(jnp.int32), cnt.reshape(-1).astype(jnp.int32),
            tile_expert, n_active, n_row_tiles * tile)


def _group_matrices(d_model):
    n_groups = d_model // HEAD_DIM
    gsum = np.zeros((d_model, LANE), np.float32)
    gsum[np.arange(d_model), np.arange(d_model) // HEAD_DIM] = 1.0
    assert n_groups <= LANE
    return jnp.asarray(gsum, BF16), jnp.asarray(gsum.T.copy(), BF16)


def kernel(x_prompt, x_sample, cache_k, cache_v, state_pool, g_pool_norm, w_pool, pool_scale, g_attn, w_q, g_qn, lambda_q1, lambda_k1, lambda_q2, lambda_k2, g_subln, w_o, g_kv, w_k, w_v, g_kn, rel_bias, g_ffn, w_gate_dense, w_up_dense, w_down_dense, w_router, w_gate_moe, w_up_moe, w_down_moe):
    Bp, Lp, D = x_prompt.shape
    Bs, Ls, _ = x_sample.shape
    past = cache_k.shape[1]
    n_heads = D // V_DIM
    n_sample = Bs * Ls
    assert Bp == 1 and n_sample == ROW_TILE and Lp % ROW_TILE == 0
    assert g_pool_norm.shape[0] == 1 and g_attn.shape[0] == 1
    bf = lambda a: a.astype(BF16)
    row = lambda a: a.reshape(1, -1)

    x1, pool_p, pool_s = _pool_layer(
        x_prompt.reshape(Lp, D), x_sample.reshape(n_sample, D), state_pool[0],
        row(g_pool_norm[0]), bf(w_pool[0]), row(pool_scale[0]), past)
    x2 = _dense_ffn(x1, row(g_ffn[0]), bf(w_gate_dense[0]), bf(w_up_dense[0]), bf(w_down_dense[0]))

    layer = 1
    lambda_init = 0.8 - 0.6 * math.exp(-0.3 * layer)
    lam = (jnp.exp(jnp.sum(lambda_q1[0] * lambda_k1[0])) - jnp.exp(jnp.sum(lambda_q2[0] * lambda_k2[0]))
           + lambda_init).reshape(1)
    gsum, gbc = _group_matrices(D)
    n_groups = D // HEAD_DIM
    k_p, v_p, k_s, v_s, qh, kh, vh, q_s = _qkv_proj(
        x2, row(g_kv), row(g_attn[0]), bf(w_k), bf(w_v), bf(w_q[0]),
        row(jnp.tile(g_kn, n_groups)), row(jnp.tile(g_qn[0], n_groups) * (HEAD_DIM ** -0.5 * LOG2E)),
        gsum, gbc, n_sample)
    out_scale = 1.0 - lambda_init
    g_sub = row(g_subln[0])
    score_bound = LOG2E * (math.sqrt(HEAD_DIM) * jnp.max(jnp.abs(g_qn[0])) * jnp.max(jnp.abs(g_kn))
                           + jnp.max(jnp.abs(rel_bias)))
    o_p = lax.cond(
        score_bound <= UNSHIFTED_SCORE_LIMIT,
        functools.partial(_attn_prompt, out_scale=out_scale, online=False),
        functools.partial(_attn_prompt, out_scale=out_scale, online=True),
        rel_bias, lam, qh, kh, vh, g_subln[0].reshape(V_DIM, 1))

    tab = jnp.repeat(rel_bias.T, 2 * Ls, axis=0) * LOG2E
    o_s = _attn_sample(lam, q_s, cache_k, cache_v, k_s, v_s, tab, g_sub, out_scale)

    wr = jnp.pad(w_router[0], ((0, 0), (0, LANE - N_EXPERTS)))
    wr_hi = bf(wr)
    wr_lo = bf(wr - wr_hi.astype(F32))
    x3, h_moe, route, route_t = _oproj_router(o_p, o_s, x2, bf(w_o[0]), row(g_ffn[1]), wr_hi, wr_lo)
    off, cnt, tile_expert, n_active, n_rows = _routing_tables(route, ROW_TILE)
    ltri = jnp.asarray(np.tril(np.ones((ROW_TILE, ROW_TILE), np.float32)), BF16)
    xs = _dispatch(off, cnt, h_moe, route_t, ltri.T, n_rows)
    ys = _experts(tile_expert, n_active, xs, bf(w_gate_moe[0]), bf(w_up_moe[0]), bf(w_down_moe[0]))
    y_p, y_s = _combine(off, cnt, x3, route, ltri, ys, n_sample)

    return (y_p.reshape(Bp, Lp, D), y_s.reshape(Bs, Ls, D),
            k_p.reshape(Bp, Lp, n_heads, 2, HEAD_DIM), v_p.reshape(Bp, Lp, n_heads, V_DIM),
            pool_p.reshape(1, Bp, POOL_STATE, D),
            k_s.reshape(Bs, Ls, n_heads, 2, HEAD_DIM), v_s.reshape(Bs, Ls, n_heads, V_DIM),
            pool_s.reshape(1, Bs, POOL_STATE, D))
```

```python
import functools
import math

import numpy as np
import jax
import jax.numpy as jnp
from jax import lax
from jax.experimental import pallas as pl
from jax.experimental.pallas import tpu as pltpu

EPS = 1e-6
CHUNK = 64
POOL_WINDOWS = (2, 4, 8, 16)
POOL_STATE = max(POOL_WINDOWS) - 1
POOL_LEAD = 24
HEAD_DIM = 64
V_DIM = 2 * HEAD_DIM
N_EXPERTS = 8
TOP_K = 2
MAX_EXACT = 8
BUCKET_UPPER = (1, 2, 3, 4, 5, 6, 7, 8, 12, 16, 23, 32, 46, 64, 91)
FAR_DISTANCE = 128
NEG = -1e30
LOG2E = math.log2(math.e)
UNSHIFTED_SCORE_LIMIT = 80.0

ROW_TILE = 512
FAR_TILES_PER_TRIP = 8
LANE = 128
MXU_DIM = 256
VMEM_LIMIT = 56 * 1024 * 1024

F32 = jnp.float32
BF16 = jnp.bfloat16


def _dot(a, b):
    return jnp.dot(a, b, preferred_element_type=F32)


def _dot_nt(a, b):
    return lax.dot_general(a, b, (((1,), (1,)), ((), ())), preferred_element_type=F32)


def _rms_unit(x):
    return x * lax.rsqrt(jnp.mean(x * x, axis=-1, keepdims=True) + EPS)


def _rel_bias_tile(rel, tab):
    n = jnp.abs(rel)
    neg = tab(15)
    pos = tab(31)
    for b in range(14, -1, -1):
        lt = n < BUCKET_UPPER[b]
        neg = jnp.where(lt, tab(b), neg)
        pos = jnp.where(lt, tab(16 + b), pos)
    return jnp.where(rel > 0, pos, neg)


def _params(*sem):
    return pltpu.CompilerParams(dimension_semantics=sem, vmem_limit_bytes=VMEM_LIMIT)


def _pool_kernel(xp_ref, xs_ref, st_ref, g_ref, w_ref, sc_ref,
                 x1_ref, pp_ref, ps_ref, ext_ref, lva_ref, lvb_ref, ext3_ref, *, past_len):
    i = pl.program_id(0)
    nt = pl.num_programs(0) - 1
    T, D = xp_ref.shape
    gw = D // len(POOL_WINDOWS)

    @pl.when(i < nt)
    def _prompt():
        x = xp_ref[...]
        h = _rms_unit(x) * g_ref[...]

        @pl.when(i == 0)
        def _():
            ext_ref[0:POOL_LEAD, :] = jnp.zeros((POOL_LEAD, D), F32)
            for buf in (lva_ref, lvb_ref):
                buf[0:8, :] = jnp.zeros((8, gw), F32)

        ext_ref[POOL_LEAD:POOL_LEAD + T, :] = h
        row = i * T + lax.broadcasted_iota(jnp.int32, (T, 1), 0)
        parts = []
        for gi, w in enumerate(POOL_WINDOWS):
            c0 = gi * gw
            read = lambda lo, n: ext_ref[lo:lo + n, c0:c0 + gw]
            n_levels = w.bit_length() - 1
            for k in range(n_levels):
                d = 1 << k
                if k == n_levels - 1:
                    s = read(POOL_LEAD, T) + read(POOL_LEAD - d, T)
                else:
                    buf = (lva_ref, lvb_ref)[k % 2]
                    buf[8:POOL_LEAD + T, :] = read(8, 16 + T) + read(8 - d, 16 + T)
                    read = lambda lo, n, buf=buf: buf[lo:lo + n, :]
            cnt = jnp.minimum(w, row + 1).astype(F32)
            pooled = s / cnt - h[:, c0:c0 + gw]
            parts.append(_dot(pooled.astype(BF16), w_ref[gi]))
        mix = jnp.concatenate(parts, axis=-1) * sc_ref[...]
        x1_ref[...] = x + mix
        tail = ext_ref[T + 8:T + POOL_LEAD, :]
        ext_ref[8:POOL_LEAD, :] = tail

        @pl.when(i == nt - 1)
        def _():
            pp_ref[...] = tail[1:16, :]

    @pl.when(i == nt)
    def _sample():
        B = st_ref.shape[0]
        L = T // B
        x = xs_ref[...]
        h = _rms_unit(x) * g_ref[...]
        ext3_ref[:, 1:16, :] = st_ref[...]
        ext3_ref[:, 16:16 + L, :] = h.reshape(B, L, D)
        t = lax.broadcasted_iota(jnp.int32, (1, L, 1), 1)
        parts = []
        for gi, w in enumerate(POOL_WINDOWS):
            c0 = gi * gw
            s = ext3_ref[:, 16:16 + L, c0:c0 + gw]
            for j in range(1, w):
                s = s + ext3_ref[:, 16 - j:16 - j + L, c0:c0 + gw]
            cnt = jnp.minimum(w, past_len + t + 1).astype(F32)
            pooled = (s / cnt).reshape(T, gw) - h[:, c0:c0 + gw]
            parts.append(_dot(pooled.astype(BF16), w_ref[gi]))
        mix = jnp.concatenate(parts, axis=-1) * sc_ref[...]
        x1_ref[...] = x + mix
        ps_ref[...] = ext3_ref[:, 16 + L - POOL_STATE:16 + L, :]


def _pool_layer(xp, xs, state, g, w, sc, past_len):
    Lp, D = xp.shape
    T = ROW_TILE
    nt = Lp // T
    B = state.shape[0]
    L = xs.shape[0] // B
    return pl.pallas_call(
        functools.partial(_pool_kernel, past_len=past_len),
        grid=(nt + 1,),
        in_specs=[
            pl.BlockSpec((T, D), lambda i: (jnp.minimum(i, nt - 1), 0)),
            pl.BlockSpec((T, D), lambda i: (0, 0)),
            pl.BlockSpec((B, POOL_STATE, D), lambda i: (0, 0, 0)),
            pl.BlockSpec((1, D), lambda i: (0, 0)),
            pl.BlockSpec(w.shape, lambda i: (0, 0, 0)),
            pl.BlockSpec((1, D), lambda i: (0, 0)),
        ],
        out_specs=[
            pl.BlockSpec((T, D), lambda i: (i, 0)),
            pl.BlockSpec((POOL_STATE, D), lambda i: (0, 0)),
            pl.BlockSpec((B, POOL_STATE, D), lambda i: (0, 0, 0)),
        ],
        out_shape=[
            jax.ShapeDtypeStruct((Lp + T, D), F32),
            jax.ShapeDtypeStruct((POOL_STATE, D), F32),
            jax.ShapeDtypeStruct((B, POOL_STATE, D), F32),
        ],
        scratch_shapes=[
            pltpu.VMEM((POOL_LEAD + T, D), F32),
            pltpu.VMEM((POOL_LEAD + T, D // len(POOL_WINDOWS)), F32),
            pltpu.VMEM((POOL_LEAD + T, D // len(POOL_WINDOWS)), F32),
            pltpu.VMEM((B, 16 + L, D), F32),
        ],
        compiler_params=_params("arbitrary"),
        name="pool_mixer",
    )(xp, xs, state, g, w, sc)


def _swiglu_tile(hb, wg_ref, wu_ref, wd_ref):
    tf = wg_ref.shape[1]
    step = 3 * MXU_DIM
    out = None
    for c0 in range(0, tf, step):
        c1 = min(c0 + step, tf)
        gt = _dot(hb, wg_ref[:, c0:c1])
        ut = _dot(hb, wu_ref[:, c0:c1])
        a = gt * jax.nn.sigmoid(gt) * ut
        part = _dot(a.astype(BF16), wd_ref[c0:c1, :])
        out = part if out is None else out + part
    return out


def _ffn_kernel(x_ref, g_ref, wg_ref, wu_ref, wd_ref, out_ref, hb_ref, acc_ref):
    f = pl.program_id(1)

    @pl.when(f == 0)
    def _():
        x = x_ref[...]
        hb_ref[...] = (_rms_unit(x) * g_ref[...]).astype(BF16)
        acc_ref[...] = x

    acc_ref[...] += _swiglu_tile(hb_ref[...], wg_ref, wu_ref, wd_ref)

    @pl.when(f == pl.num_programs(1) - 1)
    def _():
        out_ref[...] = acc_ref[...]


def _ffn_splits(d_ff):
    nf = 1
    return nf, d_ff // nf


def _dense_ffn(x, g, wg, wu, wd):
    N, D = x.shape
    T = ROW_TILE
    nf, tf = _ffn_splits(wg.shape[1])
    return pl.pallas_call(
        _ffn_kernel,
        grid=(N // T, nf),
        in_specs=[
            pl.BlockSpec((T, D), lambda i, f: (i, 0)),
            pl.BlockSpec((1, D), lambda i, f: (0, 0)),
            pl.BlockSpec((D, tf), lambda i, f: (0, f)),
            pl.BlockSpec((D, tf), lambda i, f: (0, f)),
            pl.BlockSpec((tf, D), lambda i, f: (f, 0)),
        ],
        out_specs=pl.BlockSpec((T, D), lambda i, f: (i, 0)),
        out_shape=jax.ShapeDtypeStruct((N, D), F32),
        scratch_shapes=[pltpu.VMEM((T, D), BF16), pltpu.VMEM((T, D), F32)],
        compiler_params=_params("arbitrary", "arbitrary"),
        name="dense_swiglu",
    )(x, g, wg, wu, wd)


def _qkv_kernel(x_ref, gkv_ref, gq_ref, wk_ref, wv_ref, wq_ref, gkn_ref, gqn_ref,
                gsum_ref, gbc_ref,
                kp_ref, vp_ref, ks_ref, vs_ref, qh_ref, kh_ref, vh_ref, qs_ref):
    i = pl.program_id(0)
    nt = pl.num_programs(0) - 1
    n_heads = qh_ref.shape[0]

    xn = _rms_unit(x_ref[...])
    hkv = (xn * gkv_ref[...]).astype(BF16)
    hq = (xn * gq_ref[...]).astype(BF16)

    def head_norm(y, g):
        ssq = _dot((y * y).astype(BF16), gsum_ref[...])
        rs = lax.rsqrt(ssq * (1.0 / HEAD_DIM) + EPS)
        rs_hi = rs.astype(BF16)
        rs_lo = (rs - rs_hi.astype(F32)).astype(BF16)
        rsb = _dot(rs_hi, gbc_ref[...]) + _dot(rs_lo, gbc_ref[...])
        return y * rsb * g

    k = head_norm(_dot(hkv, wk_ref[...]), gkn_ref[...])
    v = _dot(hkv, wv_ref[...])
    q = head_norm(_dot(hq, wq_ref[...]), gqn_ref[...])

    @pl.when(i < nt)
    def _():
        kp_ref[...] = k
        vp_ref[...] = v
        for h in range(n_heads):
            sl = slice(h * V_DIM, (h + 1) * V_DIM)
            qh_ref[h] = q[:, sl].T.astype(BF16)
            kh_ref[h] = k[:, sl].astype(BF16)
            vh_ref[h] = v[:, sl].T.astype(BF16)

    @pl.when(i == nt)
    def _():
        ks_ref[...] = k
        vs_ref[...] = v
        qs_ref[...] = q.astype(BF16)


def _qkv_proj(x, gkv, gq, wk, wv, wq, gkn_t, gqn_t, gsum, gbc, n_sample):
    N, D = x.shape
    T = ROW_TILE
    nt = N // T - 1
    Lp = N - n_sample
    n_heads = D // V_DIM
    const2 = lambda i: (0, 0)
    prow = lambda i: (jnp.minimum(i, nt - 1), 0)
    phead = lambda i: (0, jnp.minimum(i, nt - 1), 0)
    pheadt = lambda i: (0, 0, jnp.minimum(i, nt - 1))
    return pl.pallas_call(
        _qkv_kernel,
        grid=(nt + 1,),
        in_specs=[
            pl.BlockSpec((T, D), lambda i: (i, 0)),
            pl.BlockSpec((1, D), const2),
            pl.BlockSpec((1, D), const2),
            pl.BlockSpec((D, D), const2),
            pl.BlockSpec((D, D), const2),
            pl.BlockSpec((D, D), const2),
            pl.BlockSpec((1, D), const2),
            pl.BlockSpec((1, D), const2),
            pl.BlockSpec(gsum.shape, const2),
            pl.BlockSpec(gbc.shape, const2),
        ],
        out_specs=[
            pl.BlockSpec((T, D), prow),
            pl.BlockSpec((T, D), prow),
            pl.BlockSpec((T, D), const2),
            pl.BlockSpec((T, D), const2),
            pl.BlockSpec((n_heads, V_DIM, T), pheadt),
            pl.BlockSpec((n_heads, T, V_DIM), phead),
            pl.BlockSpec((n_heads, V_DIM, T), pheadt),
            pl.BlockSpec((T, D), const2),
        ],
        out_shape=[
            jax.ShapeDtypeStruct((Lp, D), F32),
            jax.ShapeDtypeStruct((Lp, D), F32),
            jax.ShapeDtypeStruct((n_sample, D), F32),
            jax.ShapeDtypeStruct((n_sample, D), F32),
            jax.ShapeDtypeStruct((n_heads, V_DIM, Lp), BF16),
            jax.ShapeDtypeStruct((n_heads, Lp, V_DIM), BF16),
            jax.ShapeDtypeStruct((n_heads, V_DIM, Lp), BF16),
            jax.ShapeDtypeStruct((n_sample, D), BF16),
        ],
        compiler_params=_params("arbitrary"),
        name="qkv_proj",
    )(x, gkv, gq, wk, wv, wq, gkn_t, gqn_t, gsum, gbc)


def _diff_out(o0, o1, lam, g, out_scale):
    o = o0 - lam * o1
    return _rms_unit(o) * g * out_scale


def _sublane_partial_sum(p):
    return jnp.sum(p.reshape(p.shape[0] // 8, 8, p.shape[1]), axis=0)


def _attn_prompt_kernel(bias_ref, lam_ref, qt_ref, k_ref, vt_ref, g_ref, o_ref,
                        bn_ref, l_ref, acc_ref, *m_scratch, out_scale):
    h = pl.program_id(0)
    i = pl.program_id(1)
    T = qt_ref.shape[2]
    online = bool(m_scratch)

    @pl.when(i == 0)
    def _():
        key = lax.broadcasted_iota(jnp.int32, (T, T), 0)
        qry = lax.broadcasted_iota(jnp.int32, (T, T), 1)
        tab = lambda b: bias_ref[b, h] * LOG2E
        visible = (key // CHUNK) <= (qry // CHUNK)
        far = FAR_TILES_PER_TRIP - 1
        bn_ref[0:far * T, :] = jnp.full((far * T, T), tab(15), F32)
        bn_ref[far * T:(far + 1) * T, :] = _rel_bias_tile(key - qry - T, tab)
        bn_ref[(far + 1) * T:(far + 2) * T, :] = jnp.where(visible, _rel_bias_tile(key - qry, tab), NEG)

    if online:
        m_ref, = m_scratch
        m_ref[...] = jnp.full(m_ref.shape, NEG, F32)
    l_ref[...] = jnp.zeros(l_ref.shape, F32)
    acc_ref[...] = jnp.zeros(acc_ref.shape, F32)
    qt = qt_ref[0]
    dim = lax.broadcasted_iota(jnp.int32, qt.shape, 0)
    zero = jnp.zeros_like(qt)
    qc = (jnp.where(dim < HEAD_DIM, qt, zero), jnp.where(dim >= HEAD_DIM, qt, zero))

    def update(j, bias, width=T):
        start = pl.multiple_of(j * T, T)
        kt = k_ref[0, pl.ds(start, width), :]
        vt = vt_ref[0, :, pl.ds(start, width)]
        for c in range(2):
            s = _dot(kt, qc[c]) + bias
            if online:
                m_old = m_ref[c]
                m_new = jnp.maximum(m_old, jnp.max(s, axis=0, keepdims=True))
                alpha = jnp.exp2(m_old - m_new)
                p = jnp.exp2(s - m_new)
                l_ref[c] = alpha * l_ref[c] + _sublane_partial_sum(p)
                acc_ref[c] = alpha * acc_ref[c] + _dot(vt, p.astype(BF16))
                m_ref[c] = m_new
            else:
                p = jnp.exp2(s)
                l_ref[c] += _sublane_partial_sum(p)
                acc_ref[c] += _dot(vt, p.astype(BF16))

    far_bias = bias_ref[15, h] * LOG2E

    n_far = jnp.maximum(i - 1, 0)

    def far_body(jj, carry):
        update(FAR_TILES_PER_TRIP * jj, far_bias, width=FAR_TILES_PER_TRIP * T)
        return carry

    n_trips = n_far // FAR_TILES_PER_TRIP
    lax.fori_loop(0, n_trips, far_body, 0)

    rest = n_far - n_trips * FAR_TILES_PER_TRIP
    tail_tiles = FAR_TILES_PER_TRIP + 1
    for r in range(FAR_TILES_PER_TRIP):
        @pl.when((i >= 1) & (rest == r))
        def _(r=r):
            update(i - 1 - r, bn_ref[(tail_tiles - 2 - r) * T:tail_tiles * T, :], width=(r + 2) * T)

    @pl.when(i == 0)
    def _():
        update(0, bn_ref[(tail_tiles - 1) * T:tail_tiles * T, :])

    o0 = acc_ref[0] / jnp.sum(l_ref[0], axis=0, keepdims=True)
    o1 = acc_ref[1] / jnp.sum(l_ref[1], axis=0, keepdims=True)
    o = o0 - lam_ref[0] * o1
    y = o * lax.rsqrt(jnp.mean(o * o, axis=0, keepdims=True) + EPS) * g_ref[...] * out_scale
    o_ref[...] = y.T.astype(o_ref.dtype)


def _attn_prompt(rel_bias, lam, qth, kh, vth, g_subln, out_scale, online):
    n_heads, Lp, _ = kh.shape
    T = ROW_TILE
    assert T % CHUNK == 0 and T >= FAR_DISTANCE and FAR_TILES_PER_TRIP & (FAR_TILES_PER_TRIP - 1) == 0
    m_scratch = [pltpu.VMEM((2, 1, T), F32)] if online else []
    return pl.pallas_call(
        functools.partial(_attn_prompt_kernel, out_scale=out_scale),
        grid=(n_heads, Lp // T),
        in_specs=[
            pl.BlockSpec(memory_space=pltpu.SMEM),
            pl.BlockSpec(memory_space=pltpu.SMEM),
            pl.BlockSpec((1, V_DIM, T), lambda h, i: (h, 0, i)),
            pl.BlockSpec((1, Lp, V_DIM), lambda h, i: (h, 0, 0)),
            pl.BlockSpec((1, V_DIM, Lp), lambda h, i: (h, 0, 0)),
            pl.BlockSpec((V_DIM, 1), lambda h, i: (0, 0)),
        ],
        out_specs=pl.BlockSpec((T, V_DIM), lambda h, i: (i, h)),
        out_shape=jax.ShapeDtypeStruct((Lp, n_heads * V_DIM), BF16),
        scratch_shapes=[
            pltpu.VMEM(((FAR_TILES_PER_TRIP + 1) * T, T), F32),
            pltpu.VMEM((2, 8, T), F32),
            pltpu.VMEM((2, V_DIM, T), F32),
        ] + m_scratch,
        compiler_params=_params("arbitrary", "arbitrary"),
        name="attn_prompt_online" if online else "attn_prompt",
    )(rel_bias, lam, qth, kh, vth, g_subln)


def _attn_sample_kernel(lam_ref, q_ref, ckt_ref, cv_ref, kn_ref, vn_ref, tab_ref, g_ref,
                        o_ref, nearb_ref, newb_ref, qbd_ref, m_ref, l_ref, acc_ref, *, out_scale):
    b = pl.program_id(0)
    kb = pl.program_id(1)
    nkb = pl.num_programs(1)
    R, Tk = nearb_ref.shape
    L = kn_ref.shape[0]
    D = q_ref.shape[1]
    n_heads = D // V_DIM

    @pl.when((b == 0) & (kb == 0))
    def _():
        tab = lambda bkt: tab_ref[:, bkt:bkt + 1]
        t_near = lax.broadcasted_iota(jnp.int32, (R, Tk), 0) % L
        col = lax.broadcasted_iota(jnp.int32, (R, Tk), 1)
        nearb_ref[...] = _rel_bias_tile(col - Tk - t_near, tab)
        t_new = lax.broadcasted_iota(jnp.int32, (R, L), 0) % L
        col_new = lax.broadcasted_iota(jnp.int32, (R, L), 1)
        newb_ref[...] = _rel_bias_tile(col_new - t_new, tab)

    @pl.when(kb == 0)
    def _():
        m_ref[...] = jnp.full(m_ref.shape, NEG, F32)
        l_ref[...] = jnp.zeros(l_ref.shape, F32)
        acc_ref[...] = jnp.zeros(acc_ref.shape, F32)
        q_rows = jnp.concatenate([q_ref[...]] * (R // L), axis=0)
        row_group = lax.broadcasted_iota(jnp.int32, (R, D), 0) // L
        col_group = lax.broadcasted_iota(jnp.int32, (R, D), 1) // HEAD_DIM
        qbd_ref[...] = jnp.where(row_group == col_group, q_rows, jnp.zeros_like(q_rows))

    rows_per_head = 2 * L

    def update(s, value_of_head):
        m_old = m_ref[...]
        m_new = jnp.maximum(m_old, jnp.max(s, axis=-1, keepdims=True))
        alpha = jnp.exp2(m_old - m_new)
        p = jnp.exp2(s - m_new)
        l_ref[...] = alpha * l_ref[...] + jnp.sum(p, axis=-1, keepdims=True)
        pb = p.astype(BF16)
        for h in range(n_heads):
            rs = slice(h * rows_per_head, (h + 1) * rows_per_head)
            acc_ref[rs, :] = alpha[rs, :] * acc_ref[rs, :] + _dot(pb[rs, :], value_of_head(h))
        m_ref[...] = m_new

    qbd = qbd_ref[...]
    s = _dot(qbd, ckt_ref[0].astype(BF16))
    cache_value = lambda h: cv_ref[0, pl.ds(h, Tk, stride=n_heads), :].astype(BF16)

    @pl.when(kb < nkb - 1)
    def _():
        update(s + tab_ref[:, 15:16], cache_value)

    @pl.when(kb == nkb - 1)
    def _():
        update(s + nearb_ref[...], cache_value)
        s_new = _dot_nt(qbd, kn_ref[...].astype(BF16)) + newb_ref[...]
        update(s_new, lambda h: vn_ref[:, h * V_DIM:(h + 1) * V_DIM].astype(BF16))
        lam = lam_ref[0]
        o = acc_ref[...] / l_ref[...]
        for h in range(n_heads):
            r0 = h * rows_per_head
            o_ref[:, h * V_DIM:(h + 1) * V_DIM] = _diff_out(
                o[r0:r0 + L, :], o[r0 + L:r0 + 2 * L, :], lam, g_ref[...], out_scale).astype(o_ref.dtype)


def _attn_sample(lam, q, cache_k, cache_v, k_new, v_new, tab, g_subln, out_scale):
    B, past, n_heads = cache_v.shape[:3]
    D = n_heads * V_DIM
    L = k_new.shape[0] // B
    R = (D // HEAD_DIM) * L
    Tk = min(2048, past)
    assert past % Tk == 0 and Tk >= FAR_DISTANCE + L
    cache_kt = jnp.transpose(cache_k, (0, 2, 3, 4, 1)).reshape(B, D, past)
    return pl.pallas_call(
        functools.partial(_attn_sample_kernel, out_scale=out_scale),
        grid=(B, past // Tk),
        in_specs=[
            pl.BlockSpec(memory_space=pltpu.SMEM),
            pl.BlockSpec((L, D), lambda b, k: (b, 0)),
            pl.BlockSpec((1, D, Tk), lambda b, k: (b, 0, k)),
            pl.BlockSpec((1, Tk * n_heads, V_DIM), lambda b, k: (b, k, 0)),
            pl.BlockSpec((L, D), lambda b, k: (b, 0)),
            pl.BlockSpec((L, D), lambda b, k: (b, 0)),
            pl.BlockSpec(tab.shape, lambda b, k: (0, 0)),
            pl.BlockSpec((1, V_DIM), lambda b, k: (0, 0)),
        ],
        out_specs=pl.BlockSpec((L, D), lambda b, k: (b, 0)),
        out_shape=jax.ShapeDtypeStruct((B * L, D), BF16),
        scratch_shapes=[
            pltpu.VMEM((R, Tk), F32),
            pltpu.VMEM((R, L), F32),
            pltpu.VMEM((R, D), BF16),
            pltpu.VMEM((R, 1), F32),
            pltpu.VMEM((R, 1), F32),
            pltpu.VMEM((R, V_DIM), F32),
        ],
        compiler_params=_params("arbitrary", "arbitrary"),
        name="attn_sample",
    )(lam, q, cache_kt, cache_v.reshape(B, past * n_heads, V_DIM), k_new, v_new, tab, g_subln)


def _oproj_router_kernel(op_ref, os_ref, x_ref, wo_ref, g_ref, wrh_ref, wrl_ref,
                         x3_ref, h_ref, route_ref, route_t_ref):
    i = pl.program_id(0)
    nt = pl.num_programs(0) - 1
    o = jnp.where(i == nt, os_ref[...], op_ref[...])
    x3 = x_ref[...] + _dot(o, wo_ref[...])
    x3_ref[...] = x3
    h = _rms_unit(x3) * g_ref[...]
    h_hi = h.astype(BF16)
    h_ref[...] = h_hi
    h_lo = (h - h_hi.astype(F32)).astype(BF16)
    logits = _dot(h_hi, wrh_ref[...]) + (_dot(h_lo, wrh_ref[...]) + _dot(h_hi, wrl_ref[...]))
    lane = lax.broadcasted_iota(jnp.int32, logits.shape, 1)
    lg = jnp.where(lane < N_EXPERTS, logits, -jnp.inf)
    m1 = jnp.max(lg, axis=-1, keepdims=True)
    i1 = jnp.min(jnp.where(lg == m1, lane, LANE), axis=-1, keepdims=True)
    lg2 = jnp.where(lane == i1, -jnp.inf, lg)
    m2 = jnp.max(lg2, axis=-1, keepdims=True)
    i2 = jnp.min(jnp.where(lg2 == m2, lane, LANE), axis=-1, keepdims=True)
    e2 = jnp.exp(m2 - m1)
    den = 1.0 + e2
    g1 = 1.0 / den
    g2 = e2 / den
    route = jnp.where(lane == 0, i1.astype(F32),
                      jnp.where(lane == 1, i2.astype(F32),
                                jnp.where(lane == 2, g1,
                                          jnp.where(lane == 3, g2, 0.0))))
    route_ref[...] = route
    route_t_ref[...] = route.T


def _oproj_router(o_p, o_s, x, wo, g, wr_hi, wr_lo):
    N, D = x.shape
    T = ROW_TILE
    nt = N // T - 1
    const2 = lambda i: (0, 0)
    row = lambda i: (i, 0)
    return pl.pallas_call(
        _oproj_router_kernel,
        grid=(nt + 1,),
        in_specs=[
            pl.BlockSpec((T, D), lambda i: (jnp.minimum(i, nt - 1), 0)),
            pl.BlockSpec((T, D), const2),
            pl.BlockSpec((T, D), row),
            pl.BlockSpec((D, D), const2),
            pl.BlockSpec((1, D), const2),
            pl.BlockSpec((D, LANE), const2),
            pl.BlockSpec((D, LANE), const2),
        ],
        out_specs=[
            pl.BlockSpec((T, D), row),
            pl.BlockSpec((T, D), row),
            pl.BlockSpec((T, LANE), row),
            pl.BlockSpec((LANE, T), lambda i: (0, i)),
        ],
        out_shape=[
            jax.ShapeDtypeStruct((N, D), F32),
            jax.ShapeDtypeStruct((N, D), BF16),
            jax.ShapeDtypeStruct((N, LANE), F32),
            jax.ShapeDtypeStruct((LANE, N), F32),
        ],
        compiler_params=_params("arbitrary"),
        name="oproj_router",
    )(o_p, o_s, x, wo, g, wr_hi, wr_lo)


MOE_CHUNK = 128
MOE_ROW_ALIGN = 16


def _for_each_chunk(cnt_ref, step, n_chunks, fn, first=0):
    for e in range(N_EXPERTS):
        for k in range(first, n_chunks):
            if k == 0:
                fn(e, k)
            else:
                @pl.when(cnt_ref[step * N_EXPERTS + e] > k * MOE_CHUNK)
                def _(e=e, k=k):
                    fn(e, k)


def _dispatch_kernel(off_ref, cnt_ref, hb_ref, rt_ref, utri_ref, xs_in_ref, xs_ref, stage_ref, sem):
    del xs_in_ref
    t = pl.program_id(0)
    T = hb_ref.shape[0]
    S = MOE_CHUNK
    nk = T // S

    def chunk_copy(step, e, k):
        start = pl.multiple_of(off_ref[step * N_EXPERTS + e] + k * S, MOE_ROW_ALIGN)
        slot = k * N_EXPERTS + e
        return pltpu.make_async_copy(stage_ref.at[slot], xs_ref.at[pl.ds(start, S), :], sem.at[slot])

    rt = rt_ref[...]
    expert_id = lax.broadcasted_iota(jnp.int32, (N_EXPERTS, T), 0).astype(F32)
    member = (rt[0:1, :] == expert_id) | (rt[1:2, :] == expert_id)
    rank = _dot(member.astype(BF16), utri_ref[...])
    rank = jnp.where(member, rank, 0.0)
    hb = hb_ref[...]

    row3 = lax.broadcasted_iota(jnp.int32, (1, S, 1), 1)
    sel_first = (rank[:, None, :] == (row3 + 1).astype(F32)).astype(BF16).reshape(N_EXPERTS * S, T)
    first = _dot(sel_first, hb).astype(BF16).reshape(N_EXPERTS, S, hb.shape[1])

    @pl.when(t > 0)
    def _():
        _for_each_chunk(cnt_ref, t - 1, nk, lambda e, k: chunk_copy(t - 1, e, k).wait())

    stage_ref[0:N_EXPERTS] = first
    for e in range(N_EXPERTS):
        chunk_copy(t, e, 0).start()

    row = lax.broadcasted_iota(jnp.int32, (S, 1), 0)

    def emit(e, k):
        sel = rank[e:e + 1, :] == (row + (k * S + 1)).astype(F32)
        stage_ref[k * N_EXPERTS + e] = _dot(sel.astype(BF16), hb).astype(BF16)
        chunk_copy(t, e, k).start()

    _for_each_chunk(cnt_ref, t, nk, emit, first=1)

    @pl.when(t == pl.num_programs(0) - 1)
    def _():
        _for_each_chunk(cnt_ref, t, nk, lambda e, k: chunk_copy(t, e, k).wait())


def _dispatch(off, cnt, hb, route_t, utri, n_rows):
    N, D = hb.shape
    T = ROW_TILE
    nk = T // MOE_CHUNK
    n_slots = N_EXPERTS * nk
    xs_init = jnp.zeros((n_rows, D), BF16)
    return pl.pallas_call(
        _dispatch_kernel,
        grid_spec=pltpu.PrefetchScalarGridSpec(
            num_scalar_prefetch=2,
            grid=(N // T,),
            in_specs=[
                pl.BlockSpec((T, D), lambda t, off, cnt: (t, 0)),
                pl.BlockSpec((LANE, T), lambda t, off, cnt: (0, t)),
                pl.BlockSpec((T, T), lambda t, off, cnt: (0, 0)),
                pl.BlockSpec(memory_space=pl.ANY),
            ],
            out_specs=pl.BlockSpec(memory_space=pl.ANY),
            scratch_shapes=[
                pltpu.VMEM((n_slots, MOE_CHUNK, D), BF16),
                pltpu.SemaphoreType.DMA((n_slots,)),
            ],
        ),
        out_shape=jax.ShapeDtypeStruct((n_rows, D), BF16),
        input_output_aliases={5: 0},
        compiler_params=_params("arbitrary"),
        name="moe_dispatch",
    )(off, cnt, hb, route_t, utri, xs_init)


def _experts_kernel(te_ref, na_ref, xs_ref, wg_ref, wu_ref, wd_ref, out_ref, acc_ref):
    i = pl.program_id(0)
    f = pl.program_id(1)
    last = pl.num_programs(1) - 1
    active = i < na_ref[0]

    @pl.when(active)
    def _():
        @pl.when(f == 0)
        def _():
            acc_ref[...] = jnp.zeros(acc_ref.shape, F32)

        acc_ref[...] += _swiglu_tile(xs_ref[...], wg_ref.at[0], wu_ref.at[0], wd_ref.at[0])

        @pl.when(f == last)
        def _():
            out_ref[...] = acc_ref[...].astype(out_ref.dtype)

    @pl.when(jnp.logical_not(active) & (f == last))
    def _():
        out_ref[...] = jnp.zeros(out_ref.shape, out_ref.dtype)


def _experts(tile_expert, n_active, xs, wg, wu, wd):
    P, D = xs.shape
    T = ROW_TILE
    nf, tf = _ffn_splits(wg.shape[2])
    fidx = lambda i, f, na: jnp.where(i < na[0], f, nf - 1)
    return pl.pallas_call(
        _experts_kernel,
        grid_spec=pltpu.PrefetchScalarGridSpec(
            num_scalar_prefetch=2,
            grid=(P // T, nf),
            in_specs=[
                pl.BlockSpec((T, D), lambda i, f, te, na: (i, 0)),
                pl.BlockSpec((1, D, tf), lambda i, f, te, na: (te[i], 0, fidx(i, f, na))),
                pl.BlockSpec((1, D, tf), lambda i, f, te, na: (te[i], 0, fidx(i, f, na))),
                pl.BlockSpec((1, tf, D), lambda i, f, te, na: (te[i], fidx(i, f, na), 0)),
            ],
            out_specs=pl.BlockSpec((T, D), lambda i, f, te, na: (i, 0)),
            scratch_shapes=[pltpu.VMEM((T, D), F32)],
        ),
        out_shape=jax.ShapeDtypeStruct((P, D), BF16),
        compiler_params=_params("arbitrary", "arbitrary"),
        name="moe_experts",
    )(tile_expert, n_active, xs, wg, wu, wd)


def _combine_kernel(off_ref, cnt_ref, x_ref, route_ref, ltri_ref, ys_hbm, yp_ref, ysm_ref,
                    buf_ref, acc_ref, sem):
    t = pl.program_id(0)
    nt = pl.num_programs(0) - 1
    T = x_ref.shape[0]
    S = MOE_CHUNK
    nk = T // S

    def chunk_copy(step, e, k):
        start = pl.multiple_of(off_ref[step * N_EXPERTS + e] + k * S, MOE_ROW_ALIGN)
        bank, slot = step % 2, k * N_EXPERTS + e
        return pltpu.make_async_copy(ys_hbm.at[pl.ds(start, S), :], buf_ref.at[bank, slot], sem.at[bank, slot])

    @pl.when(t == 0)
    def _():
        _for_each_chunk(cnt_ref, t, nk, lambda e, k: chunk_copy(t, e, k).start())

    @pl.when(t < nt)
    def _():
        _for_each_chunk(cnt_ref, t + 1, nk, lambda e, k: chunk_copy(t + 1, e, k).start())

    route = route_ref[...]
    e1, e2, g1, g2 = (route[:, j:j + 1] for j in range(2 * TOP_K))
    lane = lax.broadcasted_iota(jnp.int32, route.shape, 1).astype(F32)
    member = (lane == e1) | (lane == e2)
    rank = _dot(ltri_ref[...], member.astype(BF16))
    rank = jnp.where(member, rank, 0.0)
    col = lax.broadcasted_iota(jnp.int32, (1, S), 1)
    gate_of = lambda e: jnp.where(e1 == e, g1, 0.0) + jnp.where(e2 == e, g2, 0.0)
    select = lambda e, k: rank[:, e:e + 1] == (col + (k * S + 1)).astype(F32)

    for e in range(N_EXPERTS):
        chunk_copy(t, e, 0).wait()
    weight = jnp.concatenate([jnp.where(select(e, 0), gate_of(e), 0.0) for e in range(N_EXPERTS)], axis=-1)
    w_hi = weight.astype(BF16)
    w_lo = (weight - w_hi.astype(F32)).astype(BF16)
    rows = buf_ref[t % 2, 0:N_EXPERTS].reshape(N_EXPERTS * S, x_ref.shape[1])
    acc_ref[...] = x_ref[...] + (_dot(w_hi, rows) + _dot(w_lo, rows))

    def absorb(e, k):
        chunk_copy(t, e, k).wait()
        acc_ref[...] += gate_of(e) * _dot(select(e, k).astype(BF16), buf_ref[t % 2, k * N_EXPERTS + e])

    _for_each_chunk(cnt_ref, t, nk, absorb, first=1)

    @pl.when(t < nt)
    def _():
        yp_ref[...] = acc_ref[...]

    @pl.when(t == nt)
    def _():
        ysm_ref[...] = acc_ref[...]


def _combine(off, cnt, x3, route, ltri, ys, n_sample):
    N, D = x3.shape
    T = ROW_TILE
    nt = N // T - 1
    n_slots = N_EXPERTS * (T // MOE_CHUNK)
    return pl.pallas_call(
        _combine_kernel,
        grid_spec=pltpu.PrefetchScalarGridSpec(
            num_scalar_prefetch=2,
            grid=(nt + 1,),
            in_specs=[
                pl.BlockSpec((T, D), lambda t, off, cnt: (t, 0)),
                pl.BlockSpec((T, LANE), lambda t, off, cnt: (t, 0)),
                pl.BlockSpec((T, T), lambda t, off, cnt: (0, 0)),
                pl.BlockSpec(memory_space=pl.ANY),
            ],
            out_specs=[
                pl.BlockSpec((T, D), lambda t, off, cnt: (jnp.minimum(t, nt - 1), 0)),
                pl.BlockSpec((T, D), lambda t, off, cnt: (0, 0)),
            ],
            scratch_shapes=[
                pltpu.VMEM((2, n_slots, MOE_CHUNK, D), BF16),
                pltpu.VMEM((T, D), F32),
                pltpu.SemaphoreType.DMA((2, n_slots)),
            ],
        ),
        out_shape=[
            jax.ShapeDtypeStruct((N - n_sample, D), F32),
            jax.ShapeDtypeStruct((n_sample, D), F32),
        ],
        compiler_params=_params("arbitrary"),
        name="moe_combine",
    )(off, cnt, x3, route, ltri, ys)


def _routing_tables(route, tile):
    n = route.shape[0]
    n_tok_tiles = n // tile
    experts = route[:, :TOP_K].astype(jnp.int32)
    onehot = (experts[:, :, None] == jnp.arange(N_EXPERTS)[None, None, :]).astype(jnp.int32).sum(axis=1)
    cnt = onehot.reshape(n_tok_tiles, tile, N_EXPERTS).sum(axis=1)
    span = (cnt + MOE_ROW_ALIGN - 1) // MOE_ROW_ALIGN * MOE_ROW_ALIGN
    totals = span.sum(axis=0)
    tiles_per = (totals + MOE_CHUNK + tile - 1) // tile
    tile_end = jnp.cumsum(tiles_per)
    group_off = (tile_end - tiles_per) * tile
    off = group_off[None, :] + jnp.cumsum(span, axis=0) - span
    max_rows = n * TOP_K + N_EXPERTS * (MOE_CHUNK + n_tok_tiles * (MOE_ROW_ALIGN - 1))
    n_row_tiles = max_rows // tile + N_EXPERTS
    n_active = tile_end[-1:].astype(jnp.int32)
    tile_ids = jnp.minimum(jnp.arange(n_row_tiles, dtype=jnp.int32), n_active[0] - 1)
    tile_expert = jnp.sum((tile_ids[:, None] >= tile_end[None, :]).astype(jnp.int32), axis=1)
    return (off.reshape(-1).astype(jnp.int32), cnt.reshape(-1).astype(jnp.int32),
            tile_expert, n_active, n_row_tiles * tile)


def _group_matrices(d_model):
    n_groups = d_model // HEAD_DIM
    gsum = np.zeros((d_model, LANE), np.float32)
    gsum[np.arange(d_model), np.arange(d_model) // HEAD_DIM] = 1.0
    assert n_groups <= LANE
    return jnp.asarray(gsum, BF16), jnp.asarray(gsum.T.copy(), BF16)


def kernel(x_prompt, x_sample, cache_k, cache_v, state_pool, g_pool_norm, w_pool, pool_scale, g_attn, w_q, g_qn, lambda_q1, lambda_k1, lambda_q2, lambda_k2, g_subln, w_o, g_kv, w_k, w_v, g_kn, rel_bias, g_ffn, w_gate_dense, w_up_dense, w_down_dense, w_router, w_gate_moe, w_up_moe, w_down_moe):
    Bp, Lp, D = x_prompt.shape
    Bs, Ls, _ = x_sample.shape
    past = cache_k.shape[1]
    n_heads = D // V_DIM
    n_sample = Bs * Ls
    assert Bp == 1 and n_sample == ROW_TILE and Lp % ROW_TILE == 0
    assert g_pool_norm.shape[0] == 1 and g_attn.shape[0] == 1
    bf = lambda a: a.astype(BF16)
    row = lambda a: a.reshape(1, -1)

    x1, pool_p, pool_s = _pool_layer(
        x_prompt.reshape(Lp, D), x_sample.reshape(n_sample, D), state_pool[0],
        row(g_pool_norm[0]), bf(w_pool[0]), row(pool_scale[0]), past)
    x2 = _dense_ffn(x1, row(g_ffn[0]), bf(w_gate_dense[0]), bf(w_up_dense[0]), bf(w_down_dense[0]))

    layer = 1
    lambda_init = 0.8 - 0.6 * math.exp(-0.3 * layer)
    lam = (jnp.exp(jnp.sum(lambda_q1[0] * lambda_k1[0])) - jnp.exp(jnp.sum(lambda_q2[0] * lambda_k2[0]))
           + lambda_init).reshape(1)
    gsum, gbc = _group_matrices(D)
    n_groups = D // HEAD_DIM
    k_p, v_p, k_s, v_s, qh, kh, vh, q_s = _qkv_proj(
        x2, row(g_kv), row(g_attn[0]), bf(w_k), bf(w_v), bf(w_q[0]),
        row(jnp.tile(g_kn, n_groups)), row(jnp.tile(g_qn[0], n_groups) * (HEAD_DIM ** -0.5 * LOG2E)),
        gsum, gbc, n_sample)
    out_scale = 1.0 - lambda_init
    g_sub = row(g_subln[0])
    score_bound = LOG2E * (math.sqrt(HEAD_DIM) * jnp.max(jnp.abs(g_qn[0])) * jnp.max(jnp.abs(g_kn))
                           + jnp.max(jnp.abs(rel_bias)))
    o_p = lax.cond(
        score_bound <= UNSHIFTED_SCORE_LIMIT,
        functools.partial(_attn_prompt, out_scale=out_scale, online=False),
        functools.partial(_attn_prompt, out_scale=out_scale, online=True),
        rel_bias, lam, qh, kh, vh, g_subln[0].reshape(V_DIM, 1))

    tab = jnp.repeat(rel_bias.T, 2 * Ls, axis=0) * LOG2E
    o_s = _attn_sample(lam, q_s, cache_k, cache_v, k_s, v_s, tab, g_sub, out_scale)

    wr = jnp.pad(w_router[0], ((0, 0), (0, LANE - N_EXPERTS)))
    wr_hi = bf(wr)
    wr_lo = bf(wr - wr_hi.astype(F32))
    x3, h_moe, route, route_t = _oproj_router(o_p, o_s, x2, bf(w_o[0]), row(g_ffn[1]), wr_hi, wr_lo)
    off, cnt, tile_expert, n_active, n_rows = _routing_tables(route, ROW_TILE)
    ltri = jnp.asarray(np.tril(np.ones((ROW_TILE, ROW_TILE), np.float32)), BF16)
    xs = _dispatch(off, cnt, h_moe, route_t, ltri.T, n_rows)
    ys = _experts(tile_expert, n_active, xs, bf(w_gate_moe[0]), bf(w_up_moe[0]), bf(w_down_moe[0]))
    y_p, y_s = _combine(off, cnt, x3, route, ltri, ys, n_sample)

    return (y_p.reshape(Bp, Lp, D), y_s.reshape(Bs, Ls, D),
            k_p.reshape(Bp, Lp, n_heads, 2, HEAD_DIM), v_p.reshape(Bp, Lp, n_heads, V_DIM),
            pool_p.reshape(1, Bp, POOL_STATE, D),
            k_s.reshape(Bs, Ls, n_heads, 2, HEAD_DIM), v_s.reshape(Bs, Ls, n_heads, V_DIM),
            pool_s.reshape(1, Bs, POOL_STATE, D))
```

```python
import functools
import math

import numpy as np
import jax
import jax.numpy as jnp
from jax import lax
from jax.experimental import pallas as pl
from jax.experimental.pallas import tpu as pltpu

EPS = 1e-6
CHUNK = 64
POOL_WINDOWS = (2, 4, 8, 16)
POOL_STATE = max(POOL_WINDOWS) - 1
POOL_LEAD = 24
HEAD_DIM = 64
V_DIM = 2 * HEAD_DIM
N_EXPERTS = 8
TOP_K = 2
BUCKET_UPPER = (1, 2, 3, 4, 5, 6, 7, 8, 12, 16, 23, 32, 46, 64, 91)
FAR_DISTANCE = 128
NEG = -1e30
LOG2E = math.log2(math.e)
UNSHIFTED_SCORE_LIMIT = 80.0

ROW_TILE = 512
FAR_TILES_PER_TRIP = 8
LANE = 128
MXU_DIM = 256
VMEM_LIMIT = 56 * 1024 * 1024

F32 = jnp.float32
BF16 = jnp.bfloat16


def _dot(a, b):
    return jnp.dot(a, b, preferred_element_type=F32)


def _dot_nt(a, b):
    return lax.dot_general(a, b, (((1,), (1,)), ((), ())), preferred_element_type=F32)


def _rms_unit(x):
    return x * lax.rsqrt(jnp.mean(x * x, axis=-1, keepdims=True) + EPS)


def _rel_bias_tile(rel, tab):
    n = jnp.abs(rel)
    neg = tab(15)
    pos = tab(31)
    for b in range(14, -1, -1):
        lt = n < BUCKET_UPPER[b]
        neg = jnp.where(lt, tab(b), neg)
        pos = jnp.where(lt, tab(16 + b), pos)
    return jnp.where(rel > 0, pos, neg)


def _params(*sem):
    return pltpu.CompilerParams(dimension_semantics=sem, vmem_limit_bytes=VMEM_LIMIT)


def _pool_kernel(xp_ref, xs_ref, st_ref, g_ref, w_ref, sc_ref,
                 x1_ref, pp_ref, ps_ref, ext_ref, lva_ref, lvb_ref, ext3_ref, *, past_len):
    i = pl.program_id(0)
    nt = pl.num_programs(0) - 1
    T, D = xp_ref.shape
    gw = D // len(POOL_WINDOWS)

    @pl.when(i < nt)
    def _prompt():
        x = xp_ref[...]
        h = _rms_unit(x) * g_ref[...]

        @pl.when(i == 0)
        def _():
            ext_ref[0:POOL_LEAD, :] = jnp.zeros((POOL_LEAD, D), F32)
            for buf in (lva_ref, lvb_ref):
                buf[0:8, :] = jnp.zeros((8, gw), F32)

        ext_ref[POOL_LEAD:POOL_LEAD + T, :] = h
        row = i * T + lax.broadcasted_iota(jnp.int32, (T, 1), 0)
        parts = []
        for gi, w in enumerate(POOL_WINDOWS):
            c0 = gi * gw
            read = lambda lo, n: ext_ref[lo:lo + n, c0:c0 + gw]
            n_levels = w.bit_length() - 1
            for k in range(n_levels):
                d = 1 << k
                if k == n_levels - 1:
                    s = read(POOL_LEAD, T) + read(POOL_LEAD - d, T)
                else:
                    buf = (lva_ref, lvb_ref)[k % 2]
                    buf[8:POOL_LEAD + T, :] = read(8, 16 + T) + read(8 - d, 16 + T)
                    read = lambda lo, n, buf=buf: buf[lo:lo + n, :]
            cnt = jnp.minimum(w, row + 1).astype(F32)
            pooled = s / cnt - h[:, c0:c0 + gw]
            parts.append(_dot(pooled.astype(BF16), w_ref[gi]))
        mix = jnp.concatenate(parts, axis=-1) * sc_ref[...]
        x1_ref[...] = x + mix
        tail = ext_ref[T + 8:T + POOL_LEAD, :]
        ext_ref[8:POOL_LEAD, :] = tail

        @pl.when(i == nt - 1)
        def _():
            pp_ref[...] = tail[1:16, :]

    @pl.when(i == nt)
    def _sample():
        B = st_ref.shape[0]
        L = T // B
        x = xs_ref[...]
        h = _rms_unit(x) * g_ref[...]
        ext3_ref[:, 1:16, :] = st_ref[...]
        ext3_ref[:, 16:16 + L, :] = h.reshape(B, L, D)
        t = lax.broadcasted_iota(jnp.int32, (1, L, 1), 1)
        parts = []
        for gi, w in enumerate(POOL_WINDOWS):
            c0 = gi * gw
            s = ext3_ref[:, 16:16 + L, c0:c0 + gw]
            for j in range(1, w):
                s = s + ext3_ref[:, 16 - j:16 - j + L, c0:c0 + gw]
            cnt = jnp.minimum(w, past_len + t + 1).astype(F32)
            pooled = (s / cnt).reshape(T, gw) - h[:, c0:c0 + gw]
            parts.append(_dot(pooled.astype(BF16), w_ref[gi]))
        mix = jnp.concatenate(parts, axis=-1) * sc_ref[...]
        x1_ref[...] = x + mix
        ps_ref[...] = ext3_ref[:, 16 + L - POOL_STATE:16 + L, :]


def _pool_layer(xp, xs, state, g, w, sc, past_len):
    Lp, D = xp.shape
    T = ROW_TILE
    nt = Lp // T
    B = state.shape[0]
    L = xs.shape[0] // B
    return pl.pallas_call(
        functools.partial(_pool_kernel, past_len=past_len),
        grid=(nt + 1,),
        in_specs=[
            pl.BlockSpec((T, D), lambda i: (jnp.minimum(i, nt - 1), 0)),
            pl.BlockSpec((T, D), lambda i: (0, 0)),
            pl.BlockSpec((B, POOL_STATE, D), lambda i: (0, 0, 0)),
            pl.BlockSpec((1, D), lambda i: (0, 0)),
            pl.BlockSpec(w.shape, lambda i: (0, 0, 0)),
            pl.BlockSpec((1, D), lambda i: (0, 0)),
        ],
        out_specs=[
            pl.BlockSpec((T, D), lambda i: (i, 0)),
            pl.BlockSpec((POOL_STATE, D), lambda i: (0, 0)),
            pl.BlockSpec((B, POOL_STATE, D), lambda i: (0, 0, 0)),
        ],
        out_shape=[
            jax.ShapeDtypeStruct((Lp + T, D), F32),
            jax.ShapeDtypeStruct((POOL_STATE, D), F32),
            jax.ShapeDtypeStruct((B, POOL_STATE, D), F32),
        ],
        scratch_shapes=[
            pltpu.VMEM((POOL_LEAD + T, D), F32),
            pltpu.VMEM((POOL_LEAD + T, D // len(POOL_WINDOWS)), F32),
            pltpu.VMEM((POOL_LEAD + T, D // len(POOL_WINDOWS)), F32),
            pltpu.VMEM((B, 16 + L, D), F32),
        ],
        compiler_params=_params("arbitrary"),
        name="pool_mixer",
    )(xp, xs, state, g, w, sc)


def _swiglu_tile(hb, wg_ref, wu_ref, wd_ref):
    tf = wg_ref.shape[1]
    step = 3 * MXU_DIM
    out = None
    for c0 in range(0, tf, step):
        c1 = min(c0 + step, tf)
        gt = _dot(hb, wg_ref[:, c0:c1])
        ut = _dot(hb, wu_ref[:, c0:c1])
        a = gt * jax.nn.sigmoid(gt) * ut
        part = _dot(a.astype(BF16), wd_ref[c0:c1, :])
        out = part if out is None else out + part
    return out


def _ffn_kernel(x_ref, g_ref, wg_ref, wu_ref, wd_ref, out_ref, hb_ref, acc_ref):
    f = pl.program_id(1)

    @pl.when(f == 0)
    def _():
        x = x_ref[...]
        hb_ref[...] = (_rms_unit(x) * g_ref[...]).astype(BF16)
        acc_ref[...] = x

    acc_ref[...] += _swiglu_tile(hb_ref[...], wg_ref, wu_ref, wd_ref)

    @pl.when(f == pl.num_programs(1) - 1)
    def _():
        out_ref[...] = acc_ref[...]


def _ffn_splits(d_ff):
    nf = 1
    return nf, d_ff // nf


def _dense_ffn(x, g, wg, wu, wd):
    N, D = x.shape
    T = ROW_TILE
    nf, tf = _ffn_splits(wg.shape[1])
    return pl.pallas_call(
        _ffn_kernel,
        grid=(N // T, nf),
        in_specs=[
            pl.BlockSpec((T, D), lambda i, f: (i, 0)),
            pl.BlockSpec((1, D), lambda i, f: (0, 0)),
            pl.BlockSpec((D, tf), lambda i, f: (0, f)),
            pl.BlockSpec((D, tf), lambda i, f: (0, f)),
            pl.BlockSpec((tf, D), lambda i, f: (f, 0)),
        ],
        out_specs=pl.BlockSpec((T, D), lambda i, f: (i, 0)),
        out_shape=jax.ShapeDtypeStruct((N, D), F32),
        scratch_shapes=[pltpu.VMEM((T, D), BF16), pltpu.VMEM((T, D), F32)],
        compiler_params=_params("arbitrary", "arbitrary"),
        name="dense_swiglu",
    )(x, g, wg, wu, wd)


def _qkv_kernel(x_ref, gkv_ref, gq_ref, wk_ref, wv_ref, wq_ref, gkn_ref, gqn_ref,
                gsum_ref, gbc_ref,
                kp_ref, vp_ref, ks_ref, vs_ref, qh_ref, kh_ref, vh_ref, qs_ref):
    i = pl.program_id(0)
    nt = pl.num_programs(0) - 1
    n_heads = qh_ref.shape[0]

    xn = _rms_unit(x_ref[...])
    hkv = (xn * gkv_ref[...]).astype(BF16)
    hq = (xn * gq_ref[...]).astype(BF16)

    def head_norm(y, g):
        ssq = _dot((y * y).astype(BF16), gsum_ref[...])
        rs = lax.rsqrt(ssq * (1.0 / HEAD_DIM) + EPS)
        rs_hi = rs.astype(BF16)
        rs_lo = (rs - rs_hi.astype(F32)).astype(BF16)
        rsb = _dot(jnp.concatenate([rs_hi, rs_lo], axis=-1), gbc_ref[...])
        return y * rsb * g

    k = head_norm(_dot(hkv, wk_ref[...]), gkn_ref[...])
    v = _dot(hkv, wv_ref[...])
    q = head_norm(_dot(hq, wq_ref[...]), gqn_ref[...])

    @pl.when(i < nt)
    def _():
        kp_ref[...] = k
        vp_ref[...] = v
        for h in range(n_heads):
            sl = slice(h * V_DIM, (h + 1) * V_DIM)
            qh_ref[h] = q[:, sl].T.astype(BF16)
            kh_ref[h] = k[:, sl].astype(BF16)
            vh_ref[h] = v[:, sl].T.astype(BF16)

    @pl.when(i == nt)
    def _():
        ks_ref[...] = k
        vs_ref[...] = v
        qs_ref[...] = q.astype(BF16)


def _qkv_proj(x, gkv, gq, wk, wv, wq, gkn_t, gqn_t, gsum, gbc, n_sample):
    N, D = x.shape
    T = ROW_TILE
    nt = N // T - 1
    Lp = N - n_sample
    n_heads = D // V_DIM
    const2 = lambda i: (0, 0)
    prow = lambda i: (jnp.minimum(i, nt - 1), 0)
    phead = lambda i: (0, jnp.minimum(i, nt - 1), 0)
    pheadt = lambda i: (0, 0, jnp.minimum(i, nt - 1))
    return pl.pallas_call(
        _qkv_kernel,
        grid=(nt + 1,),
        in_specs=[
            pl.BlockSpec((T, D), lambda i: (i, 0)),
            pl.BlockSpec((1, D), const2),
            pl.BlockSpec((1, D), const2),
            pl.BlockSpec((D, D), const2),
            pl.BlockSpec((D, D), const2),
            pl.BlockSpec((D, D), const2),
            pl.BlockSpec((1, D), const2),
            pl.BlockSpec((1, D), const2),
            pl.BlockSpec(gsum.shape, const2),
            pl.BlockSpec(gbc.shape, const2),
        ],
        out_specs=[
            pl.BlockSpec((T, D), prow),
            pl.BlockSpec((T, D), prow),
            pl.BlockSpec((T, D), const2),
            pl.BlockSpec((T, D), const2),
            pl.BlockSpec((n_heads, V_DIM, T), pheadt),
            pl.BlockSpec((n_heads, T, V_DIM), phead),
            pl.BlockSpec((n_heads, V_DIM, T), pheadt),
            pl.BlockSpec((T, D), const2),
        ],
        out_shape=[
            jax.ShapeDtypeStruct((Lp, D), F32),
            jax.ShapeDtypeStruct((Lp, D), F32),
            jax.ShapeDtypeStruct((n_sample, D), F32),
            jax.ShapeDtypeStruct((n_sample, D), F32),
            jax.ShapeDtypeStruct((n_heads, V_DIM, Lp), BF16),
            jax.ShapeDtypeStruct((n_heads, Lp, V_DIM), BF16),
            jax.ShapeDtypeStruct((n_heads, V_DIM, Lp), BF16),
            jax.ShapeDtypeStruct((n_sample, D), BF16),
        ],
        compiler_params=_params("arbitrary"),
        name="qkv_proj",
    )(x, gkv, gq, wk, wv, wq, gkn_t, gqn_t, gsum, gbc)


def _diff_out(o0, o1, lam, g, out_scale):
    o = o0 - lam * o1
    return _rms_unit(o) * g * out_scale


def _sublane_partial_sum(p):
    return jnp.sum(p.reshape(p.shape[0] // 8, 8, p.shape[1]), axis=0)


def _attn_prompt_kernel(bias_ref, lam_ref, qt_ref, k_ref, vt_ref, g_ref, o_ref,
                        bn_ref, l_ref, acc_ref, *m_scratch, out_scale):
    h = pl.program_id(0)
    i = pl.program_id(1)
    T = qt_ref.shape[2]
    online = bool(m_scratch)

    @pl.when(i == 0)
    def _():
        key = lax.broadcasted_iota(jnp.int32, (T, T), 0)
        qry = lax.broadcasted_iota(jnp.int32, (T, T), 1)
        tab = lambda b: bias_ref[b, h] * LOG2E
        visible = (key // CHUNK) <= (qry // CHUNK)
        far = FAR_TILES_PER_TRIP - 1
        bn_ref[0:far * T, :] = jnp.full((far * T, T), tab(15), F32)
        bn_ref[far * T:(far + 1) * T, :] = _rel_bias_tile(key - qry - T, tab)
        bn_ref[(far + 1) * T:(far + 2) * T, :] = jnp.where(visible, _rel_bias_tile(key - qry, tab), NEG)

    if online:
        m_ref, = m_scratch
        m_ref[...] = jnp.full(m_ref.shape, NEG, F32)
    l_ref[...] = jnp.zeros(l_ref.shape, F32)
    acc_ref[...] = jnp.zeros(acc_ref.shape, F32)
    qt = qt_ref[0]
    dim = lax.broadcasted_iota(jnp.int32, qt.shape, 0)
    zero = jnp.zeros_like(qt)
    qc = (jnp.where(dim < HEAD_DIM, qt, zero), jnp.where(dim >= HEAD_DIM, qt, zero))

    def update(j, bias, width=T):
        start = pl.multiple_of(j * T, T)
        kt = k_ref[0, pl.ds(start, width), :]
        vt = vt_ref[0, :, pl.ds(start, width)]
        for c in range(2):
            s = _dot(kt, qc[c]) + bias
            if online:
                m_old = m_ref[c]
                m_new = jnp.maximum(m_old, jnp.max(s, axis=0, keepdims=True))
                alpha = jnp.exp2(m_old - m_new)
                p = jnp.exp2(s - m_new)
                l_ref[c] = alpha * l_ref[c] + _sublane_partial_sum(p)
                acc_ref[c] = alpha * acc_ref[c] + _dot(vt, p.astype(BF16))
                m_ref[c] = m_new
            else:
                p = jnp.exp2(s)
                l_ref[c] += _sublane_partial_sum(p)
                acc_ref[c] += _dot(vt, p.astype(BF16))

    far_bias = bias_ref[15, h] * LOG2E

    n_far = jnp.maximum(i - 1, 0)

    def far_body(jj, carry):
        update(FAR_TILES_PER_TRIP * jj, far_bias, width=FAR_TILES_PER_TRIP * T)
        return carry

    n_trips = n_far // FAR_TILES_PER_TRIP
    lax.fori_loop(0, n_trips, far_body, 0)

    rest = n_far - n_trips * FAR_TILES_PER_TRIP
    tail_tiles = FAR_TILES_PER_TRIP + 1
    for r in range(FAR_TILES_PER_TRIP):
        @pl.when((i >= 1) & (rest == r))
        def _(r=r):
            update(i - 1 - r, bn_ref[(tail_tiles - 2 - r) * T:tail_tiles * T, :], width=(r + 2) * T)

    @pl.when(i == 0)
    def _():
        update(0, bn_ref[(tail_tiles - 1) * T:tail_tiles * T, :])

    o0 = acc_ref[0] / jnp.sum(l_ref[0], axis=0, keepdims=True)
    o1 = acc_ref[1] / jnp.sum(l_ref[1], axis=0, keepdims=True)
    o = o0 - lam_ref[0] * o1
    y = o * lax.rsqrt(jnp.mean(o * o, axis=0, keepdims=True) + EPS) * g_ref[...] * out_scale
    o_ref[...] = y.T.astype(o_ref.dtype)


def _attn_prompt(rel_bias, lam, qth, kh, vth, g_subln, out_scale, online):
    n_heads, Lp, _ = kh.shape
    T = ROW_TILE
    assert T % CHUNK == 0 and T >= FAR_DISTANCE and FAR_TILES_PER_TRIP & (FAR_TILES_PER_TRIP - 1) == 0
    m_scratch = [pltpu.VMEM((2, 1, T), F32)] if online else []
    return pl.pallas_call(
        functools.partial(_attn_prompt_kernel, out_scale=out_scale),
        grid=(n_heads, Lp // T),
        in_specs=[
            pl.BlockSpec(memory_space=pltpu.SMEM),
            pl.BlockSpec(memory_space=pltpu.SMEM),
            pl.BlockSpec((1, V_DIM, T), lambda h, i: (h, 0, i)),
            pl.BlockSpec((1, Lp, V_DIM), lambda h, i: (h, 0, 0)),
            pl.BlockSpec((1, V_DIM, Lp), lambda h, i: (h, 0, 0)),
            pl.BlockSpec((V_DIM, 1), lambda h, i: (0, 0)),
        ],
        out_specs=pl.BlockSpec((T, V_DIM), lambda h, i: (i, h)),
        out_shape=jax.ShapeDtypeStruct((Lp, n_heads * V_DIM), BF16),
        scratch_shapes=[
            pltpu.VMEM(((FAR_TILES_PER_TRIP + 1) * T, T), F32),
            pltpu.VMEM((2, 8, T), F32),
            pltpu.VMEM((2, V_DIM, T), F32),
        ] + m_scratch,
        compiler_params=_params("arbitrary", "arbitrary"),
        name="attn_prompt_online" if online else "attn_prompt",
    )(rel_bias, lam, qth, kh, vth, g_subln)


def _attn_sample_kernel(lam_ref, q_ref, ckt_ref, cv_ref, kn_ref, vn_ref, tab_ref, g_ref,
                        o_ref, nearb_ref, newb_ref, qbd_ref, m_ref, l_ref, acc_ref, *, out_scale):
    b = pl.program_id(0)
    kb = pl.program_id(1)
    nkb = pl.num_programs(1)
    R, Tk = nearb_ref.shape
    L = kn_ref.shape[0]
    D = q_ref.shape[1]
    n_heads = D // V_DIM

    @pl.when((b == 0) & (kb == 0))
    def _():
        tab = lambda bkt: tab_ref[:, bkt:bkt + 1]
        t_near = lax.broadcasted_iota(jnp.int32, (R, Tk), 0) % L
        col = lax.broadcasted_iota(jnp.int32, (R, Tk), 1)
        nearb_ref[...] = _rel_bias_tile(col - Tk - t_near, tab)
        t_new = lax.broadcasted_iota(jnp.int32, (R, L), 0) % L
        col_new = lax.broadcasted_iota(jnp.int32, (R, L), 1)
        newb_ref[...] = _rel_bias_tile(col_new - t_new, tab)

    @pl.when(kb == 0)
    def _():
        m_ref[...] = jnp.full(m_ref.shape, NEG, F32)
        l_ref[...] = jnp.zeros(l_ref.shape, F32)
        acc_ref[...] = jnp.zeros(acc_ref.shape, F32)
        q_rows = jnp.concatenate([q_ref[...]] * (R // L), axis=0)
        row_group = lax.broadcasted_iota(jnp.int32, (R, D), 0) // L
        col_group = lax.broadcasted_iota(jnp.int32, (R, D), 1) // HEAD_DIM
        qbd_ref[...] = jnp.where(row_group == col_group, q_rows, jnp.zeros_like(q_rows))

    rows_per_head = 2 * L

    def update(s, value_of_head):
        m_old = m_ref[...]
        m_new = jnp.maximum(m_old, jnp.max(s, axis=-1, keepdims=True))
        alpha = jnp.exp2(m_old - m_new)
        p = jnp.exp2(s - m_new)
        l_ref[...] = alpha * l_ref[...] + jnp.sum(p, axis=-1, keepdims=True)
        pb = p.astype(BF16)
        for h in range(n_heads):
            rs = slice(h * rows_per_head, (h + 1) * rows_per_head)
            acc_ref[rs, :] = alpha[rs, :] * acc_ref[rs, :] + _dot(pb[rs, :], value_of_head(h))
        m_ref[...] = m_new

    qbd = qbd_ref[...]
    s = _dot(qbd, ckt_ref[0].astype(BF16))
    cache_value = lambda h: cv_ref[0, pl.ds(h, Tk, stride=n_heads), :].astype(BF16)

    @pl.when(kb < nkb - 1)
    def _():
        update(s + tab_ref[:, 15:16], cache_value)

    @pl.when(kb == nkb - 1)
    def _():
        update(s + nearb_ref[...], cache_value)
        s_new = _dot_nt(qbd, kn_ref[...].astype(BF16)) + newb_ref[...]
        update(s_new, lambda h: vn_ref[:, h * V_DIM:(h + 1) * V_DIM].astype(BF16))
        lam = lam_ref[0]
        o = acc_ref[...] / l_ref[...]
        for h in range(n_heads):
            r0 = h * rows_per_head
            o_ref[:, h * V_DIM:(h + 1) * V_DIM] = _diff_out(
                o[r0:r0 + L, :], o[r0 + L:r0 + 2 * L, :], lam, g_ref[...], out_scale).astype(o_ref.dtype)


def _attn_sample(lam, q, cache_k, cache_v, k_new, v_new, tab, g_subln, out_scale):
    B, past, n_heads = cache_v.shape[:3]
    D = n_heads * V_DIM
    L = k_new.shape[0] // B
    R = (D // HEAD_DIM) * L
    Tk = min(2048, past)
    assert past % Tk == 0 and Tk >= FAR_DISTANCE + L
    cache_kt = jnp.transpose(cache_k, (0, 2, 3, 4, 1)).reshape(B, D, past)
    return pl.pallas_call(
        functools.partial(_attn_sample_kernel, out_scale=out_scale),
        grid=(B, past // Tk),
        in_specs=[
            pl.BlockSpec(memory_space=pltpu.SMEM),
            pl.BlockSpec((L, D), lambda b, k: (b, 0)),
            pl.BlockSpec((1, D, Tk), lambda b, k: (b, 0, k)),
            pl.BlockSpec((1, Tk * n_heads, V_DIM), lambda b, k: (b, k, 0)),
            pl.BlockSpec((L, D), lambda b, k: (b, 0)),
            pl.BlockSpec((L, D), lambda b, k: (b, 0)),
            pl.BlockSpec(tab.shape, lambda b, k: (0, 0)),
            pl.BlockSpec((1, V_DIM), lambda b, k: (0, 0)),
        ],
        out_specs=pl.BlockSpec((L, D), lambda b, k: (b, 0)),
        out_shape=jax.ShapeDtypeStruct((B * L, D), BF16),
        scratch_shapes=[
            pltpu.VMEM((R, Tk), F32),
            pltpu.VMEM((R, L), F32),
            pltpu.VMEM((R, D), BF16),
            pltpu.VMEM((R, 1), F32),
            pltpu.VMEM((R, 1), F32),
            pltpu.VMEM((R, V_DIM), F32),
        ],
        compiler_params=_params("arbitrary", "arbitrary"),
        name="attn_sample",
    )(lam, q, cache_kt, cache_v.reshape(B, past * n_heads, V_DIM), k_new, v_new, tab, g_subln)


def _oproj_router_kernel(op_ref, os_ref, x_ref, wo_ref, g_ref, wrh_ref, wrl_ref,
                         x3_ref, h_ref, route_ref, route_t_ref):
    i = pl.program_id(0)
    nt = pl.num_programs(0) - 1
    o = jnp.where(i == nt, os_ref[...], op_ref[...])
    x3 = x_ref[...] + _dot(o, wo_ref[...])
    x3_ref[...] = x3
    h = _rms_unit(x3) * g_ref[...]
    h_hi = h.astype(BF16)
    h_ref[...] = h_hi
    h_lo = (h - h_hi.astype(F32)).astype(BF16)
    logits = _dot(h_hi, wrh_ref[...]) + (_dot(h_lo, wrh_ref[...]) + _dot(h_hi, wrl_ref[...]))
    lane = lax.broadcasted_iota(jnp.int32, logits.shape, 1)
    lg = jnp.where(lane < N_EXPERTS, logits, -jnp.inf)
    m1 = jnp.max(lg, axis=-1, keepdims=True)
    i1 = jnp.min(jnp.where(lg == m1, lane, LANE), axis=-1, keepdims=True)
    lg2 = jnp.where(lane == i1, -jnp.inf, lg)
    m2 = jnp.max(lg2, axis=-1, keepdims=True)
    i2 = jnp.min(jnp.where(lg2 == m2, lane, LANE), axis=-1, keepdims=True)
    e2 = jnp.exp(m2 - m1)
    den = 1.0 + e2
    g1 = 1.0 / den
    g2 = e2 / den
    route = jnp.where(lane == 0, i1.astype(F32),
                      jnp.where(lane == 1, i2.astype(F32),
                                jnp.where(lane == 2, g1,
                                          jnp.where(lane == 3, g2, 0.0))))
    route_ref[...] = route
    route_t_ref[...] = route.T


def _oproj_router(o_p, o_s, x, wo, g, wr_hi, wr_lo):
    N, D = x.shape
    T = ROW_TILE
    nt = N // T - 1
    const2 = lambda i: (0, 0)
    row = lambda i: (i, 0)
    return pl.pallas_call(
        _oproj_router_kernel,
        grid=(nt + 1,),
        in_specs=[
            pl.BlockSpec((T, D), lambda i: (jnp.minimum(i, nt - 1), 0)),
            pl.BlockSpec((T, D), const2),
            pl.BlockSpec((T, D), row),
            pl.BlockSpec((D, D), const2),
            pl.BlockSpec((1, D), const2),
            pl.BlockSpec((D, LANE), const2),
            pl.BlockSpec((D, LANE), const2),
        ],
        out_specs=[
            pl.BlockSpec((T, D), row),
            pl.BlockSpec((T, D), row),
            pl.BlockSpec((T, LANE), row),
            pl.BlockSpec((LANE, T), lambda i: (0, i)),
        ],
        out_shape=[
            jax.ShapeDtypeStruct((N, D), F32),
            jax.ShapeDtypeStruct((N, D), BF16),
            jax.ShapeDtypeStruct((N, LANE), F32),
            jax.ShapeDtypeStruct((LANE, N), F32),
        ],
        compiler_params=_params("arbitrary"),
        name="oproj_router",
    )(o_p, o_s, x, wo, g, wr_hi, wr_lo)


MOE_CHUNK = 128
MOE_ROW_ALIGN = 16


def _for_each_chunk(cnt_ref, step, n_chunks, fn, first=0):
    for e in range(N_EXPERTS):
        for k in range(first, n_chunks):
            if k == 0:
                fn(e, k)
            else:
                @pl.when(cnt_ref[step * N_EXPERTS + e] > k * MOE_CHUNK)
                def _(e=e, k=k):
                    fn(e, k)


def _dispatch_kernel(off_ref, cnt_ref, hb_ref, rt_ref, utri_ref, xs_in_ref, xs_ref, stage_ref, sem):
    del xs_in_ref
    t = pl.program_id(0)
    T = hb_ref.shape[0]
    S = MOE_CHUNK
    nk = T // S

    def chunk_copy(step, e, k):
        start = pl.multiple_of(off_ref[step * N_EXPERTS + e] + k * S, MOE_ROW_ALIGN)
        slot = k * N_EXPERTS + e
        return pltpu.make_async_copy(stage_ref.at[slot], xs_ref.at[pl.ds(start, S), :], sem.at[slot])

    rt = rt_ref[...]
    expert_id = lax.broadcasted_iota(jnp.int32, (N_EXPERTS, T), 0).astype(F32)
    member = (rt[0:1, :] == expert_id) | (rt[1:2, :] == expert_id)
    rank = _dot(member.astype(BF16), utri_ref[...])
    rank = jnp.where(member, rank, 0.0)
    hb = hb_ref[...]

    row3 = lax.broadcasted_iota(jnp.int32, (1, S, 1), 1)
    sel_first = (rank[:, None, :] == (row3 + 1).astype(F32)).astype(BF16).reshape(N_EXPERTS * S, T)
    first = _dot(sel_first, hb).astype(BF16).reshape(N_EXPERTS, S, hb.shape[1])

    @pl.when(t > 0)
    def _():
        _for_each_chunk(cnt_ref, t - 1, nk, lambda e, k: chunk_copy(t - 1, e, k).wait())

    stage_ref[0:N_EXPERTS] = first
    for e in range(N_EXPERTS):
        chunk_copy(t, e, 0).start()

    row = lax.broadcasted_iota(jnp.int32, (S, 1), 0)

    def emit(e, k):
        sel = rank[e:e + 1, :] == (row + (k * S + 1)).astype(F32)
        stage_ref[k * N_EXPERTS + e] = _dot(sel.astype(BF16), hb).astype(BF16)
        chunk_copy(t, e, k).start()

    _for_each_chunk(cnt_ref, t, nk, emit, first=1)

    @pl.when(t == pl.num_programs(0) - 1)
    def _():
        _for_each_chunk(cnt_ref, t, nk, lambda e, k: chunk_copy(t, e, k).wait())


def _dispatch(off, cnt, hb, route_t, utri, n_rows):
    N, D = hb.shape
    T = ROW_TILE
    nk = T // MOE_CHUNK
    n_slots = N_EXPERTS * nk
    xs_init = jnp.zeros((n_rows, D), BF16)
    return pl.pallas_call(
        _dispatch_kernel,
        grid_spec=pltpu.PrefetchScalarGridSpec(
            num_scalar_prefetch=2,
            grid=(N // T,),
            in_specs=[
                pl.BlockSpec((T, D), lambda t, off, cnt: (t, 0)),
                pl.BlockSpec((LANE, T), lambda t, off, cnt: (0, t)),
                pl.BlockSpec((T, T), lambda t, off, cnt: (0, 0)),
                pl.BlockSpec(memory_space=pl.ANY),
            ],
            out_specs=pl.BlockSpec(memory_space=pl.ANY),
            scratch_shapes=[
                pltpu.VMEM((n_slots, MOE_CHUNK, D), BF16),
                pltpu.SemaphoreType.DMA((n_slots,)),
            ],
        ),
        out_shape=jax.ShapeDtypeStruct((n_rows, D), BF16),
        input_output_aliases={5: 0},
        compiler_params=_params("arbitrary"),
        name="moe_dispatch",
    )(off, cnt, hb, route_t, utri, xs_init)


def _experts_kernel(te_ref, na_ref, xs_ref, wg_ref, wu_ref, wd_ref, out_ref, acc_ref):
    i = pl.program_id(0)
    f = pl.program_id(1)
    last = pl.num_programs(1) - 1
    active = i < na_ref[0]

    @pl.when(active)
    def _():
        @pl.when(f == 0)
        def _():
            acc_ref[...] = jnp.zeros(acc_ref.shape, F32)

        acc_ref[...] += _swiglu_tile(xs_ref[...], wg_ref.at[0], wu_ref.at[0], wd_ref.at[0])

        @pl.when(f == last)
        def _():
            out_ref[...] = acc_ref[...].astype(out_ref.dtype)

    @pl.when(jnp.logical_not(active) & (f == last))
    def _():
        out_ref[...] = jnp.zeros(out_ref.shape, out_ref.dtype)


def _experts(tile_expert, n_active, xs, wg, wu, wd):
    P, D = xs.shape
    T = ROW_TILE
    nf, tf = _ffn_splits(wg.shape[2])
    fidx = lambda i, f, na: jnp.where(i < na[0], f, nf - 1)
    return pl.pallas_call(
        _experts_kernel,
        grid_spec=pltpu.PrefetchScalarGridSpec(
            num_scalar_prefetch=2,
            grid=(P // T, nf),
            in_specs=[
                pl.BlockSpec((T, D), lambda i, f, te, na: (i, 0)),
                pl.BlockSpec((1, D, tf), lambda i, f, te, na: (te[i], 0, fidx(i, f, na))),
                pl.BlockSpec((1, D, tf), lambda i, f, te, na: (te[i], 0, fidx(i, f, na))),
                pl.BlockSpec((1, tf, D), lambda i, f, te, na: (te[i], fidx(i, f, na), 0)),
            ],
            out_specs=pl.BlockSpec((T, D), lambda i, f, te, na: (i, 0)),
            scratch_shapes=[pltpu.VMEM((T, D), F32)],
        ),
        out_shape=jax.ShapeDtypeStruct((P, D), BF16),
        compiler_params=_params("arbitrary", "arbitrary"),
        name="moe_experts",
    )(tile_expert, n_active, xs, wg, wu, wd)


def _combine_kernel(off_ref, cnt_ref, x_ref, route_ref, ltri_ref, ys_hbm, yp_ref, ysm_ref,
                    buf_ref, acc_ref, sem):
    t = pl.program_id(0)
    nt = pl.num_programs(0) - 1
    T = x_ref.shape[0]
    S = MOE_CHUNK
    nk = T // S

    def chunk_copy(step, e, k):
        start = pl.multiple_of(off_ref[step * N_EXPERTS + e] + k * S, MOE_ROW_ALIGN)
        bank, slot = step % 2, k * N_EXPERTS + e
        return pltpu.make_async_copy(ys_hbm.at[pl.ds(start, S), :], buf_ref.at[bank, slot], sem.at[bank, slot])

    @pl.when(t == 0)
    def _():
        _for_each_chunk(cnt_ref, t, nk, lambda e, k: chunk_copy(t, e, k).start())

    @pl.when(t < nt)
    def _():
        _for_each_chunk(cnt_ref, t + 1, nk, lambda e, k: chunk_copy(t + 1, e, k).start())

    route = route_ref[...]
    e1, e2, g1, g2 = (route[:, j:j + 1] for j in range(2 * TOP_K))
    lane = lax.broadcasted_iota(jnp.int32, route.shape, 1).astype(F32)
    member = (lane == e1) | (lane == e2)
    rank = _dot(ltri_ref[...], member.astype(BF16))
    rank = jnp.where(member, rank, 0.0)
    col = lax.broadcasted_iota(jnp.int32, (1, S), 1)
    gate_of = lambda e: jnp.where(e1 == e, g1, 0.0) + jnp.where(e2 == e, g2, 0.0)
    select = lambda e, k: rank[:, e:e + 1] == (col + (k * S + 1)).astype(F32)

    for e in range(N_EXPERTS):
        chunk_copy(t, e, 0).wait()
    weight = jnp.concatenate([jnp.where(select(e, 0), gate_of(e), 0.0) for e in range(N_EXPERTS)], axis=-1)
    w_hi = weight.astype(BF16)
    w_lo = (weight - w_hi.astype(F32)).astype(BF16)
    rows = buf_ref[t % 2, 0:N_EXPERTS].reshape(N_EXPERTS * S, x_ref.shape[1])
    acc_ref[...] = x_ref[...] + (_dot(w_hi, rows) + _dot(w_lo, rows))

    def absorb(e, k):
        chunk_copy(t, e, k).wait()
        acc_ref[...] += gate_of(e) * _dot(select(e, k).astype(BF16), buf_ref[t % 2, k * N_EXPERTS + e])

    _for_each_chunk(cnt_ref, t, nk, absorb, first=1)

    @pl.when(t < nt)
    def _():
        yp_ref[...] = acc_ref[...]

    @pl.when(t == nt)
    def _():
        ysm_ref[...] = acc_ref[...]


def _combine(off, cnt, x3, route, ltri, ys, n_sample):
    N, D = x3.shape
    T = ROW_TILE
    nt = N // T - 1
    n_slots = N_EXPERTS * (T // MOE_CHUNK)
    return pl.pallas_call(
        _combine_kernel,
        grid_spec=pltpu.PrefetchScalarGridSpec(
            num_scalar_prefetch=2,
            grid=(nt + 1,),
            in_specs=[
                pl.BlockSpec((T, D), lambda t, off, cnt: (t, 0)),
                pl.BlockSpec((T, LANE), lambda t, off, cnt: (t, 0)),
                pl.BlockSpec((T, T), lambda t, off, cnt: (0, 0)),
                pl.BlockSpec(memory_space=pl.ANY),
            ],
            out_specs=[
                pl.BlockSpec((T, D), lambda t, off, cnt: (jnp.minimum(t, nt - 1), 0)),
                pl.BlockSpec((T, D), lambda t, off, cnt: (0, 0)),
            ],
            scratch_shapes=[
                pltpu.VMEM((2, n_slots, MOE_CHUNK, D), BF16),
                pltpu.VMEM((T, D), F32),
                pltpu.SemaphoreType.DMA((2, n_slots)),
            ],
        ),
        out_shape=[
            jax.ShapeDtypeStruct((N - n_sample, D), F32),
            jax.ShapeDtypeStruct((n_sample, D), F32),
        ],
        compiler_params=_params("arbitrary"),
        name="moe_combine",
    )(off, cnt, x3, route, ltri, ys)


def _routing_tables(route, tile):
    n = route.shape[0]
    n_tok_tiles = n // tile
    experts = route[:, :TOP_K].astype(jnp.int32)
    onehot = (experts[:, :, None] == jnp.arange(N_EXPERTS)[None, None, :]).astype(jnp.int32).sum(axis=1)
    cnt = onehot.reshape(n_tok_tiles, tile, N_EXPERTS).sum(axis=1)
    span = (cnt + MOE_ROW_ALIGN - 1) // MOE_ROW_ALIGN * MOE_ROW_ALIGN
    totals = span.sum(axis=0)
    tiles_per = (totals + MOE_CHUNK + tile - 1) // tile
    tile_end = jnp.cumsum(tiles_per)
    group_off = (tile_end - tiles_per) * tile
    off = group_off[None, :] + jnp.cumsum(span, axis=0) - span
    max_rows = n * TOP_K + N_EXPERTS * (MOE_CHUNK + n_tok_tiles * (MOE_ROW_ALIGN - 1))
    n_row_tiles = max_rows // tile + N_EXPERTS
    n_active = tile_end[-1:].astype(jnp.int32)
    tile_ids = jnp.minimum(jnp.arange(n_row_tiles, dtype=jnp.int32), n_active[0] - 1)
    tile_expert = jnp.sum((tile_ids[:, None] >= tile_end[None, :]).astype(jnp.int32), axis=1)
    return (off.reshape(-1).astype(jnp.int32), cnt.reshape(-1).astype(jnp.int32),
            tile_expert, n_active, n_row_tiles * tile)


def _group_matrices(d_model):
    n_groups = d_model // HEAD_DIM
    gsum = np.zeros((d_model, LANE), np.float32)
    gsum[np.arange(d_model), np.arange(d_model) // HEAD_DIM] = 1.0
    assert n_groups <= LANE
    return jnp.asarray(gsum, BF16), jnp.asarray(np.concatenate([gsum.T, gsum.T], axis=0), BF16)


def kernel(x_prompt, x_sample, cache_k, cache_v, state_pool, g_pool_norm, w_pool, pool_scale, g_attn, w_q, g_qn, lambda_q1, lambda_k1, lambda_q2, lambda_k2, g_subln, w_o, g_kv, w_k, w_v, g_kn, rel_bias, g_ffn, w_gate_dense, w_up_dense, w_down_dense, w_router, w_gate_moe, w_up_moe, w_down_moe):
    Bp, Lp, D = x_prompt.shape
    Bs, Ls, _ = x_sample.shape
    past = cache_k.shape[1]
    n_heads = D // V_DIM
    n_sample = Bs * Ls
    assert Bp == 1 and n_sample == ROW_TILE and Lp % ROW_TILE == 0
    assert g_pool_norm.shape[0] == 1 and g_attn.shape[0] == 1
    bf = lambda a: a.astype(BF16)
    row = lambda a: a.reshape(1, -1)

    x1, pool_p, pool_s = _pool_layer(
        x_prompt.reshape(Lp, D), x_sample.reshape(n_sample, D), state_pool[0],
        row(g_pool_norm[0]), bf(w_pool[0]), row(pool_scale[0]), past)
    x2 = _dense_ffn(x1, row(g_ffn[0]), bf(w_gate_dense[0]), bf(w_up_dense[0]), bf(w_down_dense[0]))

    layer = 1
    lambda_init = 0.8 - 0.6 * math.exp(-0.3 * layer)
    lam = (jnp.exp(jnp.sum(lambda_q1[0] * lambda_k1[0])) - jnp.exp(jnp.sum(lambda_q2[0] * lambda_k2[0]))
           + lambda_init).reshape(1)
    gsum, gbc = _group_matrices(D)
    n_groups = D // HEAD_DIM
    k_p, v_p, k_s, v_s, qh, kh, vh, q_s = _qkv_proj(
        x2, row(g_kv), row(g_attn[0]), bf(w_k), bf(w_v), bf(w_q[0]),
        row(jnp.tile(g_kn, n_groups)), row(jnp.tile(g_qn[0], n_groups) * (HEAD_DIM ** -0.5 * LOG2E)),
        gsum, gbc, n_sample)
    out_scale = 1.0 - lambda_init
    g_sub = row(g_subln[0])
    score_bound = LOG2E * (math.sqrt(HEAD_DIM) * jnp.max(jnp.abs(g_qn[0])) * jnp.max(jnp.abs(g_kn))
                           + jnp.max(jnp.abs(rel_bias)))
    o_p = lax.cond(
        score_bound <= UNSHIFTED_SCORE_LIMIT,
        functools.partial(_attn_prompt, out_scale=out_scale, online=False),
        functools.partial(_attn_prompt, out_scale=out_scale, online=True),
        rel_bias, lam, qh, kh, vh, g_subln[0].reshape(V_DIM, 1))

    tab = jnp.repeat(rel_bias.T, 2 * Ls, axis=0) * LOG2E
    o_s = _attn_sample(lam, q_s, cache_k, cache_v, k_s, v_s, tab, g_sub, out_scale)

    wr = jnp.pad(w_router[0], ((0, 0), (0, LANE - N_EXPERTS)))
    wr_hi = bf(wr)
    wr_lo = bf(wr - wr_hi.astype(F32))
    x3, h_moe, route, route_t = _oproj_router(o_p, o_s, x2, bf(w_o[0]), row(g_ffn[1]), wr_hi, wr_lo)
    off, cnt, tile_expert, n_active, n_rows = _routing_tables(route, ROW_TILE)
    ltri = jnp.asarray(np.tril(np.ones((ROW_TILE, ROW_TILE), np.float32)), BF16)
    xs = _dispatch(off, cnt, h_moe, route_t, ltri.T, n_rows)
    ys = _experts(tile_expert, n_active, xs, bf(w_gate_moe[0]), bf(w_up_moe[0]), bf(w_down_moe[0]))
    y_p, y_s = _combine(off, cnt, x3, route, ltri, ys, n_sample)

    return (y_p.reshape(Bp, Lp, D), y_s.reshape(Bs, Ls, D),
            k_p.reshape(Bp, Lp, n_heads, 2, HEAD_DIM), v_p.reshape(Bp, Lp, n_heads, V_DIM),
            pool_p.reshape(1, Bp, POOL_STATE, D),
            k_s.reshape(Bs, Ls, n_heads, 2, HEAD_DIM), v_s.reshape(Bs, Ls, n_heads, V_DIM),
            pool_s.reshape(1, Bs, POOL_STATE, D))
```

```python
import functools
import math

import numpy as np
import jax
import jax.numpy as jnp
from jax import lax
from jax.experimental import pallas as pl
from jax.experimental.pallas import tpu as pltpu

EPS = 1e-6
CHUNK = 64
POOL_WINDOWS = (2, 4, 8, 16)
POOL_STATE = max(POOL_WINDOWS) - 1
POOL_LEAD = 24
HEAD_DIM = 64
V_DIM = 2 * HEAD_DIM
N_EXPERTS = 8
TOP_K = 2
BUCKET_UPPER = (1, 2, 3, 4, 5, 6, 7, 8, 12, 16, 23, 32, 46, 64, 91)
FAR_DISTANCE = 128
NEG = -1e30
LOG2E = math.log2(math.e)
UNSHIFTED_SCORE_LIMIT = 80.0

ROW_TILE = 512
FAR_TILES_PER_TRIP = 8
LANE = 128
MXU_DIM = 256
VMEM_LIMIT = 56 * 1024 * 1024

F32 = jnp.float32
BF16 = jnp.bfloat16


def _dot(a, b):
    return jnp.dot(a, b, preferred_element_type=F32)


def _dot_nt(a, b):
    return lax.dot_general(a, b, (((1,), (1,)), ((), ())), preferred_element_type=F32)


def _rms_unit(x):
    return x * lax.rsqrt(jnp.mean(x * x, axis=-1, keepdims=True) + EPS)


def _rel_bias_tile(rel, tab):
    n = jnp.abs(rel)
    neg = tab(15)
    pos = tab(31)
    for b in range(14, -1, -1):
        lt = n < BUCKET_UPPER[b]
        neg = jnp.where(lt, tab(b), neg)
        pos = jnp.where(lt, tab(16 + b), pos)
    return jnp.where(rel > 0, pos, neg)


def _params(*sem):
    return pltpu.CompilerParams(dimension_semantics=sem, vmem_limit_bytes=VMEM_LIMIT)


def _pool_kernel(xp_ref, xs_ref, st_ref, g_ref, w_ref, sc_ref,
                 x1_ref, pp_ref, ps_ref, ext_ref, lva_ref, lvb_ref, ext3_ref, *, past_len):
    i = pl.program_id(0)
    nt = pl.num_programs(0) - 1
    T, D = xp_ref.shape
    gw = D // len(POOL_WINDOWS)

    @pl.when(i < nt)
    def _prompt():
        x = xp_ref[...]
        h = _rms_unit(x) * g_ref[...]

        @pl.when(i == 0)
        def _():
            ext_ref[0:POOL_LEAD, :] = jnp.zeros((POOL_LEAD, D), F32)
            for buf in (lva_ref, lvb_ref):
                buf[0:8, :] = jnp.zeros((8, gw), F32)

        ext_ref[POOL_LEAD:POOL_LEAD + T, :] = h
        row = i * T + lax.broadcasted_iota(jnp.int32, (T, 1), 0)
        parts = []
        for gi, w in enumerate(POOL_WINDOWS):
            c0 = gi * gw
            read = lambda lo, n: ext_ref[lo:lo + n, c0:c0 + gw]
            n_levels = w.bit_length() - 1
            for k in range(n_levels):
                d = 1 << k
                if k == n_levels - 1:
                    s = read(POOL_LEAD, T) + read(POOL_LEAD - d, T)
                else:
                    buf = (lva_ref, lvb_ref)[k % 2]
                    buf[8:POOL_LEAD + T, :] = read(8, 16 + T) + read(8 - d, 16 + T)
                    read = lambda lo, n, buf=buf: buf[lo:lo + n, :]
            cnt = jnp.minimum(w, row + 1).astype(F32)
            pooled = s / cnt - h[:, c0:c0 + gw]
            parts.append(_dot(pooled.astype(BF16), w_ref[gi]))
        mix = jnp.concatenate(parts, axis=-1) * sc_ref[...]
        x1_ref[...] = x + mix
        tail = ext_ref[T + 8:T + POOL_LEAD, :]
        ext_ref[8:POOL_LEAD, :] = tail

        @pl.when(i == nt - 1)
        def _():
            pp_ref[...] = tail[1:16, :]

    @pl.when(i == nt)
    def _sample():
        B = st_ref.shape[0]
        L = T // B
        x = xs_ref[...]
        h = _rms_unit(x) * g_ref[...]
        ext3_ref[:, 1:16, :] = st_ref[...]
        ext3_ref[:, 16:16 + L, :] = h.reshape(B, L, D)
        t = lax.broadcasted_iota(jnp.int32, (1, L, 1), 1)
        parts = []
        for gi, w in enumerate(POOL_WINDOWS):
            c0 = gi * gw
            s = ext3_ref[:, 16:16 + L, c0:c0 + gw]
            for j in range(1, w):
                s = s + ext3_ref[:, 16 - j:16 - j + L, c0:c0 + gw]
            cnt = jnp.minimum(w, past_len + t + 1).astype(F32)
            pooled = (s / cnt).reshape(T, gw) - h[:, c0:c0 + gw]
            parts.append(_dot(pooled.astype(BF16), w_ref[gi]))
        mix = jnp.concatenate(parts, axis=-1) * sc_ref[...]
        x1_ref[...] = x + mix
        ps_ref[...] = ext3_ref[:, 16 + L - POOL_STATE:16 + L, :]


def _pool_layer(xp, xs, state, g, w, sc, past_len):
    Lp, D = xp.shape
    T = ROW_TILE
    nt = Lp // T
    B = state.shape[0]
    L = xs.shape[0] // B
    return pl.pallas_call(
        functools.partial(_pool_kernel, past_len=past_len),
        grid=(nt + 1,),
        in_specs=[
            pl.BlockSpec((T, D), lambda i: (jnp.minimum(i, nt - 1), 0)),
            pl.BlockSpec((T, D), lambda i: (0, 0)),
            pl.BlockSpec((B, POOL_STATE, D), lambda i: (0, 0, 0)),
            pl.BlockSpec((1, D), lambda i: (0, 0)),
            pl.BlockSpec(w.shape, lambda i: (0, 0, 0)),
            pl.BlockSpec((1, D), lambda i: (0, 0)),
        ],
        out_specs=[
            pl.BlockSpec((T, D), lambda i: (i, 0)),
            pl.BlockSpec((POOL_STATE, D), lambda i: (0, 0)),
            pl.BlockSpec((B, POOL_STATE, D), lambda i: (0, 0, 0)),
        ],
        out_shape=[
            jax.ShapeDtypeStruct((Lp + T, D), F32),
            jax.ShapeDtypeStruct((POOL_STATE, D), F32),
            jax.ShapeDtypeStruct((B, POOL_STATE, D), F32),
        ],
        scratch_shapes=[
            pltpu.VMEM((POOL_LEAD + T, D), F32),
            pltpu.VMEM((POOL_LEAD + T, D // len(POOL_WINDOWS)), F32),
            pltpu.VMEM((POOL_LEAD + T, D // len(POOL_WINDOWS)), F32),
            pltpu.VMEM((B, 16 + L, D), F32),
        ],
        compiler_params=_params("arbitrary"),
        name="pool_mixer",
    )(xp, xs, state, g, w, sc)


def _swiglu_tile(hb, wg_ref, wu_ref, wd_ref):
    tf = wg_ref.shape[1]
    step = 3 * MXU_DIM
    out = None
    for c0 in range(0, tf, step):
        c1 = min(c0 + step, tf)
        gt = _dot(hb, wg_ref[:, c0:c1])
        ut = _dot(hb, wu_ref[:, c0:c1])
        a = gt * jax.nn.sigmoid(gt) * ut
        part = _dot(a.astype(BF16), wd_ref[c0:c1, :])
        out = part if out is None else out + part
    return out


def _ffn_kernel(x_ref, g_ref, wg_ref, wu_ref, wd_ref, out_ref, hb_ref, acc_ref):
    f = pl.program_id(1)

    @pl.when(f == 0)
    def _():
        x = x_ref[...]
        hb_ref[...] = (_rms_unit(x) * g_ref[...]).astype(BF16)
        acc_ref[...] = x

    acc_ref[...] += _swiglu_tile(hb_ref[...], wg_ref, wu_ref, wd_ref)

    @pl.when(f == pl.num_programs(1) - 1)
    def _():
        out_ref[...] = acc_ref[...]


def _ffn_splits(d_ff):
    nf = 1
    return nf, d_ff // nf


def _dense_ffn(x, g, wg, wu, wd):
    N, D = x.shape
    T = ROW_TILE
    nf, tf = _ffn_splits(wg.shape[1])
    return pl.pallas_call(
        _ffn_kernel,
        grid=(N // T, nf),
        in_specs=[
            pl.BlockSpec((T, D), lambda i, f: (i, 0)),
            pl.BlockSpec((1, D), lambda i, f: (0, 0)),
            pl.BlockSpec((D, tf), lambda i, f: (0, f)),
            pl.BlockSpec((D, tf), lambda i, f: (0, f)),
            pl.BlockSpec((tf, D), lambda i, f: (f, 0)),
        ],
        out_specs=pl.BlockSpec((T, D), lambda i, f: (i, 0)),
        out_shape=jax.ShapeDtypeStruct((N, D), F32),
        scratch_shapes=[pltpu.VMEM((T, D), BF16), pltpu.VMEM((T, D), F32)],
        compiler_params=_params("arbitrary", "arbitrary"),
        name="dense_swiglu",
    )(x, g, wg, wu, wd)


def _qkv_kernel(x_ref, gkv_ref, gq_ref, wk_ref, wv_ref, wq_ref, gkn_ref, gqn_ref,
                gsum_ref, gbc_ref,
                kp_ref, vp_ref, ks_ref, vs_ref, qh_ref, kh_ref, vh_ref, qs_ref):
    i = pl.program_id(0)
    nt = pl.num_programs(0) - 1
    n_heads = qh_ref.shape[0]

    xn = _rms_unit(x_ref[...])
    hkv = (xn * gkv_ref[...]).astype(BF16)
    hq = (xn * gq_ref[...]).astype(BF16)

    def head_norm(y, g):
        ssq = _dot((y * y).astype(BF16), gsum_ref[...])
        rs = lax.rsqrt(ssq * (1.0 / HEAD_DIM) + EPS)
        rs_hi = rs.astype(BF16)
        rs_lo = (rs - rs_hi.astype(F32)).astype(BF16)
        rsb = _dot(jnp.concatenate([rs_hi, rs_lo], axis=-1), gbc_ref[...])
        return y * rsb * g

    k = head_norm(_dot(hkv, wk_ref[...]), gkn_ref[...])
    v = _dot(hkv, wv_ref[...])
    q = head_norm(_dot(hq, wq_ref[...]), gqn_ref[...])

    @pl.when(i < nt)
    def _():
        kp_ref[...] = k
        vp_ref[...] = v
        for h in range(n_heads):
            sl = slice(h * V_DIM, (h + 1) * V_DIM)
            qh_ref[h] = q[:, sl].T.astype(BF16)
            kh_ref[h] = k[:, sl].astype(BF16)
            vh_ref[h] = v[:, sl].T.astype(BF16)

    @pl.when(i == nt)
    def _():
        ks_ref[...] = k
        vs_ref[...] = v
        qs_ref[...] = q.astype(BF16)


def _qkv_proj(x, gkv, gq, wk, wv, wq, gkn_t, gqn_t, gsum, gbc, n_sample):
    N, D = x.shape
    T = ROW_TILE
    nt = N // T - 1
    Lp = N - n_sample
    n_heads = D // V_DIM
    const2 = lambda i: (0, 0)
    prow = lambda i: (jnp.minimum(i, nt - 1), 0)
    phead = lambda i: (0, jnp.minimum(i, nt - 1), 0)
    pheadt = lambda i: (0, 0, jnp.minimum(i, nt - 1))
    return pl.pallas_call(
        _qkv_kernel,
        grid=(nt + 1,),
        in_specs=[
            pl.BlockSpec((T, D), lambda i: (i, 0)),
            pl.BlockSpec((1, D), const2),
            pl.BlockSpec((1, D), const2),
            pl.BlockSpec((D, D), const2),
            pl.BlockSpec((D, D), const2),
            pl.BlockSpec((D, D), const2),
            pl.BlockSpec((1, D), const2),
            pl.BlockSpec((1, D), const2),
            pl.BlockSpec(gsum.shape, const2),
            pl.BlockSpec(gbc.shape, const2),
        ],
        out_specs=[
            pl.BlockSpec((T, D), prow),
            pl.BlockSpec((T, D), prow),
            pl.BlockSpec((T, D), const2),
            pl.BlockSpec((T, D), const2),
            pl.BlockSpec((n_heads, V_DIM, T), pheadt),
            pl.BlockSpec((n_heads, T, V_DIM), phead),
            pl.BlockSpec((n_heads, V_DIM, T), pheadt),
            pl.BlockSpec((T, D), const2),
        ],
        out_shape=[
            jax.ShapeDtypeStruct((Lp, D), F32),
            jax.ShapeDtypeStruct((Lp, D), F32),
            jax.ShapeDtypeStruct((n_sample, D), F32),
            jax.ShapeDtypeStruct((n_sample, D), F32),
            jax.ShapeDtypeStruct((n_heads, V_DIM, Lp), BF16),
            jax.ShapeDtypeStruct((n_heads, Lp, V_DIM), BF16),
            jax.ShapeDtypeStruct((n_heads, V_DIM, Lp), BF16),
            jax.ShapeDtypeStruct((n_sample, D), BF16),
        ],
        compiler_params=_params("arbitrary"),
        name="qkv_proj",
    )(x, gkv, gq, wk, wv, wq, gkn_t, gqn_t, gsum, gbc)


def _diff_out(o0, o1, lam, g, out_scale):
    o = o0 - lam * o1
    return _rms_unit(o) * g * out_scale


def _sublane_partial_sum(p):
    return jnp.sum(p.reshape(p.shape[0] // 8, 8, p.shape[1]), axis=0)


def _attn_prompt_kernel(bias_ref, lam_ref, qt_ref, k_ref, vt_ref, g_ref, o_ref,
                        bn_ref, l_ref, acc_ref, *m_scratch, out_scale):
    h = pl.program_id(0)
    i = pl.program_id(1)
    T = qt_ref.shape[2]
    online = bool(m_scratch)

    @pl.when(i == 0)
    def _():
        key = lax.broadcasted_iota(jnp.int32, (T, T), 0)
        qry = lax.broadcasted_iota(jnp.int32, (T, T), 1)
        tab = lambda b: bias_ref[b, h] * LOG2E
        visible = (key // CHUNK) <= (qry // CHUNK)
        far = FAR_TILES_PER_TRIP - 1
        bn_ref[0:far * T, :] = jnp.full((far * T, T), tab(15), F32)
        bn_ref[far * T:(far + 1) * T, :] = _rel_bias_tile(key - qry - T, tab)
        bn_ref[(far + 1) * T:(far + 2) * T, :] = jnp.where(visible, _rel_bias_tile(key - qry, tab), NEG)

    if online:
        m_ref, = m_scratch
        m_ref[...] = jnp.full(m_ref.shape, NEG, F32)
    l_ref[...] = jnp.zeros(l_ref.shape, F32)
    acc_ref[...] = jnp.zeros(acc_ref.shape, F32)
    qt = qt_ref[0]
    dim = lax.broadcasted_iota(jnp.int32, qt.shape, 0)
    zero = jnp.zeros_like(qt)
    qc = (jnp.where(dim < HEAD_DIM, qt, zero), jnp.where(dim >= HEAD_DIM, qt, zero))

    def update(j, bias, width=T):
        start = pl.multiple_of(j * T, T)
        kt = k_ref[0, pl.ds(start, width), :]
        vt = vt_ref[0, :, pl.ds(start, width)]
        for c in range(2):
            s = _dot(kt, qc[c]) + bias
            if online:
                m_old = m_ref[c]
                m_new = jnp.maximum(m_old, jnp.max(s, axis=0, keepdims=True))
                alpha = jnp.exp2(m_old - m_new)
                p = jnp.exp2(s - m_new)
                l_ref[c] = alpha * l_ref[c] + _sublane_partial_sum(p)
                acc_ref[c] = alpha * acc_ref[c] + _dot(vt, p.astype(BF16))
                m_ref[c] = m_new
            else:
                p = jnp.exp2(s)
                l_ref[c] += _sublane_partial_sum(p)
                acc_ref[c] += _dot(vt, p.astype(BF16))

    far_bias = bias_ref[15, h] * LOG2E

    n_far = jnp.maximum(i - 1, 0)

    def far_body(jj, carry):
        update(FAR_TILES_PER_TRIP * jj, far_bias, width=FAR_TILES_PER_TRIP * T)
        return carry

    n_trips = n_far // FAR_TILES_PER_TRIP
    lax.fori_loop(0, n_trips, far_body, 0)

    rest = n_far - n_trips * FAR_TILES_PER_TRIP
    tail_tiles = FAR_TILES_PER_TRIP + 1
    for r in range(FAR_TILES_PER_TRIP):
        @pl.when((i >= 1) & (rest == r))
        def _(r=r):
            update(i - 1 - r, bn_ref[(tail_tiles - 2 - r) * T:tail_tiles * T, :], width=(r + 2) * T)

    @pl.when(i == 0)
    def _():
        update(0, bn_ref[(tail_tiles - 1) * T:tail_tiles * T, :])

    o0 = acc_ref[0] / jnp.sum(l_ref[0], axis=0, keepdims=True)
    o1 = acc_ref[1] / jnp.sum(l_ref[1], axis=0, keepdims=True)
    o = o0 - lam_ref[0] * o1
    y = o * lax.rsqrt(jnp.mean(o * o, axis=0, keepdims=True) + EPS) * g_ref[...] * out_scale
    o_ref[...] = y.T.astype(o_ref.dtype)


def _attn_prompt(rel_bias, lam, qth, kh, vth, g_subln, out_scale, online):
    n_heads, Lp, _ = kh.shape
    T = ROW_TILE
    assert T % CHUNK == 0 and T >= FAR_DISTANCE and FAR_TILES_PER_TRIP & (FAR_TILES_PER_TRIP - 1) == 0
    m_scratch = [pltpu.VMEM((2, 1, T), F32)] if online else []
    return pl.pallas_call(
        functools.partial(_attn_prompt_kernel, out_scale=out_scale),
        grid=(n_heads, Lp // T),
        in_specs=[
            pl.BlockSpec(memory_space=pltpu.SMEM),
            pl.BlockSpec(memory_space=pltpu.SMEM),
            pl.BlockSpec((1, V_DIM, T), lambda h, i: (h, 0, i)),
            pl.BlockSpec((1, Lp, V_DIM), lambda h, i: (h, 0, 0)),
            pl.BlockSpec((1, V_DIM, Lp), lambda h, i: (h, 0, 0)),
            pl.BlockSpec((V_DIM, 1), lambda h, i: (0, 0)),
        ],
        out_specs=pl.BlockSpec((T, V_DIM), lambda h, i: (i, h)),
        out_shape=jax.ShapeDtypeStruct((Lp, n_heads * V_DIM), BF16),
        scratch_shapes=[
            pltpu.VMEM(((FAR_TILES_PER_TRIP + 1) * T, T), F32),
            pltpu.VMEM((2, 8, T), F32),
            pltpu.VMEM((2, V_DIM, T), F32),
        ] + m_scratch,
        compiler_params=_params("arbitrary", "arbitrary"),
        name="attn_prompt_online" if online else "attn_prompt",
    )(rel_bias, lam, qth, kh, vth, g_subln)


def _attn_sample_kernel(lam_ref, q_ref, ckt_ref, cv_ref, kn_ref, vn_ref, tab_ref, g_ref,
                        o_ref, nearb_ref, newb_ref, qbd_ref, m_ref, l_ref, acc_ref, *, out_scale):
    b = pl.program_id(0)
    kb = pl.program_id(1)
    nkb = pl.num_programs(1)
    R, Tk = nearb_ref.shape
    L = kn_ref.shape[0]
    D = q_ref.shape[1]
    n_heads = D // V_DIM

    @pl.when((b == 0) & (kb == 0))
    def _():
        tab = lambda bkt: tab_ref[:, bkt:bkt + 1]
        t_near = lax.broadcasted_iota(jnp.int32, (R, Tk), 0) % L
        col = lax.broadcasted_iota(jnp.int32, (R, Tk), 1)
        nearb_ref[...] = _rel_bias_tile(col - Tk - t_near, tab)
        t_new = lax.broadcasted_iota(jnp.int32, (R, L), 0) % L
        col_new = lax.broadcasted_iota(jnp.int32, (R, L), 1)
        newb_ref[...] = _rel_bias_tile(col_new - t_new, tab)

    @pl.when(kb == 0)
    def _():
        m_ref[...] = jnp.full(m_ref.shape, NEG, F32)
        l_ref[...] = jnp.zeros(l_ref.shape, F32)
        acc_ref[...] = jnp.zeros(acc_ref.shape, F32)
        q_rows = jnp.concatenate([q_ref[...]] * (R // L), axis=0)
        row_group = lax.broadcasted_iota(jnp.int32, (R, D), 0) // L
        col_group = lax.broadcasted_iota(jnp.int32, (R, D), 1) // HEAD_DIM
        qbd_ref[...] = jnp.where(row_group == col_group, q_rows, jnp.zeros_like(q_rows))

    rows_per_head = 2 * L

    def update(s, value_of_head):
        m_old = m_ref[...]
        m_new = jnp.maximum(m_old, jnp.max(s, axis=-1, keepdims=True))
        alpha = jnp.exp2(m_old - m_new)
        p = jnp.exp2(s - m_new)
        l_ref[...] = alpha * l_ref[...] + jnp.sum(p, axis=-1, keepdims=True)
        pb = p.astype(BF16)
        for h in range(n_heads):
            rs = slice(h * rows_per_head, (h + 1) * rows_per_head)
            acc_ref[rs, :] = alpha[rs, :] * acc_ref[rs, :] + _dot(pb[rs, :], value_of_head(h))
        m_ref[...] = m_new

    qbd = qbd_ref[...]
    s = _dot(qbd, ckt_ref[0].astype(BF16))
    cache_value = lambda h: cv_ref[0, pl.ds(h, Tk, stride=n_heads), :].astype(BF16)

    @pl.when(kb < nkb - 1)
    def _():
        update(s + tab_ref[:, 15:16], cache_value)

    @pl.when(kb == nkb - 1)
    def _():
        update(s + nearb_ref[...], cache_value)
        s_new = _dot_nt(qbd, kn_ref[...].astype(BF16)) + newb_ref[...]
        update(s_new, lambda h: vn_ref[:, h * V_DIM:(h + 1) * V_DIM].astype(BF16))
        lam = lam_ref[0]
        o = acc_ref[...] / l_ref[...]
        for h in range(n_heads):
            r0 = h * rows_per_head
            o_ref[:, h * V_DIM:(h + 1) * V_DIM] = _diff_out(
                o[r0:r0 + L, :], o[r0 + L:r0 + 2 * L, :], lam, g_ref[...], out_scale).astype(o_ref.dtype)


def _attn_sample(lam, q, cache_k, cache_v, k_new, v_new, tab, g_subln, out_scale):
    B, past, n_heads = cache_v.shape[:3]
    D = n_heads * V_DIM
    L = k_new.shape[0] // B
    R = (D // HEAD_DIM) * L
    Tk = min(2048, past)
    assert past % Tk == 0 and Tk >= FAR_DISTANCE + L
    cache_kt = jnp.transpose(cache_k, (0, 2, 3, 4, 1)).reshape(B, D, past)
    return pl.pallas_call(
        functools.partial(_attn_sample_kernel, out_scale=out_scale),
        grid=(B, past // Tk),
        in_specs=[
            pl.BlockSpec(memory_space=pltpu.SMEM),
            pl.BlockSpec((L, D), lambda b, k: (b, 0)),
            pl.BlockSpec((1, D, Tk), lambda b, k: (b, 0, k)),
            pl.BlockSpec((1, Tk * n_heads, V_DIM), lambda b, k: (b, k, 0)),
            pl.BlockSpec((L, D), lambda b, k: (b, 0)),
            pl.BlockSpec((L, D), lambda b, k: (b, 0)),
            pl.BlockSpec(tab.shape, lambda b, k: (0, 0)),
            pl.BlockSpec((1, V_DIM), lambda b, k: (0, 0)),
        ],
        out_specs=pl.BlockSpec((L, D), lambda b, k: (b, 0)),
        out_shape=jax.ShapeDtypeStruct((B * L, D), BF16),
        scratch_shapes=[
            pltpu.VMEM((R, Tk), F32),
            pltpu.VMEM((R, L), F32),
            pltpu.VMEM((R, D), BF16),
            pltpu.VMEM((R, 1), F32),
            pltpu.VMEM((R, 1), F32),
            pltpu.VMEM((R, V_DIM), F32),
        ],
        compiler_params=_params("arbitrary", "arbitrary"),
        name="attn_sample",
    )(lam, q, cache_kt, cache_v.reshape(B, past * n_heads, V_DIM), k_new, v_new, tab, g_subln)


def _oproj_router_kernel(op_ref, os_ref, x_ref, wo_ref, g_ref, wrh_ref, wrl_ref,
                         x3_ref, h_ref, route_ref, route_t_ref):
    i = pl.program_id(0)
    nt = pl.num_programs(0) - 1
    o = jnp.where(i == nt, os_ref[...], op_ref[...])
    x3 = x_ref[...] + _dot(o, wo_ref[...])
    x3_ref[...] = x3
    h = _rms_unit(x3) * g_ref[...]
    h_hi = h.astype(BF16)
    h_ref[...] = h_hi
    h_lo = (h - h_hi.astype(F32)).astype(BF16)
    logits = _dot(h_hi, wrh_ref[...]) + (_dot(h_lo, wrh_ref[...]) + _dot(h_hi, wrl_ref[...]))
    lt = logits.T[0:N_EXPERTS, :]
    eid = lax.broadcasted_iota(jnp.int32, lt.shape, 0)
    m1 = jnp.max(lt, axis=0, keepdims=True)
    i1 = jnp.min(jnp.where(lt == m1, eid, N_EXPERTS), axis=0, keepdims=True)
    lt2 = jnp.where(eid == i1, -jnp.inf, lt)
    m2 = jnp.max(lt2, axis=0, keepdims=True)
    i2 = jnp.min(jnp.where(lt2 == m2, eid, N_EXPERTS), axis=0, keepdims=True)
    e2 = jnp.exp(m2 - m1)
    den = 1.0 + e2
    g1 = 1.0 / den
    g2 = e2 / den
    row = lax.broadcasted_iota(jnp.int32, route_t_ref.shape, 0)
    route_t = jnp.where(row == 0, i1.astype(F32),
                        jnp.where(row == 1, i2.astype(F32),
                                  jnp.where(row == 2, g1,
                                            jnp.where(row == 3, g2, 0.0))))
    route_t_ref[...] = route_t
    route_ref[...] = route_t.T


def _oproj_router(o_p, o_s, x, wo, g, wr_hi, wr_lo):
    N, D = x.shape
    T = ROW_TILE
    nt = N // T - 1
    const2 = lambda i: (0, 0)
    row = lambda i: (i, 0)
    return pl.pallas_call(
        _oproj_router_kernel,
        grid=(nt + 1,),
        in_specs=[
            pl.BlockSpec((T, D), lambda i: (jnp.minimum(i, nt - 1), 0)),
            pl.BlockSpec((T, D), const2),
            pl.BlockSpec((T, D), row),
            pl.BlockSpec((D, D), const2),
            pl.BlockSpec((1, D), const2),
            pl.BlockSpec((D, LANE), const2),
            pl.BlockSpec((D, LANE), const2),
        ],
        out_specs=[
            pl.BlockSpec((T, D), row),
            pl.BlockSpec((T, D), row),
            pl.BlockSpec((T, LANE), row),
            pl.BlockSpec((LANE, T), lambda i: (0, i)),
        ],
        out_shape=[
            jax.ShapeDtypeStruct((N, D), F32),
            jax.ShapeDtypeStruct((N, D), BF16),
            jax.ShapeDtypeStruct((N, LANE), F32),
            jax.ShapeDtypeStruct((LANE, N), F32),
        ],
        compiler_params=_params("arbitrary"),
        name="oproj_router",
    )(o_p, o_s, x, wo, g, wr_hi, wr_lo)


MOE_CHUNK = 128
MOE_ROW_ALIGN = 16


def _for_each_chunk(cnt_ref, step, n_chunks, fn, first=0):
    for e in range(N_EXPERTS):
        for k in range(first, n_chunks):
            if k == 0:
                fn(e, k)
            else:
                @pl.when(cnt_ref[step * N_EXPERTS + e] > k * MOE_CHUNK)
                def _(e=e, k=k):
                    fn(e, k)


def _dispatch_kernel(off_ref, cnt_ref, hb_ref, rt_ref, utri_ref, xs_in_ref, xs_ref, stage_ref, sem):
    del xs_in_ref
    t = pl.program_id(0)
    T = hb_ref.shape[0]
    S = MOE_CHUNK
    nk = T // S

    def chunk_copy(step, e, k):
        start = pl.multiple_of(off_ref[step * N_EXPERTS + e] + k * S, MOE_ROW_ALIGN)
        slot = k * N_EXPERTS + e
        return pltpu.make_async_copy(stage_ref.at[slot], xs_ref.at[pl.ds(start, S), :], sem.at[slot])

    rt = rt_ref[...]
    expert_id = lax.broadcasted_iota(jnp.int32, (N_EXPERTS, T), 0).astype(F32)
    member = (rt[0:1, :] == expert_id) | (rt[1:2, :] == expert_id)
    rank = _dot(member.astype(BF16), utri_ref[...])
    rank = jnp.where(member, rank, 0.0)
    hb = hb_ref[...]

    row3 = lax.broadcasted_iota(jnp.int32, (1, S, 1), 1)
    sel_first = (rank[:, None, :] == (row3 + 1).astype(F32)).astype(BF16).reshape(N_EXPERTS * S, T)
    first = _dot(sel_first, hb).astype(BF16).reshape(N_EXPERTS, S, hb.shape[1])

    @pl.when(t > 0)
    def _():
        _for_each_chunk(cnt_ref, t - 1, nk, lambda e, k: chunk_copy(t - 1, e, k).wait())

    stage_ref[0:N_EXPERTS] = first
    for e in range(N_EXPERTS):
        chunk_copy(t, e, 0).start()

    row = lax.broadcasted_iota(jnp.int32, (S, 1), 0)

    def emit(e, k):
        sel = rank[e:e + 1, :] == (row + (k * S + 1)).astype(F32)
        stage_ref[k * N_EXPERTS + e] = _dot(sel.astype(BF16), hb).astype(BF16)
        chunk_copy(t, e, k).start()

    _for_each_chunk(cnt_ref, t, nk, emit, first=1)

    @pl.when(t == pl.num_programs(0) - 1)
    def _():
        _for_each_chunk(cnt_ref, t, nk, lambda e, k: chunk_copy(t, e, k).wait())


def _dispatch(off, cnt, hb, route_t, utri, n_rows):
    N, D = hb.shape
    T = ROW_TILE
    nk = T // MOE_CHUNK
    n_slots = N_EXPERTS * nk
    xs_init = jnp.zeros((n_rows, D), BF16)
    return pl.pallas_call(
        _dispatch_kernel,
        grid_spec=pltpu.PrefetchScalarGridSpec(
            num_scalar_prefetch=2,
            grid=(N // T,),
            in_specs=[
                pl.BlockSpec((T, D), lambda t, off, cnt: (t, 0)),
                pl.BlockSpec((LANE, T), lambda t, off, cnt: (0, t)),
                pl.BlockSpec((T, T), lambda t, off, cnt: (0, 0)),
                pl.BlockSpec(memory_space=pl.ANY),
            ],
            out_specs=pl.BlockSpec(memory_space=pl.ANY),
            scratch_shapes=[
                pltpu.VMEM((n_slots, MOE_CHUNK, D), BF16),
                pltpu.SemaphoreType.DMA((n_slots,)),
            ],
        ),
        out_shape=jax.ShapeDtypeStruct((n_rows, D), BF16),
        input_output_aliases={5: 0},
        compiler_params=_params("arbitrary"),
        name="moe_dispatch",
    )(off, cnt, hb, route_t, utri, xs_init)


def _experts_kernel(te_ref, na_ref, xs_ref, wg_ref, wu_ref, wd_ref, out_ref, acc_ref):
    i = pl.program_id(0)
    f = pl.program_id(1)
    last = pl.num_programs(1) - 1
    active = i < na_ref[0]

    @pl.when(active)
    def _():
        @pl.when(f == 0)
        def _():
            acc_ref[...] = jnp.zeros(acc_ref.shape, F32)

        acc_ref[...] += _swiglu_tile(xs_ref[...], wg_ref.at[0], wu_ref.at[0], wd_ref.at[0])

        @pl.when(f == last)
        def _():
            out_ref[...] = acc_ref[...].astype(out_ref.dtype)

    @pl.when(jnp.logical_not(active) & (f == last))
    def _():
        out_ref[...] = jnp.zeros(out_ref.shape, out_ref.dtype)


def _experts(tile_expert, n_active, xs, wg, wu, wd):
    P, D = xs.shape
    T = ROW_TILE
    nf, tf = _ffn_splits(wg.shape[2])
    fidx = lambda i, f, na: jnp.where(i < na[0], f, nf - 1)
    return pl.pallas_call(
        _experts_kernel,
        grid_spec=pltpu.PrefetchScalarGridSpec(
            num_scalar_prefetch=2,
            grid=(P // T, nf),
            in_specs=[
                pl.BlockSpec((T, D), lambda i, f, te, na: (i, 0)),
                pl.BlockSpec((1, D, tf), lambda i, f, te, na: (te[i], 0, fidx(i, f, na))),
                pl.BlockSpec((1, D, tf), lambda i, f, te, na: (te[i], 0, fidx(i, f, na))),
                pl.BlockSpec((1, tf, D), lambda i, f, te, na: (te[i], fidx(i, f, na), 0)),
            ],
            out_specs=pl.BlockSpec((T, D), lambda i, f, te, na: (i, 0)),
            scratch_shapes=[pltpu.VMEM((T, D), F32)],
        ),
        out_shape=jax.ShapeDtypeStruct((P, D), BF16),
        compiler_params=_params("arbitrary", "arbitrary"),
        name="moe_experts",
    )(tile_expert, n_active, xs, wg, wu, wd)


def _combine_kernel(off_ref, cnt_ref, x_ref, route_ref, ltri_ref, ys_hbm, yp_ref, ysm_ref,
                    buf_ref, acc_ref, sem):
    t = pl.program_id(0)
    nt = pl.num_programs(0) - 1
    T = x_ref.shape[0]
    S = MOE_CHUNK
    nk = T // S

    def chunk_copy(step, e, k):
        start = pl.multiple_of(off_ref[step * N_EXPERTS + e] + k * S, MOE_ROW_ALIGN)
        bank, slot = step % 2, k * N_EXPERTS + e
        return pltpu.make_async_copy(ys_hbm.at[pl.ds(start, S), :], buf_ref.at[bank, slot], sem.at[bank, slot])

    @pl.when(t == 0)
    def _():
        _for_each_chunk(cnt_ref, t, nk, lambda e, k: chunk_copy(t, e, k).start())

    @pl.when(t < nt)
    def _():
        _for_each_chunk(cnt_ref, t + 1, nk, lambda e, k: chunk_copy(t + 1, e, k).start())

    route = route_ref[...]
    e1, e2, g1, g2 = (route[:, j:j + 1] for j in range(2 * TOP_K))
    lane = lax.broadcasted_iota(jnp.int32, route.shape, 1).astype(F32)
    member = (lane == e1) | (lane == e2)
    rank = _dot(ltri_ref[...], member.astype(BF16))
    rank = jnp.where(member, rank, 0.0)
    col = lax.broadcasted_iota(jnp.int32, (1, S), 1)
    gate_of = lambda e: jnp.where(e1 == e, g1, 0.0) + jnp.where(e2 == e, g2, 0.0)
    select = lambda e, k: rank[:, e:e + 1] == (col + (k * S + 1)).astype(F32)

    for e in range(N_EXPERTS):
        chunk_copy(t, e, 0).wait()
    weight = jnp.concatenate([jnp.where(select(e, 0), gate_of(e), 0.0) for e in range(N_EXPERTS)], axis=-1)
    w_hi = weight.astype(BF16)
    w_lo = (weight - w_hi.astype(F32)).astype(BF16)
    rows = buf_ref[t % 2, 0:N_EXPERTS].reshape(N_EXPERTS * S, x_ref.shape[1])
    acc_ref[...] = x_ref[...] + (_dot(w_hi, rows) + _dot(w_lo, rows))

    def absorb(e, k):
        chunk_copy(t, e, k).wait()
        acc_ref[...] += gate_of(e) * _dot(select(e, k).astype(BF16), buf_ref[t % 2, k * N_EXPERTS + e])

    _for_each_chunk(cnt_ref, t, nk, absorb, first=1)

    @pl.when(t < nt)
    def _():
        yp_ref[...] = acc_ref[...]

    @pl.when(t == nt)
    def _():
        ysm_ref[...] = acc_ref[...]


def _combine(off, cnt, x3, route, ltri, ys, n_sample):
    N, D = x3.shape
    T = ROW_TILE
    nt = N // T - 1
    n_slots = N_EXPERTS * (T // MOE_CHUNK)
    return pl.pallas_call(
        _combine_kernel,
        grid_spec=pltpu.PrefetchScalarGridSpec(
            num_scalar_prefetch=2,
            grid=(nt + 1,),
            in_specs=[
                pl.BlockSpec((T, D), lambda t, off, cnt: (t, 0)),
                pl.BlockSpec((T, LANE), lambda t, off, cnt: (t, 0)),
                pl.BlockSpec((T, T), lambda t, off, cnt: (0, 0)),
                pl.BlockSpec(memory_space=pl.ANY),
            ],
            out_specs=[
                pl.BlockSpec((T, D), lambda t, off, cnt: (jnp.minimum(t, nt - 1), 0)),
                pl.BlockSpec((T, D), lambda t, off, cnt: (0, 0)),
            ],
            scratch_shapes=[
                pltpu.VMEM((2, n_slots, MOE_CHUNK, D), BF16),
                pltpu.VMEM((T, D), F32),
                pltpu.SemaphoreType.DMA((2, n_slots)),
            ],
        ),
        out_shape=[
            jax.ShapeDtypeStruct((N - n_sample, D), F32),
            jax.ShapeDtypeStruct((n_sample, D), F32),
        ],
        compiler_params=_params("arbitrary"),
        name="moe_combine",
    )(off, cnt, x3, route, ltri, ys)


def _routing_tables(route, tile):
    n = route.shape[0]
    n_tok_tiles = n // tile
    experts = route[:, :TOP_K].astype(jnp.int32)
    onehot = (experts[:, :, None] == jnp.arange(N_EXPERTS)[None, None, :]).astype(jnp.int32).sum(axis=1)
    cnt = onehot.reshape(n_tok_tiles, tile, N_EXPERTS).sum(axis=1)
    span = (cnt + MOE_ROW_ALIGN - 1) // MOE_ROW_ALIGN * MOE_ROW_ALIGN
    totals = span.sum(axis=0)
    tiles_per = (totals + MOE_CHUNK + tile - 1) // tile
    tile_end = jnp.cumsum(tiles_per)
    group_off = (tile_end - tiles_per) * tile
    off = group_off[None, :] + jnp.cumsum(span, axis=0) - span
    max_rows = n * TOP_K + N_EXPERTS * (MOE_CHUNK + n_tok_tiles * (MOE_ROW_ALIGN - 1))
    n_row_tiles = max_rows // tile + N_EXPERTS
    n_active = tile_end[-1:].astype(jnp.int32)
    tile_ids = jnp.minimum(jnp.arange(n_row_tiles, dtype=jnp.int32), n_active[0] - 1)
    tile_expert = jnp.sum((tile_ids[:, None] >= tile_end[None, :]).astype(jnp.int32), axis=1)
    return (off.reshape(-1).astype(jnp.int32), cnt.reshape(-1).astype(jnp.int32),
            tile_expert, n_active, n_row_tiles * tile)


def _group_matrices(d_model):
    n_groups = d_model // HEAD_DIM
    gsum = np.zeros((d_model, LANE), np.float32)
    gsum[np.arange(d_model), np.arange(d_model) // HEAD_DIM] = 1.0
    assert n_groups <= LANE
    return jnp.asarray(gsum, BF16), jnp.asarray(np.concatenate([gsum.T, gsum.T], axis=0), BF16)


def kernel(x_prompt, x_sample, cache_k, cache_v, state_pool, g_pool_norm, w_pool, pool_scale, g_attn, w_q, g_qn, lambda_q1, lambda_k1, lambda_q2, lambda_k2, g_subln, w_o, g_kv, w_k, w_v, g_kn, rel_bias, g_ffn, w_gate_dense, w_up_dense, w_down_dense, w_router, w_gate_moe, w_up_moe, w_down_moe):
    Bp, Lp, D = x_prompt.shape
    Bs, Ls, _ = x_sample.shape
    past = cache_k.shape[1]
    n_heads = D // V_DIM
    n_sample = Bs * Ls
    assert Bp == 1 and n_sample == ROW_TILE and Lp % ROW_TILE == 0
    assert g_pool_norm.shape[0] == 1 and g_attn.shape[0] == 1
    bf = lambda a: a.astype(BF16)
    row = lambda a: a.reshape(1, -1)

    x1, pool_p, pool_s = _pool_layer(
        x_prompt.reshape(Lp, D), x_sample.reshape(n_sample, D), state_pool[0],
        row(g_pool_norm[0]), bf(w_pool[0]), row(pool_scale[0]), past)
    x2 = _dense_ffn(x1, row(g_ffn[0]), bf(w_gate_dense[0]), bf(w_up_dense[0]), bf(w_down_dense[0]))

    layer = 1
    lambda_init = 0.8 - 0.6 * math.exp(-0.3 * layer)
    lam = (jnp.exp(jnp.sum(lambda_q1[0] * lambda_k1[0])) - jnp.exp(jnp.sum(lambda_q2[0] * lambda_k2[0]))
           + lambda_init).reshape(1)
    gsum, gbc = _group_matrices(D)
    n_groups = D // HEAD_DIM
    k_p, v_p, k_s, v_s, qh, kh, vh, q_s = _qkv_proj(
        x2, row(g_kv), row(g_attn[0]), bf(w_k), bf(w_v), bf(w_q[0]),
        row(jnp.tile(g_kn, n_groups)), row(jnp.tile(g_qn[0], n_groups) * (HEAD_DIM ** -0.5 * LOG2E)),
        gsum, gbc, n_sample)
    out_scale = 1.0 - lambda_init
    g_sub = row(g_subln[0])
    score_bound = LOG2E * (math.sqrt(HEAD_DIM) * jnp.max(jnp.abs(g_qn[0])) * jnp.max(jnp.abs(g_kn))
                           + jnp.max(jnp.abs(rel_bias)))
    o_p = lax.cond(
        score_bound <= UNSHIFTED_SCORE_LIMIT,
        functools.partial(_attn_prompt, out_scale=out_scale, online=False),
        functools.partial(_attn_prompt, out_scale=out_scale, online=True),
        rel_bias, lam, qh, kh, vh, g_subln[0].reshape(V_DIM, 1))

    tab = jnp.repeat(rel_bias.T, 2 * Ls, axis=0) * LOG2E
    o_s = _attn_sample(lam, q_s, cache_k, cache_v, k_s, v_s, tab, g_sub, out_scale)

    wr = jnp.pad(w_router[0], ((0, 0), (0, LANE - N_EXPERTS)))
    wr_hi = bf(wr)
    wr_lo = bf(wr - wr_hi.astype(F32))
    x3, h_moe, route, route_t = _oproj_router(o_p, o_s, x2, bf(w_o[0]), row(g_ffn[1]), wr_hi, wr_lo)
    off, cnt, tile_expert, n_active, n_rows = _routing_tables(route, ROW_TILE)
    ltri = jnp.asarray(np.tril(np.ones((ROW_TILE, ROW_TILE), np.float32)), BF16)
    xs = _dispatch(off, cnt, h_moe, route_t, ltri.T, n_rows)
    ys = _experts(tile_expert, n_active, xs, bf(w_gate_moe[0]), bf(w_up_moe[0]), bf(w_down_moe[0]))
    y_p, y_s = _combine(off, cnt, x3, route, ltri, ys, n_sample)

    return (y_p.reshape(Bp, Lp, D), y_s.reshape(Bs, Ls, D),
            k_p.reshape(Bp, Lp, n_heads, 2, HEAD_DIM), v_p.reshape(Bp, Lp, n_heads, V_DIM),
            pool_p.reshape(1, Bp, POOL_STATE, D),
            k_s.reshape(Bs, Ls, n_heads, 2, HEAD_DIM), v_s.reshape(Bs, Ls, n_heads, V_DIM),
            pool_s.reshape(1, Bs, POOL_STATE, D))
```
